```python
import math
import jax
import jax.numpy as jnp
from jax import lax
import numpy as np

D_MODEL = 2048
BATCH = 8
SEQ = 2048
DEPTH = 1
DEC_BATCH = 32
DEC_SEQ = 4
PAST_LEN = 16384
PAGE_SIZE = 128

N_META = 16
HEAD_DIM = 64
RWKV_WIDTH = D_MODEL // 2
ATTN_WIDTH = D_MODEL - RWKV_WIDTH
RWKV_HEADS = RWKV_WIDTH // HEAD_DIM
ATTN_HEADS = ATTN_WIDTH // HEAD_DIM
KV_HEADS = 4
GROUP = ATTN_HEADS // KV_HEADS
KV_WIDTH = KV_HEADS * HEAD_DIM
DECAY_LORA = 64
ICLR_LORA = 64
GATE_LORA = 160
RWKV_COLS = 3 * RWKV_WIDTH + DECAY_LORA + ICLR_LORA + GATE_LORA
RWKV_SPLITS = (RWKV_WIDTH, 2 * RWKV_WIDTH, 3 * RWKV_WIDTH, 3 * RWKV_WIDTH + DECAY_LORA,
               3 * RWKV_WIDTH + DECAY_LORA + ICLR_LORA)
GN_EPS = 64e-5
IDX_HEADS = 8
IDX_DIM = 128
IDX_TOPK = 256
ATTN_COLS = ATTN_WIDTH + 2 * KV_WIDTH + IDX_HEADS * IDX_DIM + IDX_DIM + IDX_HEADS
ATTN_SPLITS = (ATTN_WIDTH, ATTN_WIDTH + KV_WIDTH, ATTN_WIDTH + 2 * KV_WIDTH,
               ATTN_WIDTH + 2 * KV_WIDTH + IDX_HEADS * IDX_DIM,
               ATTN_WIDTH + 2 * KV_WIDTH + IDX_HEADS * IDX_DIM + IDX_DIM)
IN_COLS = RWKV_COLS + ATTN_COLS
ATTN_Q_BLOCK = 16
REL_BUCKETS = 32
REL_MAX_DIST = 128
PEER_HEADS = 8
PEER_NKEYS = 128
PEER_EXPERTS = PEER_NKEYS * PEER_NKEYS
PEER_QUERY = 256
PEER_HALF = PEER_QUERY // 2
PEER_TOPK = 16
PEER_BLOCK = 128
DN_ALPHA = (2 * DEPTH) ** 0.25
DN_BETA = (8 * DEPTH) ** -0.25
LN_EPS = 1e-5

kernel_name = 'hymba_rwkv7_dsa_peer_decode_step'


def layer_norm(x, g, b):
    xf = x.astype(jnp.float32)
    mu = jnp.mean(xf, axis=-1, keepdims=True)
    var = jnp.mean(jnp.square(xf - mu), axis=-1, keepdims=True)
    y = (xf - mu) * lax.rsqrt(var + LN_EPS) * g.astype(jnp.float32) + b.astype(jnp.float32)
    return y.astype(x.dtype)


def rel_bucket(dist):
    max_exact = REL_BUCKETS // 2
    d = jnp.maximum(dist, 0)
    log_b = max_exact + (jnp.log(jnp.maximum(d, 1).astype(jnp.float32) / max_exact)
                         / math.log(REL_MAX_DIST / max_exact) * (REL_BUCKETS - max_exact)).astype(jnp.int32)
    return jnp.where(d < max_exact, d, jnp.minimum(log_b, REL_BUCKETS - 1))


def take_rows(a, idx):
    return jax.vmap(lambda rows, i: rows[i])(a, idx)


def rwkv7_mix(feat, shift_prev, wkv_prev, mu, w0, w_up, a0, a_up, g_up, k_k, k_a, r_k, gn_g, gn_b):
    f32 = jnp.float32
    B, L, _ = feat.shape
    prev = jnp.concatenate([shift_prev[:, None].astype(feat.dtype), feat[:, :-1]], axis=1)
    xm = feat + (prev - feat) * mu
    r, k, v, w_lo, a_lo, g_lo = jnp.split(xm, list(RWKV_SPLITS), axis=-1)
    logw = -jax.nn.softplus(-(w0 + jnp.tanh(w_lo) @ w_up).astype(f32)) - 0.5
    decay = jnp.exp(-jnp.exp(logw))
    a = jax.nn.sigmoid((a0 + a_lo @ a_up).astype(f32))
    g = (jax.nn.sigmoid(g_lo) @ g_up).astype(f32)
    hs = (B, L, RWKV_HEADS, HEAD_DIM)
    kk = (k.astype(f32) * k_k.astype(f32)).reshape(hs)
    kk = kk / jnp.maximum(jnp.linalg.norm(kk, axis=-1, keepdims=True), 1e-12)
    k_h = (k.astype(f32) * (1.0 + (a - 1.0) * k_a.astype(f32))).reshape(hs)
    r_h = r.astype(f32).reshape(hs)
    v_h = v.astype(f32).reshape(hs)
    w_h = decay.reshape(hs)
    b_h = kk * a.reshape(hs)

    def step(S, inp):
        r_t, w_t, k_t, v_t, kk_t, b_t = inp
        s_kk = jnp.einsum('bhvk,bhk->bhv', S, kk_t)
        S = (S * w_t[:, :, None, :] - s_kk[..., None] * b_t[:, :, None, :]
             + v_t[..., None] * k_t[:, :, None, :])
        return S, jnp.einsum('bhvk,bhk->bhv', S, r_t)

    seq_in = tuple(jnp.moveaxis(t, 1, 0) for t in (r_h, w_h, k_h, v_h, kk, b_h))
    s_fin, y = lax.scan(step, wkv_prev.astype(f32), seq_in)
    y = jnp.moveaxis(y, 0, 1)
    y_mu = jnp.mean(y, axis=-1, keepdims=True)
    y_var = jnp.mean(jnp.square(y - y_mu), axis=-1, keepdims=True)
    y = ((y - y_mu) * lax.rsqrt(y_var + GN_EPS)).reshape(B, L, RWKV_WIDTH)
    y = y * gn_g.astype(f32) + gn_b.astype(f32)
    bonus = jnp.sum(r_h * k_h * r_k.astype(f32), axis=-1, keepdims=True) * v_h
    y = (y + bonus.reshape(B, L, RWKV_WIDTH)) * g
    return y.astype(feat.dtype), feat[:, -1], s_fin.astype(wkv_prev.dtype)


def split_attn(feat):
    B, L, _ = feat.shape
    q, k, v, qi, ki, wi = jnp.split(feat, list(ATTN_SPLITS), axis=-1)
    return (q.reshape(B, L, KV_HEADS, GROUP, HEAD_DIM), k.reshape(B, L, KV_HEADS, HEAD_DIM),
            v.reshape(B, L, KV_HEADS, HEAD_DIM), qi.reshape(B, L, IDX_HEADS, IDX_DIM), ki, wi)


def index_scores(q_idx, w_idx, k_idx):
    f32 = jnp.float32
    rel = jax.nn.relu(jnp.einsum('bqhd,bsd->bqhs', q_idx, k_idx, preferred_element_type=f32))
    return jnp.einsum('bqh,bqhs->bqs', w_idx.astype(f32), rel)


def sparse_attend(q, k_sel, v_sel, q_pos, sel_pos, rel_bias):
    f32 = jnp.float32
    B, Q, K = sel_pos.shape
    dist = q_pos[None, :, None] - sel_pos
    bias = rel_bias[rel_bucket(dist)].astype(f32)
    bias = jnp.moveaxis(bias, 2, -1).reshape(B, Q, KV_HEADS, GROUP, K)
    s = jnp.einsum('bqngd,bqknd->bqngk', q, k_sel, preferred_element_type=f32) * (HEAD_DIM ** -0.5) + bias
    s = jnp.where((dist >= 0)[:, :, None, None, :], s, -jnp.inf)
    p = jax.nn.softmax(s, axis=-1)
    o = jnp.einsum('bqngk,bqknd->bqngd', p.astype(v_sel.dtype), v_sel)
    return o.reshape(B, Q, ATTN_WIDTH)


def prompt_sparse_attention(q, k, v, q_idx, k_idx, w_idx, rel_bias):
    B, T = q.shape[:2]
    nb = T // ATTN_Q_BLOCK
    n_sel = min(IDX_TOPK, T // 4)
    key_pos = jnp.arange(T, dtype=jnp.int32)

    def blocks(t):
        return jnp.moveaxis(t.reshape(B, nb, ATTN_Q_BLOCK, *t.shape[2:]), 1, 0)

    def one_block(args):
        qb, qib, wb, qpos = args
        sc = index_scores(qib, wb, k_idx)
        sc = jnp.where(key_pos[None, None, :] <= qpos[None, :, None], sc, -jnp.inf)
        _, sel = lax.top_k(sc, n_sel)
        return sparse_attend(qb, take_rows(k, sel), take_rows(v, sel), qpos, sel, rel_bias)

    out = lax.map(one_block, (blocks(q), blocks(q_idx), blocks(w_idx), key_pos.reshape(nb, ATTN_Q_BLOCK)))
    return jnp.moveaxis(out, 0, 1).reshape(B, T, ATTN_WIDTH)


def sample_sparse_attention(q, k_new, v_new, q_idx, kidx_new, w_idx, cache_k, cache_v, cache_kidx,
                            page_table, layer, rel_bias):
    Bd, S = q.shape[:2]
    L = PAST_LEN + S
    n_sel = min(IDX_TOPK, L // 4)
    kidx_past = cache_kidx[layer, page_table].reshape(Bd, PAST_LEN, IDX_DIM)
    kidx_all = jnp.concatenate([kidx_past.astype(kidx_new.dtype), kidx_new], axis=1)
    q_pos = PAST_LEN + jnp.arange(S, dtype=jnp.int32)
    key_pos = jnp.arange(L, dtype=jnp.int32)
    sc = index_scores(q_idx, w_idx, kidx_all)
    sc = jnp.where(key_pos[None, None, :] <= q_pos[None, :, None], sc, -jnp.inf)
    _, sel = lax.top_k(sc, n_sel)
    past = jnp.minimum(sel, PAST_LEN - 1)
    phys = jnp.take_along_axis(page_table, (past // PAGE_SIZE).reshape(Bd, -1), axis=1).reshape(sel.shape)
    off = past % PAGE_SIZE
    new = jnp.clip(sel - PAST_LEN, 0, S - 1)
    is_new = (sel >= PAST_LEN)[..., None, None]
    k_sel = jnp.where(is_new, take_rows(k_new, new), cache_k[layer, phys, off].astype(k_new.dtype))
    v_sel = jnp.where(is_new, take_rows(v_new, new), cache_v[layer, phys, off].astype(v_new.dtype))
    return sparse_attend(q, k_sel, v_sel, q_pos, sel, rel_bias)


def peer_ffn(x, wq, subkeys, u_tab, v_tab):
    f32 = jnp.float32
    shp = x.shape
    xf = x.reshape(-1, D_MODEL)
    n = xf.shape[0]
    q = (xf @ wq).astype(f32).reshape(n, PEER_HEADS, 2, PEER_HALF)
    s = jnp.einsum('nhpd,hpkd->nhpk', q, subkeys.astype(f32))
    top_s, top_i = lax.top_k(s, PEER_TOPK)
    cand_s = (top_s[:, :, 0, :, None] + top_s[:, :, 1, None, :]).reshape(n, PEER_HEADS, PEER_TOPK * PEER_TOPK)
    cand_i = (top_i[:, :, 0, :, None] * PEER_NKEYS + top_i[:, :, 1, None, :]).reshape(n, PEER_HEADS, PEER_TOPK * PEER_TOPK)
    best_s, best_j = lax.top_k(cand_s, PEER_TOPK)
    expert = jnp.take_along_axis(cand_i, best_j, axis=-1)
    gate = jax.nn.softmax(best_s, axis=-1)
    n_pad = (-n) % PEER_BLOCK
    nb = (n + n_pad) // PEER_BLOCK

    def pad(t):
        t = jnp.pad(t, [(0, n_pad)] + [(0, 0)] * (t.ndim - 1))
        return t.reshape(nb, PEER_BLOCK, *t.shape[1:])

    def one_block(args):
        xb, eb, gb = args
        h = jax.nn.gelu(jnp.einsum('nd,nhkd->nhk', xb, u_tab[eb], preferred_element_type=f32), approximate=False)
        return jnp.einsum('nhk,nhkd->nd', (gb * h).astype(xb.dtype), v_tab[eb])

    y = lax.map(one_block, (pad(xf), pad(expert), pad(gate)))
    return y.reshape(-1, D_MODEL)[:n].reshape(shp)


def finish_layer(x, y_rwkv, y_attn, w_o, ln1_g, ln1_b, peer_wq, peer_subkeys, peer_u, peer_v, ln2_g, ln2_b):
    mix = jnp.concatenate([y_rwkv, y_attn], axis=-1) @ w_o
    x = layer_norm(DN_ALPHA * x + mix, ln1_g, ln1_b)
    return layer_norm(DN_ALPHA * x + peer_ffn(x, peer_wq, peer_subkeys, peer_u, peer_v), ln2_g, ln2_b)


def setup_inputs(seed: int = 0) -> dict:
    key = jax.random.key(seed)
    keys = list(jax.random.split(key, 48))
    f32 = jnp.float32

    def nrm(shape, scale):
        return jax.random.normal(keys.pop(), shape, f32) * scale

    def unif(shape, lo, hi):
        return jax.random.uniform(keys.pop(), shape, f32, lo, hi)

    n_pages = PAST_LEN // PAGE_SIZE
    used = DEC_BATCH * n_pages
    n_pool = used + max(1, used // 4)
    T = DEPTH
    return {
        'x_prompt': nrm((BATCH, SEQ, D_MODEL), 1.0),
        'x_sample': nrm((DEC_BATCH, DEC_SEQ, D_MODEL), 1.0),
        'cache_k': nrm((T, n_pool, PAGE_SIZE, KV_HEADS, HEAD_DIM), 1.0),
        'cache_v': nrm((T, n_pool, PAGE_SIZE, KV_HEADS, HEAD_DIM), 1.0),
        'cache_kidx': nrm((T, n_pool, PAGE_SIZE, IDX_DIM), 1.0),
        'state_wkv': nrm((T, DEC_BATCH, RWKV_HEADS, HEAD_DIM, HEAD_DIM), 0.5),
        'state_shift': nrm((T, DEC_BATCH, RWKV_COLS), 1.0),
        'page_table': jax.random.permutation(keys.pop(), n_pool)[:used].reshape(DEC_BATCH, n_pages).astype(jnp.int32),
        'meta_tokens': nrm((N_META, D_MODEL), 1.0),
        'ln_in_g': 1.0 + nrm((D_MODEL,), 0.02),
        'ln_in_b': nrm((D_MODEL,), 0.02),
        'rel_bias': nrm((REL_BUCKETS, ATTN_HEADS), 0.5),
        'w_in': nrm((T, D_MODEL, IN_COLS), D_MODEL ** -0.5),
        'mu_shift': unif((T, RWKV_COLS), 0.0, 1.0),
        'w0': unif((T, RWKV_WIDTH), -6.5, -1.5),
        'w_up': nrm((T, DECAY_LORA, RWKV_WIDTH), 0.5 * DECAY_LORA ** -0.5),
        'a0': nrm((T, RWKV_WIDTH), 0.1),
        'a_up': nrm((T, ICLR_LORA, RWKV_WIDTH), 0.5 * ICLR_LORA ** -0.5),
        'g_up': nrm((T, GATE_LORA, RWKV_WIDTH), GATE_LORA ** -0.5),
        'k_k': 0.85 + nrm((T, RWKV_WIDTH), 0.05),
        'k_a': 1.0 + nrm((T, RWKV_WIDTH), 0.05),
        'r_k': nrm((T, RWKV_HEADS, HEAD_DIM), 0.1),
        'gn_g': 1.0 + nrm((T, RWKV_WIDTH), 0.02),
        'gn_b': nrm((T, RWKV_WIDTH), 0.02),
        'w_o': nrm((T, RWKV_WIDTH + ATTN_WIDTH, D_MODEL), DN_BETA * (RWKV_WIDTH + ATTN_WIDTH) ** -0.5),
        'ln1_g': 1.0 + nrm((T, D_MODEL), 0.02),
        'ln1_b': nrm((T, D_MODEL), 0.02),
        'peer_wq': nrm((T, D_MODEL, PEER_HEADS * PEER_QUERY), D_MODEL ** -0.5),
        'peer_subkeys': nrm((T, PEER_HEADS, 2, PEER_NKEYS, PEER_HALF), PEER_HALF ** -0.5),
        'peer_u': nrm((T, PEER_EXPERTS, D_MODEL), D_MODEL ** -0.5),
        'peer_v': nrm((T, PEER_EXPERTS, D_MODEL), DN_BETA * PEER_HEADS ** -0.5),
        'ln2_g': 1.0 + nrm((T, D_MODEL), 0.02),
        'ln2_b': nrm((T, D_MODEL), 0.02),
    }


def reference(x_prompt, x_sample, cache_k, cache_v, cache_kidx, state_wkv, state_shift, page_table,
              meta_tokens, ln_in_g, ln_in_b, rel_bias, w_in, mu_shift, w0, w_up, a0, a_up, g_up,
              k_k, k_a, r_k, gn_g, gn_b, w_o, ln1_g, ln1_b, peer_wq, peer_subkeys, peer_u, peer_v,
              ln2_g, ln2_b):
    B = x_prompt.shape[0]
    meta = jnp.broadcast_to(meta_tokens[None].astype(x_prompt.dtype), (B, N_META, D_MODEL))
    x_p = layer_norm(jnp.concatenate([meta, x_prompt], axis=1), ln_in_g, ln_in_b)
    x_s = layer_norm(x_sample, ln_in_g, ln_in_b)
    nk_p, nv_p, nki_p, nwkv_p, nsh_p = [], [], [], [], []
    nk_s, nv_s, nki_s, nwkv_s, nsh_s = [], [], [], [], []
    for l in range(DEPTH):
        rw = (mu_shift[l], w0[l], w_up[l], a0[l], a_up[l], g_up[l], k_k[l], k_a[l], r_k[l], gn_g[l], gn_b[l])
        fin = (w_o[l], ln1_g[l], ln1_b[l], peer_wq[l], peer_subkeys[l], peer_u[l], peer_v[l], ln2_g[l], ln2_b[l])
        proj = x_p @ w_in[l]
        y_r, sh_p, wkv_p = rwkv7_mix(proj[..., :RWKV_COLS], jnp.zeros((B, RWKV_COLS), proj.dtype),
                                     jnp.zeros((B, RWKV_HEADS, HEAD_DIM, HEAD_DIM), jnp.float32), *rw)
        q, k, v, qi, ki, wi = split_attn(proj[..., RWKV_COLS:])
        y_a = prompt_sparse_attention(q, k, v, qi, ki, wi, rel_bias)
        x_p = finish_layer(x_p, y_r, y_a, *fin)
        nk_p.append(k); nv_p.append(v); nki_p.append(ki); nwkv_p.append(wkv_p); nsh_p.append(sh_p)
        proj = x_s @ w_in[l]
        y_r, sh_s, wkv_s = rwkv7_mix(proj[..., :RWKV_COLS], state_shift[l], state_wkv[l], *rw)
        q, k, v, qi, ki, wi = split_attn(proj[..., RWKV_COLS:])
        y_a = sample_sparse_attention(q, k, v, qi, ki, wi, cache_k, cache_v, cache_kidx, page_table, l, rel_bias)
        x_s = finish_layer(x_s, y_r, y_a, *fin)
        nk_s.append(k); nv_s.append(v); nki_s.append(ki); nwkv_s.append(wkv_s); nsh_s.append(sh_s)
    y_prompt = x_p[:, N_META:]
    return (y_prompt, x_s,
            jnp.stack(nk_p), jnp.stack(nv_p), jnp.stack(nki_p), jnp.stack(nwkv_p), jnp.stack(nsh_p),
            jnp.stack(nk_s), jnp.stack(nv_s), jnp.stack(nki_s), jnp.stack(nwkv_s), jnp.stack(nsh_s))
```

```python
import functools
import math

import numpy as np
import jax
import jax.numpy as jnp
from jax import lax
from jax.experimental import pallas as pl
from jax.experimental.pallas import tpu as pltpu

F32, BF16, I32 = jnp.float32, jnp.bfloat16, jnp.int32

D_MODEL = 2048
N_META = 16
HEAD_DIM = 64
RWKV_WIDTH = 1024
ATTN_WIDTH = 1024
RWKV_HEADS = 16
ATTN_HEADS = 16
KV_HEADS = 4
GROUP = 4
KV_WIDTH = 256
DECAY_LORA = 64
ICLR_LORA = 64
GATE_LORA = 160
RWKV_COLS = 3 * RWKV_WIDTH + DECAY_LORA + ICLR_LORA + GATE_LORA
RWKV_PAD = 3456
LORA_WA = DECAY_LORA + ICLR_LORA
GATE_PAD = RWKV_PAD - 3 * RWKV_WIDTH - LORA_WA
GN_EPS = 64e-5
IDX_HEADS = 8
IDX_DIM = 128
IDX_TOPK = 256
REL_BUCKETS = 32
REL_MAX_DIST = 128
PEER_HEADS = 8
PEER_NKEYS = 128
PEER_HALF = 128
PEER_TOPK = 16
PEER_EXPERTS = PEER_NKEYS * PEER_NKEYS
DN_ALPHA = 2.0 ** 0.25
LN_EPS = 1e-5
PAGE = 128
LANE = 128
ROW_ALIGN = 1280
VMEM_LIMIT = 56 * 1024 * 1024
INT_MIN = -2 ** 31
NEG_BIG = -1e30
NEG_INF_KEY = int(np.int32(np.uint32(0xFF800000) ^ np.uint32(0x7FFFFFFF)))


def _params(*sem):
    return pltpu.CompilerParams(dimension_semantics=sem, vmem_limit_bytes=VMEM_LIMIT)


def _dot(a, b):
    return jnp.dot(a, b, preferred_element_type=F32)


def _dot_nt(a, b):
    return lax.dot_general(a, b, (((1,), (1,)), ((), ())), preferred_element_type=F32)


def _split2(x):
    hi = x.astype(BF16)
    lo = (x - hi.astype(F32)).astype(BF16)
    return hi, lo


def _split3(x):
    hi = x.astype(BF16)
    r1 = x - hi.astype(F32)
    mid = r1.astype(BF16)
    lo = (r1 - mid.astype(F32)).astype(BF16)
    return hi, mid, lo


def _dot_hp(a, b):
    ah, al = _split2(a)
    bh, bl = _split2(b)
    return _dot(ah, bh) + (_dot(ah, bl) + _dot(al, bh))


def _dot_nt_hp(a, b):
    ah, al = _split2(a)
    bh, bl = _split2(b)
    return _dot_nt(ah, bh) + (_dot_nt(ah, bl) + _dot_nt(al, bh))


def _segsum(x, ones_bd):
    hi, mid, lo = _split3(x)
    return _dot(hi, ones_bd) + (_dot(mid, ones_bd) + _dot(lo, ones_bd))


def _float_key(x):
    bits = pltpu.bitcast(x, I32)
    return bits ^ (lax.shift_right_arithmetic(bits, 31) & 0x7FFFFFFF)


def _kth_largest_key(count_ge, n_sel, shape):
    def body(it, ans_u):
        bit = lax.shift_left(jnp.int32(1), 31 - it)
        cand_u = ans_u | bit
        cnt = count_ge(cand_u ^ INT_MIN)
        return jnp.where(cnt >= n_sel, cand_u, ans_u)

    ans_u = lax.fori_loop(0, 32, body, jnp.zeros(shape, I32))
    return ans_u ^ INT_MIN


def _tie_cutoff(count_eq_below, need, nbits, shape):
    def body(it, cut):
        cand = cut | lax.shift_left(jnp.int32(1), nbits - 1 - it)
        return jnp.where(count_eq_below(cand) <= need, cand, cut)

    return lax.fori_loop(0, nbits, body, jnp.zeros(shape, I32))


def _ln(x, g, b):
    mu = jnp.mean(x, axis=-1, keepdims=True)
    xc = x - mu
    var = jnp.mean(xc * xc, axis=-1, keepdims=True)
    return xc * lax.rsqrt(var + LN_EPS) * g + b


def _ln_in_kernel(x_ref, g_ref, b_ref, xn_ref, xb_ref):
    y = _ln(x_ref[...], g_ref[...], b_ref[...])
    xn_ref[...] = y
    xb_ref[...] = y.astype(BF16)


def _ln_in(x, g, b, tm):
    rows = x.shape[0]
    row = pl.BlockSpec((tm, D_MODEL), lambda i: (i, 0))
    vec = pl.BlockSpec((1, D_MODEL), lambda i: (0, 0))
    return pl.pallas_call(
        _ln_in_kernel,
        grid=(rows // tm,),
        in_specs=[row, vec, vec],
        out_specs=[row, row],
        out_shape=[jax.ShapeDtypeStruct((rows, D_MODEL), F32), jax.ShapeDtypeStruct((rows, D_MODEL), BF16)],
        compiler_params=_params("parallel"),
    )(x, g.reshape(1, -1), b.reshape(1, -1))


def _ln_out_kernel(x_ref, p_ref, g_ref, b_ref, o_ref):
    o_ref[...] = _ln(DN_ALPHA * x_ref[...] + p_ref[...], g_ref[...], b_ref[...])


def _ln_out(x, p, g, b, tm):
    rows = x.shape[0]
    row = pl.BlockSpec((tm, D_MODEL), lambda i: (i, 0))
    vec = pl.BlockSpec((1, D_MODEL), lambda i: (0, 0))
    return pl.pallas_call(
        _ln_out_kernel,
        grid=(rows // tm,),
        in_specs=[row, row, vec, vec],
        out_specs=row,
        out_shape=jax.ShapeDtypeStruct((rows, D_MODEL), F32),
        compiler_params=_params("parallel"),
    )(x, p, g.reshape(1, -1), b.reshape(1, -1))


def _mm_kernel(x_ref, w_ref, o_ref):
    o_ref[...] = _dot(x_ref[...], w_ref[...])


def _matmul(xb, w, tm, tn):
    m, k = xb.shape
    n = w.shape[1]
    return pl.pallas_call(
        _mm_kernel,
        grid=(m // tm, n // tn),
        in_specs=[pl.BlockSpec((tm, k), lambda i, j: (i, 0)), pl.BlockSpec((k, tn), lambda i, j: (0, j))],
        out_specs=pl.BlockSpec((tm, tn), lambda i, j: (i, j)),
        out_shape=jax.ShapeDtypeStruct((m, n), F32),
        compiler_params=_params("parallel", "arbitrary"),
    )(xb, w)


def _rwkv_pre_kernel(cur_ref, prev8_ref, init_ref, mu_ref, w0_ref, a0_ref, kk_ref, ka_ref, rk_ref,
                     wup_ref, aup_ref, gup_ref, ones_ref,
                     r_o, w_o, k_o, v_o, kn_o, b_o, g_o, bonus_o, *, n_prompt_tiles, tiles_per_batch, dec_seq):
    i = pl.program_id(0)
    cur = cur_ref[...]
    row = lax.broadcasted_iota(I32, cur.shape, 0)
    prev = jnp.where(row == 0, jnp.broadcast_to(prev8_ref[7:8, :], cur.shape), pltpu.roll(cur, 1, axis=0))
    batch_start = ((i % tiles_per_batch) == 0).astype(I32)
    first_prompt = jnp.where(row == 0, batch_start, 0)
    first_sample = jnp.where(row % dec_seq == 0, 1, 0)
    first = jnp.where(i < n_prompt_tiles, first_prompt, first_sample)
    prev = jnp.where(first > 0, init_ref[...], prev)

    xm = cur + (prev - cur) * mu_ref[...]
    r = xm[:, 0:RWKV_WIDTH]
    k = xm[:, RWKV_WIDTH:2 * RWKV_WIDTH]
    v = xm[:, 2 * RWKV_WIDTH:3 * RWKV_WIDTH]
    wa = xm[:, 3 * RWKV_WIDTH:3 * RWKV_WIDTH + LORA_WA]
    gl = xm[:, 3 * RWKV_WIDTH + LORA_WA:]
    ones_bd = ones_ref[...]

    nz = -(w0_ref[...] + _dot_hp(jnp.tanh(wa), wup_ref[...]))
    softplus = jnp.maximum(nz, 0.0) + jnp.log1p(jnp.exp(-jnp.abs(nz)))
    decay = jnp.exp(-jnp.exp(-softplus - 0.5))
    a = jax.nn.sigmoid(a0_ref[...] + _dot_hp(wa, aup_ref[...]))
    g = _dot_hp(jax.nn.sigmoid(gl), gup_ref[...])
    kn = k * kk_ref[...]
    kn = kn / jnp.maximum(jnp.sqrt(_segsum(kn * kn, ones_bd)), 1e-12)
    k_h = k * (1.0 + (a - 1.0) * ka_ref[...])
    r_o[...] = r
    w_o[...] = decay
    k_o[...] = k_h
    v_o[...] = v
    kn_o[...] = kn
    b_o[...] = kn * a
    g_o[...] = g
    bonus_o[...] = _segsum(r * k_h * rk_ref[...], ones_bd) * v


def _rwkv_pre(feat, init, mu, w0, a0, k_k, k_a, r_k, wup, aup, gup, ones_bd, n_prompt_tiles, tiles_per_batch, dec_seq):
    n_tiles = n_prompt_tiles + 1
    tm = LANE
    vec = lambda n: pl.BlockSpec((1, n), lambda i: (0, 0))
    full = lambda a: pl.BlockSpec(a.shape, lambda i: (0, 0))
    out_row = pl.BlockSpec((tm, RWKV_WIDTH), lambda i: (i, 0))
    kern = functools.partial(_rwkv_pre_kernel, n_prompt_tiles=n_prompt_tiles, tiles_per_batch=tiles_per_batch, dec_seq=dec_seq)
    return pl.pallas_call(
        kern,
        grid=(n_tiles,),
        in_specs=[
            pl.BlockSpec((tm, RWKV_PAD), lambda i: (i, 0)),
            pl.BlockSpec((8, RWKV_PAD), lambda i: (jnp.maximum(i * (tm // 8) - 1, 0), 0)),
            pl.BlockSpec((tm, RWKV_PAD), lambda i: (jnp.where(i < n_prompt_tiles, 0, 1), 0)),
            vec(RWKV_PAD), vec(RWKV_WIDTH), vec(RWKV_WIDTH), vec(RWKV_WIDTH), vec(RWKV_WIDTH), vec(RWKV_WIDTH),
            full(wup), full(aup), full(gup), full(ones_bd),
        ],
        out_specs=[out_row] * 8,
        out_shape=[jax.ShapeDtypeStruct((feat.shape[0], RWKV_WIDTH), F32)] * 8,
        compiler_params=_params("parallel"),
    )(feat, feat, init, mu, w0, a0, k_k, k_a, r_k, wup, aup, gup, ones_bd)


def _rwkv_scan_kernel(r_ref, w_ref, k_ref, v_ref, kn_ref, b_ref, s0_ref, y_ref, s_ref):
    @pl.when(pl.program_id(1) == 0)
    def _():
        s_ref[...] = s0_ref[...]

    def step(t, carry):
        kn_t = kn_ref[t]
        w_t = w_ref[t]
        b_t = b_ref[t]
        k_t = k_ref[t]
        r_t = r_ref[t]

        def value_row(vi, c):
            s_v = s_ref[vi]
            s_kn = jnp.sum(s_v * kn_t, axis=0, keepdims=True)
            s_new = s_v * w_t - s_kn * b_t + v_ref[t, pl.ds(vi, 1), :] * k_t
            s_ref[vi] = s_new
            y_ref[t, pl.ds(vi, 1), :] = jnp.sum(s_new * r_t, axis=0, keepdims=True)
            return c

        return lax.fori_loop(0, HEAD_DIM, value_row, carry, unroll=2)

    lax.fori_loop(0, r_ref.shape[0], step, 0)


def _rwkv_scan(r, w, k, v, kn, b, s0, tc):
    steps, _, pairs = r.shape
    seq = pl.BlockSpec((tc, HEAD_DIM, LANE), lambda p, c: (c, 0, p))
    state = pl.BlockSpec((HEAD_DIM, HEAD_DIM, LANE), lambda p, c: (0, 0, p))
    return pl.pallas_call(
        _rwkv_scan_kernel,
        grid=(pairs // LANE, steps // tc),
        in_specs=[seq] * 6 + [state],
        out_specs=[seq, state],
        out_shape=[jax.ShapeDtypeStruct(r.shape, F32), jax.ShapeDtypeStruct(s0.shape, F32)],
        compiler_params=_params("parallel", "arbitrary"),
    )(r, w, k, v, kn, b, s0)


def _prompt_attn_kernel(qq_ref, kvi_ref, wi_ref, bias_ref, y_ref, vt_ref, key_ref, sel_ref, *, n_blocks, n_sel):
    i = pl.program_id(1)
    t_pad = n_blocks * LANE

    @pl.when(i == 0)
    def _():
        for j in range(n_blocks):
            vt_ref[j] = kvi_ref[j * LANE:(j + 1) * LANE, KV_WIDTH:2 * KV_WIDTH].T

    kpos0 = lax.broadcasted_iota(I32, (LANE, LANE), 0)
    qpos = i * LANE + lax.broadcasted_iota(I32, (LANE, LANE), 1)

    qi_all = jnp.concatenate(
        [qq_ref[:, ATTN_WIDTH + h * IDX_DIM:ATTN_WIDTH + (h + 1) * IDX_DIM] for h in range(IDX_HEADS)], axis=0).astype(BF16)
    w_t = wi_ref[...].T
    w_flat = jnp.concatenate([w_t[h:h + 1, :] for h in range(IDX_HEADS)], axis=1)
    key_ref[...] = jnp.full((t_pad, LANE), NEG_INF_KEY, I32)

    def score_block(j, c):
        r0 = pl.multiple_of(j * LANE, LANE)
        ki = kvi_ref[pl.ds(r0, LANE), 2 * KV_WIDTH:2 * KV_WIDTH + IDX_DIM].astype(BF16)
        s = jnp.maximum(_dot_nt(ki, qi_all), 0.0) * w_flat
        acc = s[:, 0:LANE]
        for h in range(1, IDX_HEADS):
            acc = acc + s[:, h * LANE:(h + 1) * LANE]
        acc = jnp.where(acc == 0.0, 0.0, acc)
        acc = jnp.where(kpos0 + r0 <= qpos, acc, -jnp.inf)
        key_ref[pl.ds(r0, LANE), :] = _float_key(acc)
        return c

    lax.fori_loop(0, i + 1, score_block, 0)

    row1 = (1, LANE)
    thr = _kth_largest_key(lambda t: jnp.sum(jnp.where(key_ref[...] >= t, 1.0, 0.0), axis=0, keepdims=True), n_sel, row1)
    keys = key_ref[...]
    need = n_sel - jnp.sum(jnp.where(keys > thr, 1.0, 0.0), axis=0, keepdims=True)
    kidx = lax.broadcasted_iota(I32, (t_pad, LANE), 0)
    cut = _tie_cutoff(
        lambda c: jnp.sum(jnp.where(key_ref[...] == thr, jnp.where(kidx < c, 1.0, 0.0), 0.0), axis=0, keepdims=True),
        need, t_pad.bit_length(), row1)
    chosen = jnp.where(keys > thr, 1.0, jnp.where(keys == thr, jnp.where(kidx < cut, 1.0, 0.0), 0.0))
    sel_ref[...] = jnp.where(kidx <= i * LANE + lax.broadcasted_iota(I32, (t_pad, LANE), 1), chosen, 0.0)

    outs = []
    for n in range(KV_HEADS):
        q_n = jnp.concatenate(
            [qq_ref[:, (GROUP * n + g) * HEAD_DIM:(GROUP * n + g + 1) * HEAD_DIM] for g in range(GROUP)], axis=0)
        q_n = (q_n * HEAD_DIM ** -0.5).astype(BF16)

        def key_block(j, carry, n=n, q_n=q_n):
            m, l, acc = carry
            r0 = pl.multiple_of(j * LANE, LANE)
            kb = kvi_ref[pl.ds(r0, LANE), n * HEAD_DIM:(n + 1) * HEAD_DIM].astype(BF16)
            s = _dot_nt(kb, q_n) + bias_ref[n, jnp.minimum(i - j, 2)]
            sel_b = sel_ref[pl.ds(r0, LANE), :]
            sel4 = jnp.concatenate([sel_b] * GROUP, axis=1) > 0.5
            s = jnp.where(sel4, s, NEG_BIG)
            m_new = jnp.maximum(m, jnp.max(s, axis=0, keepdims=True))
            alpha = jnp.exp(m - m_new)
            p = jnp.where(sel4, jnp.exp(s - m_new), 0.0)
            l = alpha * l + jnp.sum(p, axis=0, keepdims=True)
            vt = vt_ref[j, n * HEAD_DIM:(n + 1) * HEAD_DIM, :].astype(BF16)
            acc = alpha * acc + _dot(vt, p.astype(BF16))
            return m_new, l, acc

        gq = GROUP * LANE
        m, l, acc = lax.fori_loop(
            0, i + 1, key_block,
            (jnp.full((1, gq), NEG_BIG, F32), jnp.zeros((1, gq), F32), jnp.zeros((HEAD_DIM, gq), F32)))
        o = acc / l
        outs += [o[:, g * LANE:(g + 1) * LANE].T for g in range(GROUP)]
    y_ref[...] = jnp.concatenate(outs, axis=1)


def _prompt_attn(qq, kvi, bias_tiles, n_batch, n_blocks, n_sel):
    t_pad = n_blocks * LANE
    kern = functools.partial(_prompt_attn_kernel, n_blocks=n_blocks, n_sel=n_sel)
    return pl.pallas_call(
        kern,
        grid=(n_batch, n_blocks),
        in_specs=[
            pl.BlockSpec((LANE, 2 * ATTN_WIDTH), lambda b, i: (b * n_blocks + i, 0)),
            pl.BlockSpec((t_pad, kvi.shape[1]), lambda b, i: (b, 0)),
            pl.BlockSpec((LANE, LANE), lambda b, i: (b * n_blocks + i, (2 * KV_WIDTH + IDX_DIM) // LANE)),
            pl.BlockSpec(bias_tiles.shape, lambda b, i: (0, 0, 0, 0)),
        ],
        out_specs=pl.BlockSpec((LANE, ATTN_WIDTH), lambda b, i: (b * n_blocks + i, 0)),
        out_shape=jax.ShapeDtypeStruct((n_batch * t_pad, ATTN_WIDTH), F32),
        scratch_shapes=[
            pltpu.VMEM((n_blocks, KV_WIDTH, LANE), F32),
            pltpu.VMEM((t_pad, LANE), I32),
            pltpu.VMEM((t_pad, LANE), F32),
        ],
        compiler_params=_params("parallel", "arbitrary"),
    )(qq, kvi, kvi, bias_tiles)


PAGES_PER_STEP = 8
Q_PAD = 8


def _sample_score_kernel(pt_ref, qi_ref, wb_ref, *refs):
    page_refs, out_ref = refs[:PAGES_PER_STEP], refs[PAGES_PER_STEP]
    qi = qi_ref[...].astype(BF16)
    wb = wb_ref[...]
    for u in range(PAGES_PER_STEP):
        s = jnp.maximum(_dot_nt(qi, page_refs[u][...].astype(BF16)), 0.0) * wb
        acc = s[0:Q_PAD]
        for h in range(1, IDX_HEADS):
            acc = acc + s[h * Q_PAD:(h + 1) * Q_PAD]
        out_ref[u] = acc


def _sample_scores(page_table, qi8, wb, cache_kidx):
    n_batch, n_pages = page_table.shape
    page_spec = lambda u: pl.BlockSpec((None, PAGE, IDX_DIM), lambda b, s, pt: (pt[b, s * PAGES_PER_STEP + u], 0, 0))
    per_batch = pl.BlockSpec((None, IDX_HEADS * Q_PAD, IDX_DIM), lambda b, s, pt: (b, 0, 0))
    return pl.pallas_call(
        _sample_score_kernel,
        grid_spec=pltpu.PrefetchScalarGridSpec(
            num_scalar_prefetch=1,
            grid=(n_batch, n_pages // PAGES_PER_STEP),
            in_specs=[per_batch, per_batch] + [page_spec(u) for u in range(PAGES_PER_STEP)],
            out_specs=pl.BlockSpec((None, PAGES_PER_STEP, Q_PAD, PAGE), lambda b, s, pt: (b, s, 0, 0)),
        ),
        out_shape=jax.ShapeDtypeStruct((n_batch, n_pages, Q_PAD, PAGE), F32),
        compiler_params=_params("parallel", "arbitrary"),
    )(page_table, qi8, wb, *([cache_kidx] * PAGES_PER_STEP))


def _sample_select_kernel(sc_ref, qi_ref, wb_ref, kin_ref, sel_ref, key_ref, *, n_pages, dec_seq, n_sel):
    qrow = lax.broadcasted_iota(I32, (Q_PAD, PAGE), 0)
    lane = lax.broadcasted_iota(I32, (Q_PAD, PAGE), 1)
    s = jnp.maximum(_dot_nt(qi_ref[...].astype(BF16), kin_ref[...].astype(BF16)), 0.0) * wb_ref[...]
    acc = s[0:Q_PAD]
    for h in range(1, IDX_HEADS):
        acc = acc + s[h * Q_PAD:(h + 1) * Q_PAD]
    new_valid = jnp.where(lane < dec_seq, jnp.where(lane <= qrow, 1, 0), 0) > 0
    past = sc_ref[...]
    key_ref[0:n_pages] = _float_key(jnp.where(past == 0.0, 0.0, past))
    key_ref[n_pages] = _float_key(jnp.where(new_valid, jnp.where(acc == 0.0, 0.0, acc), -jnp.inf))

    def lane_count(x):
        return jnp.sum(jnp.sum(x, axis=0), axis=1, keepdims=True)

    col1 = (Q_PAD, 1)
    thr = _kth_largest_key(lambda t: lane_count(jnp.where(key_ref[...] >= t, 1.0, 0.0)), n_sel, col1)
    keys = key_ref[...]
    need = n_sel - lane_count(jnp.where(keys > thr, 1.0, 0.0))
    shape3 = (n_pages + 1, Q_PAD, PAGE)
    kidx = lax.broadcasted_iota(I32, shape3, 0) * PAGE + lax.broadcasted_iota(I32, shape3, 2)
    cut = _tie_cutoff(
        lambda c: lane_count(jnp.where(key_ref[...] == thr, jnp.where(kidx < c, 1.0, 0.0), 0.0)),
        need, ((n_pages + 1) * PAGE).bit_length(), col1)
    chosen = jnp.where(keys > thr, 1.0, jnp.where(keys == thr, jnp.where(kidx < cut, 1.0, 0.0), 0.0))
    sel_ref[0:n_pages] = chosen[0:n_pages]
    sel_ref[n_pages] = jnp.where(new_valid, chosen[n_pages], 0.0)


def _sample_select(sc, qi8, wb, ki_new, dec_seq, n_sel):
    n_batch, n_pages = sc.shape[:2]
    kern = functools.partial(_sample_select_kernel, n_pages=n_pages, dec_seq=dec_seq, n_sel=n_sel)
    per_batch = lambda a: pl.BlockSpec((None,) + a.shape[1:], lambda b: (b,) + (0,) * (a.ndim - 1))
    return pl.pallas_call(
        kern,
        grid=(n_batch,),
        in_specs=[per_batch(sc), per_batch(qi8), per_batch(wb), per_batch(ki_new)],
        out_specs=pl.BlockSpec((None, n_pages + 1, Q_PAD, PAGE), lambda b: (b, 0, 0, 0)),
        out_shape=jax.ShapeDtypeStruct((n_batch, n_pages + 1, Q_PAD, PAGE), F32),
        scratch_shapes=[pltpu.VMEM((n_pages + 1, Q_PAD, PAGE), I32)],
        compiler_params=_params("parallel"),
    )(sc, qi8, wb, ki_new)


def _sample_attn_kernel(pt_ref, q_ref, sel_ref, selnew_ref, knew_ref, vnew_ref, bias_ref, *refs, n_steps):
    k_refs = refs[:PAGES_PER_STEP]
    v_refs = refs[PAGES_PER_STEP:2 * PAGES_PER_STEP]
    o_ref, m_ref, l_ref, acc_ref = refs[2 * PAGES_PER_STEP:]
    s_id = pl.program_id(1)
    rows = KV_HEADS * GROUP * Q_PAD
    per_kv = GROUP * Q_PAD
    q = (q_ref[...] * HEAD_DIM ** -0.5).astype(BF16)

    @pl.when(s_id == 0)
    def _():
        m_ref[...] = jnp.full(m_ref.shape, NEG_BIG, F32)
        l_ref[...] = jnp.zeros(l_ref.shape, F32)
        acc_ref[...] = jnp.zeros(acc_ref.shape, F32)

    def attend(k_pages, v_pages, sel_pages, bias_pages):
        s_blocks, sel_blocks = [], []
        for kp, sp, bp in zip(k_pages, sel_pages, bias_pages):
            kb = kp.astype(BF16)
            s = jnp.concatenate(
                [_dot_nt(q[n * per_kv:(n + 1) * per_kv], kb[:, n * HEAD_DIM:(n + 1) * HEAD_DIM]) for n in range(KV_HEADS)],
                axis=0) + bp
            s_blocks.append(s)
            sel_blocks.append(jnp.concatenate([sp] * (KV_HEADS * GROUP), axis=0) > 0.5)
        s = jnp.concatenate(s_blocks, axis=1)
        sel = jnp.concatenate(sel_blocks, axis=1)
        s = jnp.where(sel, s, NEG_BIG)
        m_old = m_ref[...]
        m_new = jnp.maximum(m_old, jnp.max(s, axis=1, keepdims=True))
        alpha = jnp.exp(m_old - m_new)
        p = jnp.where(sel, jnp.exp(s - m_new), 0.0)
        l_ref[...] = alpha * l_ref[...] + jnp.sum(p, axis=1, keepdims=True)
        m_ref[...] = m_new
        pb = p.astype(BF16)
        pv = None
        for u, vp in enumerate(v_pages):
            vb = vp.astype(BF16)
            pu = pb[:, u * PAGE:(u + 1) * PAGE]
            part = jnp.concatenate(
                [_dot(pu[n * per_kv:(n + 1) * per_kv], vb[:, n * HEAD_DIM:(n + 1) * HEAD_DIM]) for n in range(KV_HEADS)], axis=0)
            pv = part if pv is None else pv + part
        acc_ref[...] = alpha * acc_ref[...] + pv

    @pl.when(s_id < n_steps)
    def _():
        far, near = bias_ref[0], bias_ref[1]
        biases = [far] * PAGES_PER_STEP
        last = s_id == n_steps - 1
        biases[-1] = jnp.where(last, near, far)
        attend([r[...] for r in k_refs], [r[...] for r in v_refs], [sel_ref[u] for u in range(PAGES_PER_STEP)], biases)

    @pl.when(s_id == n_steps)
    def _():
        attend([knew_ref[...]], [vnew_ref[...]], [selnew_ref[...]], [bias_ref[2]])
        o_ref[...] = acc_ref[...] / l_ref[...]


def _sample_attn(page_table, q8, sel, k_new, v_new, bias_tiles, cache_k, cache_v):
    n_batch, n_pages = page_table.shape
    n_steps = n_pages // PAGES_PER_STEP
    rows = KV_HEADS * GROUP * Q_PAD
    kern = functools.partial(_sample_attn_kernel, n_steps=n_steps)

    def page_spec(u):
        return pl.BlockSpec((None, PAGE, KV_WIDTH),
                            lambda b, s, pt: (pt[b, jnp.minimum(s, n_steps - 1) * PAGES_PER_STEP + u], 0, 0))

    per_batch = lambda a: pl.BlockSpec((None,) + a.shape[1:], lambda b, s, pt: (b,) + (0,) * (a.ndim - 1))
    return pl.pallas_call(
        kern,
        grid_spec=pltpu.PrefetchScalarGridSpec(
            num_scalar_prefetch=1,
            grid=(n_batch, n_steps + 1),
            in_specs=[
                per_batch(q8),
                pl.BlockSpec((None, PAGES_PER_STEP, Q_PAD, PAGE), lambda b, s, pt: (b, jnp.minimum(s, n_steps - 1), 0, 0)),
                pl.BlockSpec((None, None, Q_PAD, PAGE), lambda b, s, pt: (b, n_pages, 0, 0)),
                per_batch(k_new), per_batch(v_new),
                pl.BlockSpec(bias_tiles.shape, lambda b, s, pt: (0, 0, 0)),
            ] + [page_spec(u) for u in range(PAGES_PER_STEP)] * 2,
            out_specs=pl.BlockSpec((None, rows, HEAD_DIM), lambda b, s, pt: (b, 0, 0)),
            scratch_shapes=[pltpu.VMEM((rows, 1), F32), pltpu.VMEM((rows, 1), F32), pltpu.VMEM((rows, HEAD_DIM), F32)],
        ),
        out_shape=jax.ShapeDtypeStruct((n_batch, rows, HEAD_DIM), F32),
        compiler_params=_params("parallel", "arbitrary"),
    )(page_table, q8, sel, sel, k_new, v_new, bias_tiles, *([cache_k] * PAGES_PER_STEP), *([cache_v] * PAGES_PER_STEP))


def _mix_kernel(ys_ref, bonus_ref, g_ref, ya_ref, xn_ref, gng_ref, gnb_ref, ones_ref, wor_ref, woa_ref, lg_ref, lb_ref,
                x1_ref, x1b_ref):
    ones_bd = ones_ref[...]
    ys = ys_ref[...]
    inv = 1.0 / HEAD_DIM
    yc = ys - _segsum(ys, ones_bd) * inv
    var = _segsum(yc * yc, ones_bd) * inv
    yr = (yc * lax.rsqrt(var + GN_EPS) * gng_ref[...] + gnb_ref[...] + bonus_ref[...]) * g_ref[...]
    mix = _dot(yr.astype(BF16), wor_ref[...]) + _dot(ya_ref[...].astype(BF16), woa_ref[...])
    x1 = _ln(DN_ALPHA * xn_ref[...] + mix, lg_ref[...], lb_ref[...])
    x1_ref[...] = x1
    x1b_ref[...] = x1.astype(BF16)


def _mix(ys, bonus, g, ya, xn, gn_g, gn_b, ones_bd, wo_r, wo_a, ln_g, ln_b, tm):
    rows = xn.shape[0]
    half = pl.BlockSpec((tm, RWKV_WIDTH), lambda i: (i, 0))
    row = pl.BlockSpec((tm, D_MODEL), lambda i: (i, 0))
    vec = lambda n: pl.BlockSpec((1, n), lambda i: (0, 0))
    full = lambda a: pl.BlockSpec(a.shape, lambda i: (0, 0))
    return pl.pallas_call(
        _mix_kernel,
        grid=(rows // tm,),
        in_specs=[half, half, half, half, row, vec(RWKV_WIDTH), vec(RWKV_WIDTH), full(ones_bd), full(wo_r), full(wo_a),
                  vec(D_MODEL), vec(D_MODEL)],
        out_specs=[row, row],
        out_shape=[jax.ShapeDtypeStruct((rows, D_MODEL), F32), jax.ShapeDtypeStruct((rows, D_MODEL), BF16)],
        compiler_params=_params("parallel"),
    )(ys, bonus, g, ya, xn, gn_g, gn_b, ones_bd, wo_r, wo_a, ln_g, ln_b)


N_CAND = sum(PEER_TOPK // c for c in range(1, PEER_TOPK + 1))
CAND_PAIRS = [(c, d) for c in range(PEER_TOPK) for d in range(PEER_TOPK) if (c + 1) * (d + 1) <= PEER_TOPK]


def _top_rows(x, n):
    rows = []
    for _ in range(n):
        m = jnp.max(x, axis=0, keepdims=True)
        rows.append(m)
        x = jnp.where(x == m, -jnp.inf, x)
    return rows


def _peer_route_kernel(x_ref, wq_ref, sub_ref, w_ref, s1_ref, s2_ref, e2_ref, thr_ref, m1_ref, zinv_ref):
    q = _dot(x_ref[...], wq_ref[...])
    for h in range(PEER_HEADS):
        base = h * 2 * PEER_HALF
        s1 = _dot_nt_hp(sub_ref[h, 0], q[:, base:base + PEER_HALF])
        s2 = _dot_nt_hp(sub_ref[h, 1], q[:, base + PEER_HALF:base + 2 * PEER_HALF])
        top1 = _top_rows(s1, PEER_TOPK)
        top2 = _top_rows(s2, PEER_TOPK)
        cand = jnp.concatenate([top1[c] + top2[d] for c, d in CAND_PAIRS]
                               + [jnp.full_like(top1[0], -jnp.inf)] * (-len(CAND_PAIRS) % 8), axis=0)
        best = _top_rows(cand, PEER_TOPK)
        thr = best[-1]
        m = top1[0] + top2[0]
        z = jnp.sum(jnp.where(cand >= thr, jnp.exp(cand - m), 0.0), axis=0, keepdims=True)
        s1_ref[h] = s1
        s2_ref[h] = s2
        e2_ref[h] = jnp.exp(s2 - top2[0])
        thr_ref[h:h + 1, :] = thr
        m1_ref[h:h + 1, :] = top1[0]
        zinv_ref[h:h + 1, :] = 1.0 / z

    def expert_row(i, c):
        acc = jnp.zeros((PEER_NKEYS, x_ref.shape[0]), F32)
        for h in range(PEER_HEADS):
            s1_i = s1_ref[h, pl.ds(i, 1), :]
            f = jnp.exp(s1_i - m1_ref[h:h + 1, :]) * zinv_ref[h:h + 1, :]
            acc = acc + jnp.where(s1_i + s2_ref[h] >= thr_ref[h:h + 1, :], e2_ref[h] * f, 0.0)
        w_ref[i] = acc.T
        return c

    lax.fori_loop(0, PEER_NKEYS, expert_row, 0)


def _peer_route(x1b, wq, subkeys):
    rows = x1b.shape[0]
    tm = LANE
    stat = pltpu.VMEM((PEER_HEADS, tm), F32)
    big = pltpu.VMEM((PEER_HEADS, PEER_NKEYS, tm), F32)
    return pl.pallas_call(
        _peer_route_kernel,
        grid=(rows // tm,),
        in_specs=[pl.BlockSpec((tm, D_MODEL), lambda i: (i, 0)),
                  pl.BlockSpec(wq.shape, lambda i: (0, 0)),
                  pl.BlockSpec(subkeys.shape, lambda i: (0, 0, 0, 0))],
        out_specs=pl.BlockSpec((PEER_NKEYS, tm, PEER_NKEYS), lambda i: (0, i, 0)),
        out_shape=jax.ShapeDtypeStruct((PEER_NKEYS, rows, PEER_NKEYS), F32),
        scratch_shapes=[big, big, big, stat, stat, stat],
        compiler_params=_params("parallel"),
    )(x1b, wq, subkeys)


EXPERT_ROWS = 4


def _peer_dense_kernel(x_ref, u_ref, v_ref, w_ref, o_ref):
    @pl.when(pl.program_id(1) == 0)
    def _():
        o_ref[...] = jnp.zeros(o_ref.shape, F32)

    h = _dot_nt(x_ref[...], u_ref[...])
    w = jnp.concatenate([w_ref[e] for e in range(EXPERT_ROWS)], axis=1)
    a = 0.5 * h * (1.0 + lax.erf(h * (2.0 ** -0.5))) * w
    o_ref[...] += _dot(a.astype(BF16), v_ref[...])


def _peer_dense(x1b, u, v, w, tm):
    rows = x1b.shape[0]
    eb = EXPERT_ROWS * PEER_NKEYS
    return pl.pallas_call(
        _peer_dense_kernel,
        grid=(rows // tm, PEER_EXPERTS // eb),
        in_specs=[pl.BlockSpec((tm, D_MODEL), lambda i, j: (i, 0)),
                  pl.BlockSpec((eb, D_MODEL), lambda i, j: (j, 0)),
                  pl.BlockSpec((eb, D_MODEL), lambda i, j: (j, 0)),
                  pl.BlockSpec((EXPERT_ROWS, tm, PEER_NKEYS), lambda i, j: (j, i, 0))],
        out_specs=pl.BlockSpec((tm, D_MODEL), lambda i, j: (i, 0)),
        out_shape=jax.ShapeDtypeStruct((rows, D_MODEL), F32),
        compiler_params=_params("parallel", "arbitrary"),
    )(x1b, u, v, w)


def _rel_buckets(dist):
    max_exact = REL_BUCKETS // 2
    d = np.maximum(dist, 0)
    ratio = np.log(np.maximum(d, 1).astype(np.float32) / np.float32(max_exact)) / np.float32(math.log(REL_MAX_DIST / max_exact))
    log_b = max_exact + (ratio * np.float32(REL_BUCKETS - max_exact)).astype(np.int32)
    return np.where(d < max_exact, d, np.minimum(log_b, REL_BUCKETS - 1)).astype(np.int32)


def _prompt_bias_tiles(rel_bias):
    kk = np.arange(LANE)[:, None]
    qq = np.arange(LANE)[None, :]
    tiles = []
    for delta in range(3):
        b = rel_bias[_rel_buckets(delta * LANE + qq - kk)]
        b = b.reshape(LANE, LANE, KV_HEADS, GROUP).transpose(2, 0, 3, 1).reshape(KV_HEADS, LANE, GROUP * LANE)
        tiles.append(b)
    return jnp.stack(tiles, axis=1).astype(F32)


def _sample_bias_tiles(rel_bias, past_len, dec_seq):
    q = np.arange(Q_PAD)[:, None]
    off = np.arange(PAGE)[None, :]
    qpos = past_len + np.minimum(q, dec_seq - 1)
    dists = [qpos - 0 * off - (past_len - 2 * PAGE), qpos - (past_len - PAGE + off), qpos - (past_len + np.minimum(off, dec_seq - 1))]
    tiles = []
    for d in dists:
        b = rel_bias[_rel_buckets(d + 0 * off)]
        tiles.append(b.transpose(2, 0, 1).reshape(ATTN_HEADS * Q_PAD, PAGE))
    return jnp.stack(tiles).astype(F32)


def kernel(x_prompt, x_sample, cache_k, cache_v, cache_kidx, state_wkv, state_shift, page_table, meta_tokens, ln_in_g, ln_in_b, rel_bias, w_in, mu_shift, w0, w_up, a0, a_up, g_up, k_k, k_a, r_k, gn_g, gn_b, w_o, ln1_g, ln1_b, peer_wq, peer_subkeys, peer_u, peer_v, ln2_g, ln2_b):
    n_batch, seq, _ = x_prompt.shape
    dec_batch, dec_seq, _ = x_sample.shape
    n_pages = page_table.shape[1]
    past_len = n_pages * PAGE
    t_len = seq + N_META
    n_blocks = -(-t_len // LANE)
    t_pad = n_blocks * LANE
    rows_p = n_batch * t_pad
    rows_s = dec_batch * dec_seq
    assert rows_s == LANE and Q_PAD >= dec_seq and n_pages % PAGES_PER_STEP == 0
    assert past_len >= 2 * PAGE + REL_MAX_DIST
    rows = -(-(rows_p + rows_s) // ROW_ALIGN) * ROW_ALIGN
    layer = 0

    meta = jnp.broadcast_to(meta_tokens[None], (n_batch, N_META, D_MODEL))
    xp = jnp.pad(jnp.concatenate([meta, x_prompt], axis=1), ((0, 0), (0, t_pad - t_len), (0, 0)))
    x_all = jnp.concatenate([xp.reshape(rows_p, D_MODEL), x_sample.reshape(rows_s, D_MODEL),
                             jnp.zeros((rows - rows_p - rows_s, D_MODEL), F32)], axis=0)
    xn, xb = _ln_in(x_all, ln_in_g, ln_in_b, 256)

    w = w_in[layer]
    c0 = RWKV_COLS
    w_rwkv = jnp.pad(w[:, :c0], ((0, 0), (0, RWKV_PAD - RWKV_COLS))).astype(BF16)
    w_qq = jnp.concatenate([w[:, c0:c0 + ATTN_WIDTH], w[:, c0 + ATTN_WIDTH + 2 * KV_WIDTH:c0 + 2 * ATTN_WIDTH + 2 * KV_WIDTH]], axis=1).astype(BF16)
    c_ki = c0 + 2 * ATTN_WIDTH + 2 * KV_WIDTH
    w_kvi = jnp.concatenate([w[:, c0 + ATTN_WIDTH:c0 + ATTN_WIDTH + 2 * KV_WIDTH], w[:, c_ki:c_ki + IDX_DIM],
                             jnp.pad(w[:, c_ki + IDX_DIM:], ((0, 0), (0, LANE - IDX_HEADS)))], axis=1).astype(BF16)
    feat = _matmul(xb, w_rwkv, 640, RWKV_PAD // 3)
    qq = _matmul(xb, w_qq, 640, 1024)
    kvi = _matmul(xb, w_kvi, 640, w_kvi.shape[1])

    def prompt_rows(a):
        return a[:rows_p].reshape(n_batch, t_pad, -1)[:, :t_len]

    def sample_rows(a):
        return a[rows_p:rows_p + rows_s].reshape(dec_batch, dec_seq, -1)

    ones_bd = jnp.asarray(np.kron(np.eye(RWKV_HEADS), np.ones((HEAD_DIM, HEAD_DIM))), BF16)
    pad_cols = lambda a: jnp.pad(a, ((0, 0), (0, RWKV_PAD - RWKV_COLS)))
    init = jnp.zeros((dec_batch, dec_seq, RWKV_PAD), F32).at[:, 0].set(pad_cols(state_shift[layer]))
    init = jnp.concatenate([jnp.zeros((LANE, RWKV_PAD), F32), init.reshape(rows_s, RWKV_PAD)], axis=0)
    wup = jnp.pad(w_up[layer], ((0, ICLR_LORA), (0, 0)))
    aup = jnp.pad(a_up[layer], ((DECAY_LORA, 0), (0, 0)))
    gup = jnp.pad(g_up[layer], ((0, GATE_PAD - GATE_LORA), (0, 0)))
    vec = lambda a: a.reshape(1, -1)
    pre = _rwkv_pre(feat, init, vec(pad_cols(mu_shift[layer][None])), vec(w0[layer]), vec(a0[layer]), vec(k_k[layer]),
                    vec(k_a[layer]), vec(r_k[layer]), wup, aup, gup, ones_bd, rows_p // LANE, n_blocks, dec_seq)
    r_all, w_all, k_all, v_all, kn_all, b_all, g_all, bonus_all = pre

    def to_scan(a, take, nb, steps):
        return take(a).reshape(nb, steps, RWKV_HEADS, HEAD_DIM).transpose(1, 3, 0, 2).reshape(steps, HEAD_DIM, nb * RWKV_HEADS)

    def from_scan(y, nb, steps):
        return y.reshape(steps, HEAD_DIM, nb, RWKV_HEADS).transpose(2, 0, 3, 1).reshape(nb, steps, RWKV_WIDTH)

    def state_in(s):
        nb = s.shape[0]
        return s.transpose(2, 3, 0, 1).reshape(HEAD_DIM, HEAD_DIM, nb * RWKV_HEADS)

    def state_out(s, nb):
        return s.reshape(HEAD_DIM, HEAD_DIM, nb, RWKV_HEADS).transpose(2, 3, 0, 1)

    seqs = (r_all, w_all, k_all, v_all, kn_all, b_all)
    tc = next(c for c in (48, 43, 32, 16, 8, 4, 2, 1) if t_len % c == 0)
    y_p, wkv_p = _rwkv_scan(*[to_scan(a, prompt_rows, n_batch, t_len) for a in seqs],
                            jnp.zeros((HEAD_DIM, HEAD_DIM, n_batch * RWKV_HEADS), F32), tc)
    y_s, wkv_s = _rwkv_scan(*[to_scan(a, sample_rows, dec_batch, dec_seq) for a in seqs],
                            state_in(state_wkv[layer]), dec_seq)
    ys_all = jnp.concatenate([
        jnp.pad(from_scan(y_p, n_batch, t_len), ((0, 0), (0, t_pad - t_len), (0, 0))).reshape(rows_p, RWKV_WIDTH),
        from_scan(y_s, dec_batch, dec_seq).reshape(rows_s, RWKV_WIDTH),
        jnp.zeros((rows - rows_p - rows_s, RWKV_WIDTH), F32)], axis=0)

    ya_p = _prompt_attn(qq, kvi, _prompt_bias_tiles(rel_bias), n_batch, n_blocks, min(IDX_TOPK, t_len // 4))

    qq_s, kvi_s = sample_rows(qq), sample_rows(kvi)

    def pad_q(a):
        a = jnp.pad(a.transpose(0, 2, 1, 3), ((0, 0), (0, 0), (0, Q_PAD - dec_seq), (0, 0)))
        return a.reshape(dec_batch, -1, a.shape[-1])

    qi8 = pad_q(qq_s[..., ATTN_WIDTH:].reshape(dec_batch, dec_seq, IDX_HEADS, IDX_DIM))
    wi_s = kvi_s[..., 2 * KV_WIDTH + IDX_DIM:2 * KV_WIDTH + IDX_DIM + IDX_HEADS]
    wb = jnp.broadcast_to(pad_q(wi_s[..., None]), (dec_batch, IDX_HEADS * Q_PAD, IDX_DIM))
    q8 = pad_q(qq_s[..., :ATTN_WIDTH].reshape(dec_batch, dec_seq, ATTN_HEADS, HEAD_DIM))
    pad_keys = lambda a: jnp.pad(a, ((0, 0), (0, PAGE - dec_seq), (0, 0)))
    ki_new = pad_keys(kvi_s[..., 2 * KV_WIDTH:2 * KV_WIDTH + IDX_DIM])
    k_new = pad_keys(kvi_s[..., :KV_WIDTH])
    v_new = pad_keys(kvi_s[..., KV_WIDTH:2 * KV_WIDTH])
    n_pool = cache_k.shape[1]
    sc = _sample_scores(page_table, qi8, wb, cache_kidx[layer])
    sel = _sample_select(sc, qi8, wb, ki_new, dec_seq, min(IDX_TOPK, (past_len + dec_seq) // 4))
    o_s = _sample_attn(page_table, q8, sel, k_new, v_new, _sample_bias_tiles(rel_bias, past_len, dec_seq),
                       cache_k[layer].reshape(n_pool, PAGE, KV_WIDTH), cache_v[layer].reshape(n_pool, PAGE, KV_WIDTH))
    ya_s = o_s.reshape(dec_batch, ATTN_HEADS, Q_PAD, HEAD_DIM)[:, :, :dec_seq].transpose(0, 2, 1, 3).reshape(rows_s, ATTN_WIDTH)
    ya_all = jnp.concatenate([ya_p, ya_s, jnp.zeros((rows - rows_p - rows_s, ATTN_WIDTH), F32)], axis=0)

    wo = w_o[layer].astype(BF16)
    x1, x1b = _mix(ys_all, bonus_all, g_all, ya_all, xn, vec(gn_g[layer]), vec(gn_b[layer]), ones_bd,
                   wo[:RWKV_WIDTH], wo[RWKV_WIDTH:], vec(ln1_g[layer]), vec(ln1_b[layer]), 256)
    gates = _peer_route(x1b, peer_wq[layer].astype(BF16), peer_subkeys[layer])
    peer = _peer_dense(x1b, peer_u[layer].astype(BF16), peer_v[layer].astype(BF16), gates, 640)
    y = _ln_out(x1, peer, ln2_g[layer], ln2_b[layer], 256)

    feat_p, feat_s = prompt_rows(feat), sample_rows(feat)
    kvi_p = prompt_rows(kvi)
    kv4 = lambda a, nb, steps: a.reshape(nb, steps, KV_HEADS, HEAD_DIM)[None]
    return (
        prompt_rows(y)[:, N_META:], sample_rows(y),
        kv4(kvi_p[..., :KV_WIDTH], n_batch, t_len), kv4(kvi_p[..., KV_WIDTH:2 * KV_WIDTH], n_batch, t_len),
        kvi_p[..., 2 * KV_WIDTH:2 * KV_WIDTH + IDX_DIM][None],
        state_out(wkv_p, n_batch)[None], feat_p[:, -1, :RWKV_COLS][None],
        kv4(kvi_s[..., :KV_WIDTH], dec_batch, dec_seq), kv4(kvi_s[..., KV_WIDTH:2 * KV_WIDTH], dec_batch, dec_seq),
        kvi_s[..., 2 * KV_WIDTH:2 * KV_WIDTH + IDX_DIM][None],
        state_out(wkv_s, dec_batch)[None], feat_s[:, -1, :RWKV_COLS][None],
    )
```

```python
import functools
import math

import numpy as np
import jax
import jax.numpy as jnp
from jax import lax
from jax.experimental import pallas as pl
from jax.experimental.pallas import tpu as pltpu

F32, BF16, I32 = jnp.float32, jnp.bfloat16, jnp.int32

D_MODEL = 2048
N_META = 16
HEAD_DIM = 64
RWKV_WIDTH = 1024
ATTN_WIDTH = 1024
RWKV_HEADS = 16
ATTN_HEADS = 16
KV_HEADS = 4
GROUP = 4
KV_WIDTH = 256
DECAY_LORA = 64
ICLR_LORA = 64
GATE_LORA = 160
RWKV_COLS = 3 * RWKV_WIDTH + DECAY_LORA + ICLR_LORA + GATE_LORA
RWKV_PAD = 3456
LORA_WA = DECAY_LORA + ICLR_LORA
GATE_PAD = RWKV_PAD - 3 * RWKV_WIDTH - LORA_WA
GN_EPS = 64e-5
IDX_HEADS = 8
IDX_DIM = 128
IDX_TOPK = 256
REL_BUCKETS = 32
REL_MAX_DIST = 128
PEER_HEADS = 8
PEER_NKEYS = 128
PEER_HALF = 128
PEER_TOPK = 16
PEER_EXPERTS = PEER_NKEYS * PEER_NKEYS
DN_ALPHA = 2.0 ** 0.25
LN_EPS = 1e-5
PAGE = 128
LANE = 128
ROW_ALIGN = 2560
VMEM_LIMIT = 56 * 1024 * 1024
INT_MIN = -2 ** 31
NEG_BIG = -1e30
NEG_INF_KEY = int(np.int32(np.uint32(0xFF800000) ^ np.uint32(0x7FFFFFFF)))


def _params(*sem):
    return pltpu.CompilerParams(dimension_semantics=sem, vmem_limit_bytes=VMEM_LIMIT)


def _dot(a, b):
    return jnp.dot(a, b, preferred_element_type=F32)


def _dot_nt(a, b):
    return lax.dot_general(a, b, (((1,), (1,)), ((), ())), preferred_element_type=F32)


def _split2(x):
    hi = x.astype(BF16)
    lo = (x - hi.astype(F32)).astype(BF16)
    return hi, lo


def _split3(x):
    hi = x.astype(BF16)
    r1 = x - hi.astype(F32)
    mid = r1.astype(BF16)
    lo = (r1 - mid.astype(F32)).astype(BF16)
    return hi, mid, lo


def _dot_hp(a, b):
    ah, al = _split2(a)
    bh, bl = _split2(b)
    return _dot(ah, bh) + (_dot(ah, bl) + _dot(al, bh))


def _dot_nt_hp(a, b):
    ah, al = _split2(a)
    bh, bl = _split2(b)
    return _dot_nt(ah, bh) + (_dot_nt(ah, bl) + _dot_nt(al, bh))


def _segsum(x, ones_bd):
    hi, mid, lo = _split3(x)
    return _dot(hi, ones_bd) + (_dot(mid, ones_bd) + _dot(lo, ones_bd))


def _float_key(x):
    bits = pltpu.bitcast(x, I32)
    return bits ^ (lax.shift_right_arithmetic(bits, 31) & 0x7FFFFFFF)


def _kth_largest_key(count_ge, n_sel, shape):
    def body(it, ans_u):
        bit = lax.shift_left(jnp.int32(1), 31 - it)
        cand_u = ans_u | bit
        cnt = count_ge(cand_u ^ INT_MIN)
        return jnp.where(cnt >= n_sel, cand_u, ans_u)

    ans_u = lax.fori_loop(0, 32, body, jnp.zeros(shape, I32))
    return ans_u ^ INT_MIN


def _tie_cutoff(count_eq_below, need, nbits, shape):
    def body(it, cut):
        cand = cut | lax.shift_left(jnp.int32(1), nbits - 1 - it)
        return jnp.where(count_eq_below(cand) <= need, cand, cut)

    return lax.fori_loop(0, nbits, body, jnp.zeros(shape, I32))


def _ln(x, g, b):
    mu = jnp.mean(x, axis=-1, keepdims=True)
    xc = x - mu
    var = jnp.mean(xc * xc, axis=-1, keepdims=True)
    return xc * lax.rsqrt(var + LN_EPS) * g + b


def _ln_in_kernel(x_ref, g_ref, b_ref, xn_ref, xb_ref):
    y = _ln(x_ref[...], g_ref[...], b_ref[...])
    xn_ref[...] = y
    xb_ref[...] = y.astype(BF16)


def _ln_in(x, g, b, tm):
    rows = x.shape[0]
    row = pl.BlockSpec((tm, D_MODEL), lambda i: (i, 0))
    vec = pl.BlockSpec((1, D_MODEL), lambda i: (0, 0))
    return pl.pallas_call(
        _ln_in_kernel,
        grid=(rows // tm,),
        in_specs=[row, vec, vec],
        out_specs=[row, row],
        out_shape=[jax.ShapeDtypeStruct((rows, D_MODEL), F32), jax.ShapeDtypeStruct((rows, D_MODEL), BF16)],
        compiler_params=_params("parallel"),
    )(x, g.reshape(1, -1), b.reshape(1, -1))


def _ln_out_kernel(x_ref, pt_ref, g_ref, b_ref, o_ref):
    o_ref[...] = _ln(DN_ALPHA * x_ref[...] + pt_ref[...].T, g_ref[...], b_ref[...])


def _ln_out(x, pt, g, b, tm):
    rows = x.shape[0]
    row = pl.BlockSpec((tm, D_MODEL), lambda i: (i, 0))
    vec = pl.BlockSpec((1, D_MODEL), lambda i: (0, 0))
    return pl.pallas_call(
        _ln_out_kernel,
        grid=(rows // tm,),
        in_specs=[row, pl.BlockSpec((D_MODEL, tm), lambda i: (0, i)), vec, vec],
        out_specs=row,
        out_shape=jax.ShapeDtypeStruct((rows, D_MODEL), F32),
        compiler_params=_params("parallel"),
    )(x, pt, g.reshape(1, -1), b.reshape(1, -1))


def _mm_kernel(x_ref, w_ref, o_ref):
    o_ref[...] = _dot(x_ref[...], w_ref[...])


def _matmul(xb, w, tm, tn):
    m, k = xb.shape
    n = w.shape[1]
    return pl.pallas_call(
        _mm_kernel,
        grid=(m // tm, n // tn),
        in_specs=[pl.BlockSpec((tm, k), lambda i, j: (i, 0)), pl.BlockSpec((k, tn), lambda i, j: (0, j))],
        out_specs=pl.BlockSpec((tm, tn), lambda i, j: (i, j)),
        out_shape=jax.ShapeDtypeStruct((m, n), F32),
        compiler_params=_params("parallel", "arbitrary"),
    )(xb, w)


def _rwkv_pre_kernel(cur_ref, prev8_ref, init_ref, mu_ref, w0_ref, a0_ref, kk_ref, ka_ref, rk_ref,
                     wup_ref, aup_ref, gup_ref, ones_ref,
                     r_o, w_o, k_o, v_o, kn_o, b_o, g_o, bonus_o, *, n_prompt_tiles, tiles_per_batch, dec_seq):
    i = pl.program_id(0)
    cur = cur_ref[...]
    row = lax.broadcasted_iota(I32, cur.shape, 0)
    prev = jnp.where(row == 0, jnp.broadcast_to(prev8_ref[7:8, :], cur.shape), pltpu.roll(cur, 1, axis=0))
    batch_start = ((i % tiles_per_batch) == 0).astype(I32)
    first_prompt = jnp.where(row == 0, batch_start, 0)
    first_sample = jnp.where(row % dec_seq == 0, 1, 0)
    first = jnp.where(i < n_prompt_tiles, first_prompt, first_sample)
    prev = jnp.where(first > 0, init_ref[...], prev)

    xm = cur + (prev - cur) * mu_ref[...]
    r = xm[:, 0:RWKV_WIDTH]
    k = xm[:, RWKV_WIDTH:2 * RWKV_WIDTH]
    v = xm[:, 2 * RWKV_WIDTH:3 * RWKV_WIDTH]
    wa = xm[:, 3 * RWKV_WIDTH:3 * RWKV_WIDTH + LORA_WA]
    gl = xm[:, 3 * RWKV_WIDTH + LORA_WA:]
    ones_bd = ones_ref[...]

    nz = -(w0_ref[...] + _dot_hp(jnp.tanh(wa), wup_ref[...]))
    softplus = jnp.maximum(nz, 0.0) + jnp.log1p(jnp.exp(-jnp.abs(nz)))
    decay = jnp.exp(-jnp.exp(-softplus - 0.5))
    a = jax.nn.sigmoid(a0_ref[...] + _dot_hp(wa, aup_ref[...]))
    g = _dot_hp(jax.nn.sigmoid(gl), gup_ref[...])
    kn = k * kk_ref[...]
    kn = kn / jnp.maximum(jnp.sqrt(_segsum(kn * kn, ones_bd)), 1e-12)
    k_h = k * (1.0 + (a - 1.0) * ka_ref[...])
    r_o[...] = r
    w_o[...] = decay
    k_o[...] = k_h
    v_o[...] = v
    kn_o[...] = kn
    b_o[...] = kn * a
    g_o[...] = g
    bonus_o[...] = _segsum(r * k_h * rk_ref[...], ones_bd) * v


def _rwkv_pre(feat, init, mu, w0, a0, k_k, k_a, r_k, wup, aup, gup, ones_bd, n_prompt_tiles, tiles_per_batch, dec_seq):
    n_tiles = n_prompt_tiles + 1
    tm = LANE
    vec = lambda n: pl.BlockSpec((1, n), lambda i: (0, 0))
    full = lambda a: pl.BlockSpec(a.shape, lambda i: (0, 0))
    out_row = pl.BlockSpec((tm, RWKV_WIDTH), lambda i: (i, 0))
    kern = functools.partial(_rwkv_pre_kernel, n_prompt_tiles=n_prompt_tiles, tiles_per_batch=tiles_per_batch, dec_seq=dec_seq)
    return pl.pallas_call(
        kern,
        grid=(n_tiles,),
        in_specs=[
            pl.BlockSpec((tm, RWKV_PAD), lambda i: (i, 0)),
            pl.BlockSpec((8, RWKV_PAD), lambda i: (jnp.maximum(i * (tm // 8) - 1, 0), 0)),
            pl.BlockSpec((tm, RWKV_PAD), lambda i: (jnp.where(i < n_prompt_tiles, 0, 1), 0)),
            vec(RWKV_PAD), vec(RWKV_WIDTH), vec(RWKV_WIDTH), vec(RWKV_WIDTH), vec(RWKV_WIDTH), vec(RWKV_WIDTH),
            full(wup), full(aup), full(gup), full(ones_bd),
        ],
        out_specs=[out_row] * 8,
        out_shape=[jax.ShapeDtypeStruct((feat.shape[0], RWKV_WIDTH), F32)] * 8,
        compiler_params=_params("parallel"),
    )(feat, feat, init, mu, w0, a0, k_k, k_a, r_k, wup, aup, gup, ones_bd)


def _rwkv_scan_kernel(r_ref, w_ref, k_ref, v_ref, kn_ref, b_ref, s0_ref, y_ref, s_ref):
    @pl.when(pl.program_id(1) == 0)
    def _():
        s_ref[...] = s0_ref[...]

    def step(t, carry):
        kn_t = kn_ref[t]
        w_t = w_ref[t]
        b_t = b_ref[t]
        k_t = k_ref[t]
        r_t = r_ref[t]

        def value_row(vi, c):
            s_v = s_ref[vi]
            s_kn = jnp.sum(s_v * kn_t, axis=0, keepdims=True)
            s_new = s_v * w_t - s_kn * b_t + v_ref[t, pl.ds(vi, 1), :] * k_t
            s_ref[vi] = s_new
            y_ref[t, pl.ds(vi, 1), :] = jnp.sum(s_new * r_t, axis=0, keepdims=True)
            return c

        return lax.fori_loop(0, HEAD_DIM, value_row, carry, unroll=2)

    lax.fori_loop(0, r_ref.shape[0], step, 0)


def _rwkv_scan(r, w, k, v, kn, b, s0, tc):
    steps, _, pairs = r.shape
    seq = pl.BlockSpec((tc, HEAD_DIM, LANE), lambda p, c: (c, 0, p))
    state = pl.BlockSpec((HEAD_DIM, HEAD_DIM, LANE), lambda p, c: (0, 0, p))
    return pl.pallas_call(
        _rwkv_scan_kernel,
        grid=(pairs // LANE, steps // tc),
        in_specs=[seq] * 6 + [state],
        out_specs=[seq, state],
        out_shape=[jax.ShapeDtypeStruct(r.shape, F32), jax.ShapeDtypeStruct(s0.shape, F32)],
        compiler_params=_params("parallel", "arbitrary"),
    )(r, w, k, v, kn, b, s0)


def _prompt_attn_kernel(qq_ref, kvi_ref, wi_ref, bias_ref, y_ref,
                        vt_ref, kb_ref, kib_ref, qn_ref, key_ref, sel_ref, cut_ref, acc_ref, *, n_blocks, n_sel):
    i = pl.program_id(1)
    n_kb = i + 1
    gq = GROUP * LANE

    @pl.when(i == 0)
    def _():
        for j in range(n_blocks):
            vt_ref[j] = kvi_ref[j * LANE:(j + 1) * LANE, KV_WIDTH:2 * KV_WIDTH].T.astype(BF16)
        for n in range(KV_HEADS):
            kb_ref[n] = kvi_ref[:, n * HEAD_DIM:(n + 1) * HEAD_DIM].astype(BF16)
        kib_ref[...] = kvi_ref[:, 2 * KV_WIDTH:2 * KV_WIDTH + IDX_DIM].astype(BF16)

    kpos0 = lax.broadcasted_iota(I32, (LANE, LANE), 0)
    qpos = i * LANE + lax.broadcasted_iota(I32, (LANE, LANE), 1)

    qi_all = jnp.concatenate(
        [qq_ref[:, ATTN_WIDTH + h * IDX_DIM:ATTN_WIDTH + (h + 1) * IDX_DIM] for h in range(IDX_HEADS)], axis=0).astype(BF16)
    w_t = wi_ref[...].T
    w_flat = jnp.concatenate([w_t[h:h + 1, :] for h in range(IDX_HEADS)], axis=1)

    def score_block(j, c):
        r0 = pl.multiple_of(j * LANE, LANE)
        ki = kib_ref[pl.ds(r0, LANE), :]
        s = jnp.maximum(_dot_nt(ki, qi_all), 0.0) * w_flat
        acc = s[:, 0:LANE]
        for h in range(1, IDX_HEADS):
            acc = acc + s[:, h * LANE:(h + 1) * LANE]
        acc = jnp.where(acc == 0.0, 0.0, acc)
        acc = jnp.where(kpos0 + r0 <= qpos, acc, -jnp.inf)
        key_ref[pl.ds(r0, LANE), :] = _float_key(acc)
        return c

    lax.fori_loop(0, n_kb, score_block, 0)

    row1 = (1, LANE)

    def count(flag):
        def body(j, acc):
            r0 = pl.multiple_of(j * LANE, LANE)
            return acc + flag(key_ref[pl.ds(r0, LANE), :], r0)

        return jnp.sum(lax.fori_loop(0, n_kb, body, jnp.zeros((LANE, LANE), F32)), axis=0, keepdims=True)

    thr = _kth_largest_key(lambda t: count(lambda k, r0: jnp.where(k >= t, 1.0, 0.0)), n_sel, row1)
    need = n_sel - count(lambda k, r0: jnp.where(k > thr, 1.0, 0.0))
    n_tied = count(lambda k, r0: jnp.where(k == thr, 1.0, 0.0))
    nbits = (n_blocks * LANE).bit_length()
    cut_ref[...] = jnp.full(cut_ref.shape, 2 ** nbits, I32)

    @pl.when(jnp.max(n_tied - need) > 0.0)
    def _():
        cut = _tie_cutoff(
            lambda c: count(lambda k, r0: jnp.where(k == thr, jnp.where(kpos0 + r0 < c, 1.0, 0.0), 0.0)), need, nbits, row1)
        cut_ref[...] = jnp.broadcast_to(cut, cut_ref.shape)

    cut = cut_ref[0:1, :]

    def select_block(j, c):
        r0 = pl.multiple_of(j * LANE, LANE)
        k = key_ref[pl.ds(r0, LANE), :]
        kpos = kpos0 + r0
        chosen = jnp.where(k > thr, 1.0, jnp.where(k == thr, jnp.where(kpos < cut, 1.0, 0.0), 0.0))
        sel_ref[pl.ds(r0, LANE), :] = jnp.where(kpos <= qpos, chosen, 0.0)
        return c

    lax.fori_loop(0, n_kb, select_block, 0)

    for n in range(KV_HEADS):
        q_n = jnp.concatenate(
            [qq_ref[:, (GROUP * n + g) * HEAD_DIM:(GROUP * n + g + 1) * HEAD_DIM] for g in range(GROUP)], axis=0)
        qn_ref[n] = (q_n * HEAD_DIM ** -0.5).astype(BF16)
    acc_ref[...] = jnp.zeros(acc_ref.shape, F32)

    def key_block(j, carry):
        ms, ls = carry
        r0 = pl.multiple_of(j * LANE, LANE)
        mask = sel_ref[pl.ds(r0, LANE), :] > 0.5
        near = jnp.minimum(i - j, 2)
        new_ms, new_ls = [], []
        for n in range(KV_HEADS):
            s = _dot_nt(kb_ref[n, pl.ds(r0, LANE), :], qn_ref[n]) + bias_ref[n, near]
            s = jnp.concatenate([jnp.where(mask, s[:, g * LANE:(g + 1) * LANE], NEG_BIG) for g in range(GROUP)], axis=1)
            m_new = jnp.maximum(ms[n], jnp.max(s, axis=0, keepdims=True))
            alpha = jnp.exp(ms[n] - m_new)
            p = jnp.exp(s - m_new)
            new_ls.append(alpha * ls[n] + jnp.sum(p, axis=0, keepdims=True))
            new_ms.append(m_new)
            acc_ref[n] = alpha * acc_ref[n] + _dot(vt_ref[j, n * HEAD_DIM:(n + 1) * HEAD_DIM, :], p.astype(BF16))
        return tuple(new_ms), tuple(new_ls)

    init = (tuple(jnp.full((1, gq), NEG_BIG, F32) for _ in range(KV_HEADS)),
            tuple(jnp.zeros((1, gq), F32) for _ in range(KV_HEADS)))
    _, ls = lax.fori_loop(0, n_kb, key_block, init)
    outs = []
    for n in range(KV_HEADS):
        o = acc_ref[n] / ls[n]
        outs += [o[:, g * LANE:(g + 1) * LANE].T for g in range(GROUP)]
    y_ref[...] = jnp.concatenate(outs, axis=1)


def _prompt_attn(qq, kvi, bias_tiles, n_batch, n_blocks, n_sel):
    t_pad = n_blocks * LANE
    kern = functools.partial(_prompt_attn_kernel, n_blocks=n_blocks, n_sel=n_sel)
    return pl.pallas_call(
        kern,
        grid=(n_batch, n_blocks),
        in_specs=[
            pl.BlockSpec((LANE, 2 * ATTN_WIDTH), lambda b, i: (b * n_blocks + i, 0)),
            pl.BlockSpec((t_pad, kvi.shape[1]), lambda b, i: (b, 0)),
            pl.BlockSpec((LANE, LANE), lambda b, i: (b * n_blocks + i, (2 * KV_WIDTH + IDX_DIM) // LANE)),
            pl.BlockSpec(bias_tiles.shape, lambda b, i: (0, 0, 0, 0)),
        ],
        out_specs=pl.BlockSpec((LANE, ATTN_WIDTH), lambda b, i: (b * n_blocks + i, 0)),
        out_shape=jax.ShapeDtypeStruct((n_batch * t_pad, ATTN_WIDTH), F32),
        scratch_shapes=[
            pltpu.VMEM((n_blocks, KV_WIDTH, LANE), BF16),
            pltpu.VMEM((KV_HEADS, t_pad, HEAD_DIM), BF16),
            pltpu.VMEM((t_pad, IDX_DIM), BF16),
            pltpu.VMEM((KV_HEADS, GROUP * LANE, HEAD_DIM), BF16),
            pltpu.VMEM((t_pad, LANE), I32),
            pltpu.VMEM((t_pad, LANE), F32),
            pltpu.VMEM((8, LANE), I32),
            pltpu.VMEM((KV_HEADS, HEAD_DIM, GROUP * LANE), F32),
        ],
        compiler_params=_params("parallel", "arbitrary"),
    )(qq, kvi, kvi, bias_tiles)


PAGES_PER_STEP = 8
Q_PAD = 8


def _sample_score_kernel(pt_ref, qi_ref, wb_ref, *refs):
    page_refs, out_ref = refs[:PAGES_PER_STEP], refs[PAGES_PER_STEP]
    qi = qi_ref[...].astype(BF16)
    wb = wb_ref[...]
    for u in range(PAGES_PER_STEP):
        s = jnp.maximum(_dot_nt(qi, page_refs[u][...].astype(BF16)), 0.0) * wb
        acc = s[0:Q_PAD]
        for h in range(1, IDX_HEADS):
            acc = acc + s[h * Q_PAD:(h + 1) * Q_PAD]
        out_ref[u] = acc


def _sample_scores(page_table, qi8, wb, cache_kidx):
    n_batch, n_pages = page_table.shape
    page_spec = lambda u: pl.BlockSpec((None, PAGE, IDX_DIM), lambda b, s, pt: (pt[b, s * PAGES_PER_STEP + u], 0, 0))
    per_batch = pl.BlockSpec((None, IDX_HEADS * Q_PAD, IDX_DIM), lambda b, s, pt: (b, 0, 0))
    return pl.pallas_call(
        _sample_score_kernel,
        grid_spec=pltpu.PrefetchScalarGridSpec(
            num_scalar_prefetch=1,
            grid=(n_batch, n_pages // PAGES_PER_STEP),
            in_specs=[per_batch, per_batch] + [page_spec(u) for u in range(PAGES_PER_STEP)],
            out_specs=pl.BlockSpec((None, PAGES_PER_STEP, Q_PAD, PAGE), lambda b, s, pt: (b, s, 0, 0)),
        ),
        out_shape=jax.ShapeDtypeStruct((n_batch, n_pages, Q_PAD, PAGE), F32),
        compiler_params=_params("parallel", "arbitrary"),
    )(page_table, qi8, wb, *([cache_kidx] * PAGES_PER_STEP))


def _sample_select_kernel(sc_ref, qi_ref, wb_ref, kin_ref, sel_ref, key_ref, *, n_pages, dec_seq, n_sel):
    qrow = lax.broadcasted_iota(I32, (Q_PAD, PAGE), 0)
    lane = lax.broadcasted_iota(I32, (Q_PAD, PAGE), 1)
    s = jnp.maximum(_dot_nt(qi_ref[...].astype(BF16), kin_ref[...].astype(BF16)), 0.0) * wb_ref[...]
    acc = s[0:Q_PAD]
    for h in range(1, IDX_HEADS):
        acc = acc + s[h * Q_PAD:(h + 1) * Q_PAD]
    new_valid = jnp.where(lane < dec_seq, jnp.where(lane <= qrow, 1, 0), 0) > 0
    past = sc_ref[...]
    key_ref[0:n_pages] = _float_key(jnp.where(past == 0.0, 0.0, past))
    key_ref[n_pages] = _float_key(jnp.where(new_valid, jnp.where(acc == 0.0, 0.0, acc), -jnp.inf))

    def lane_count(x):
        return jnp.sum(jnp.sum(x, axis=0), axis=1, keepdims=True)

    col1 = (Q_PAD, 1)
    thr = _kth_largest_key(lambda t: lane_count(jnp.where(key_ref[...] >= t, 1.0, 0.0)), n_sel, col1)
    keys = key_ref[...]
    need = n_sel - lane_count(jnp.where(keys > thr, 1.0, 0.0))
    shape3 = (n_pages + 1, Q_PAD, PAGE)
    kidx = lax.broadcasted_iota(I32, shape3, 0) * PAGE + lax.broadcasted_iota(I32, shape3, 2)
    cut = _tie_cutoff(
        lambda c: lane_count(jnp.where(key_ref[...] == thr, jnp.where(kidx < c, 1.0, 0.0), 0.0)),
        need, ((n_pages + 1) * PAGE).bit_length(), col1)
    chosen = jnp.where(keys > thr, 1.0, jnp.where(keys == thr, jnp.where(kidx < cut, 1.0, 0.0), 0.0))
    sel_ref[0:n_pages] = chosen[0:n_pages]
    sel_ref[n_pages] = jnp.where(new_valid, chosen[n_pages], 0.0)


def _sample_select(sc, qi8, wb, ki_new, dec_seq, n_sel):
    n_batch, n_pages = sc.shape[:2]
    kern = functools.partial(_sample_select_kernel, n_pages=n_pages, dec_seq=dec_seq, n_sel=n_sel)
    per_batch = lambda a: pl.BlockSpec((None,) + a.shape[1:], lambda b: (b,) + (0,) * (a.ndim - 1))
    return pl.pallas_call(
        kern,
        grid=(n_batch,),
        in_specs=[per_batch(sc), per_batch(qi8), per_batch(wb), per_batch(ki_new)],
        out_specs=pl.BlockSpec((None, n_pages + 1, Q_PAD, PAGE), lambda b: (b, 0, 0, 0)),
        out_shape=jax.ShapeDtypeStruct((n_batch, n_pages + 1, Q_PAD, PAGE), F32),
        scratch_shapes=[pltpu.VMEM((n_pages + 1, Q_PAD, PAGE), I32)],
        compiler_params=_params("parallel"),
    )(sc, qi8, wb, ki_new)


def _sample_attn_kernel(pt_ref, q_ref, sel_ref, selnew_ref, knew_ref, vnew_ref, bias_ref, *refs, n_steps):
    k_refs = refs[:PAGES_PER_STEP]
    v_refs = refs[PAGES_PER_STEP:2 * PAGES_PER_STEP]
    o_ref, m_ref, l_ref, acc_ref = refs[2 * PAGES_PER_STEP:]
    s_id = pl.program_id(1)
    rows = KV_HEADS * GROUP * Q_PAD
    per_kv = GROUP * Q_PAD
    q = (q_ref[...] * HEAD_DIM ** -0.5).astype(BF16)

    @pl.when(s_id == 0)
    def _():
        m_ref[...] = jnp.full(m_ref.shape, NEG_BIG, F32)
        l_ref[...] = jnp.zeros(l_ref.shape, F32)
        acc_ref[...] = jnp.zeros(acc_ref.shape, F32)

    def attend(k_pages, v_pages, sel_pages, bias_pages):
        s_blocks, sel_blocks = [], []
        for kp, sp, bp in zip(k_pages, sel_pages, bias_pages):
            kb = kp.astype(BF16)
            s = jnp.concatenate(
                [_dot_nt(q[n * per_kv:(n + 1) * per_kv], kb[:, n * HEAD_DIM:(n + 1) * HEAD_DIM]) for n in range(KV_HEADS)],
                axis=0) + bp
            s_blocks.append(s)
            sel_blocks.append(jnp.concatenate([sp] * (KV_HEADS * GROUP), axis=0) > 0.5)
        s = jnp.concatenate(s_blocks, axis=1)
        sel = jnp.concatenate(sel_blocks, axis=1)
        s = jnp.where(sel, s, NEG_BIG)
        m_old = m_ref[...]
        m_new = jnp.maximum(m_old, jnp.max(s, axis=1, keepdims=True))
        alpha = jnp.exp(m_old - m_new)
        p = jnp.where(sel, jnp.exp(s - m_new), 0.0)
        l_ref[...] = alpha * l_ref[...] + jnp.sum(p, axis=1, keepdims=True)
        m_ref[...] = m_new
        pb = p.astype(BF16)
        pv = None
        for u, vp in enumerate(v_pages):
            vb = vp.astype(BF16)
            pu = pb[:, u * PAGE:(u + 1) * PAGE]
            part = jnp.concatenate(
                [_dot(pu[n * per_kv:(n + 1) * per_kv], vb[:, n * HEAD_DIM:(n + 1) * HEAD_DIM]) for n in range(KV_HEADS)], axis=0)
            pv = part if pv is None else pv + part
        acc_ref[...] = alpha * acc_ref[...] + pv

    @pl.when(s_id < n_steps)
    def _():
        far, near = bias_ref[0], bias_ref[1]
        biases = [far] * PAGES_PER_STEP
        last = s_id == n_steps - 1
        biases[-1] = jnp.where(last, near, far)
        attend([r[...] for r in k_refs], [r[...] for r in v_refs], [sel_ref[u] for u in range(PAGES_PER_STEP)], biases)

    @pl.when(s_id == n_steps)
    def _():
        attend([knew_ref[...]], [vnew_ref[...]], [selnew_ref[...]], [bias_ref[2]])
        o_ref[...] = acc_ref[...] / l_ref[...]


def _sample_attn(page_table, q8, sel, k_new, v_new, bias_tiles, cache_k, cache_v):
    n_batch, n_pages = page_table.shape
    n_steps = n_pages // PAGES_PER_STEP
    rows = KV_HEADS * GROUP * Q_PAD
    kern = functools.partial(_sample_attn_kernel, n_steps=n_steps)

    def page_spec(u):
        return pl.BlockSpec((None, PAGE, KV_WIDTH),
                            lambda b, s, pt: (pt[b, jnp.minimum(s, n_steps - 1) * PAGES_PER_STEP + u], 0, 0))

    per_batch = lambda a: pl.BlockSpec((None,) + a.shape[1:], lambda b, s, pt: (b,) + (0,) * (a.ndim - 1))
    return pl.pallas_call(
        kern,
        grid_spec=pltpu.PrefetchScalarGridSpec(
            num_scalar_prefetch=1,
            grid=(n_batch, n_steps + 1),
            in_specs=[
                per_batch(q8),
                pl.BlockSpec((None, PAGES_PER_STEP, Q_PAD, PAGE), lambda b, s, pt: (b, jnp.minimum(s, n_steps - 1), 0, 0)),
                pl.BlockSpec((None, None, Q_PAD, PAGE), lambda b, s, pt: (b, n_pages, 0, 0)),
                per_batch(k_new), per_batch(v_new),
                pl.BlockSpec(bias_tiles.shape, lambda b, s, pt: (0, 0, 0)),
            ] + [page_spec(u) for u in range(PAGES_PER_STEP)] * 2,
            out_specs=pl.BlockSpec((None, rows, HEAD_DIM), lambda b, s, pt: (b, 0, 0)),
            scratch_shapes=[pltpu.VMEM((rows, 1), F32), pltpu.VMEM((rows, 1), F32), pltpu.VMEM((rows, HEAD_DIM), F32)],
        ),
        out_shape=jax.ShapeDtypeStruct((n_batch, rows, HEAD_DIM), F32),
        compiler_params=_params("parallel", "arbitrary"),
    )(page_table, q8, sel, sel, k_new, v_new, bias_tiles, *([cache_k] * PAGES_PER_STEP), *([cache_v] * PAGES_PER_STEP))


def _mix_kernel(ys_ref, bonus_ref, g_ref, ya_ref, xn_ref, gng_ref, gnb_ref, ones_ref, wor_ref, woa_ref, lg_ref, lb_ref,
                x1_ref, x1b_ref, x1t_ref):
    ones_bd = ones_ref[...]
    ys = ys_ref[...]
    inv = 1.0 / HEAD_DIM
    yc = ys - _segsum(ys, ones_bd) * inv
    var = _segsum(yc * yc, ones_bd) * inv
    yr = (yc * lax.rsqrt(var + GN_EPS) * gng_ref[...] + gnb_ref[...] + bonus_ref[...]) * g_ref[...]
    mix = _dot(yr.astype(BF16), wor_ref[...]) + _dot(ya_ref[...].astype(BF16), woa_ref[...])
    x1 = _ln(DN_ALPHA * xn_ref[...] + mix, lg_ref[...], lb_ref[...])
    x1_ref[...] = x1
    x1b_ref[...] = x1.astype(BF16)
    x1t_ref[...] = x1.T.astype(BF16)


def _mix(ys, bonus, g, ya, xn, gn_g, gn_b, ones_bd, wo_r, wo_a, ln_g, ln_b, tm):
    rows = xn.shape[0]
    half = pl.BlockSpec((tm, RWKV_WIDTH), lambda i: (i, 0))
    row = pl.BlockSpec((tm, D_MODEL), lambda i: (i, 0))
    vec = lambda n: pl.BlockSpec((1, n), lambda i: (0, 0))
    full = lambda a: pl.BlockSpec(a.shape, lambda i: (0, 0))
    return pl.pallas_call(
        _mix_kernel,
        grid=(rows // tm,),
        in_specs=[half, half, half, half, row, vec(RWKV_WIDTH), vec(RWKV_WIDTH), full(ones_bd), full(wo_r), full(wo_a),
                  vec(D_MODEL), vec(D_MODEL)],
        out_specs=[row, row, pl.BlockSpec((D_MODEL, tm), lambda i: (0, i))],
        out_shape=[jax.ShapeDtypeStruct((rows, D_MODEL), F32), jax.ShapeDtypeStruct((rows, D_MODEL), BF16),
                   jax.ShapeDtypeStruct((D_MODEL, rows), BF16)],
        compiler_params=_params("parallel"),
    )(ys, bonus, g, ya, xn, gn_g, gn_b, ones_bd, wo_r, wo_a, ln_g, ln_b)


CAND_PAIRS = [(c, d) for c in range(PEER_TOPK) for d in range(PEER_TOPK) if (c + 1) * (d + 1) <= PEER_TOPK]


def _top_rows(x, n):
    rows = []
    for _ in range(n):
        m = jnp.max(x, axis=0, keepdims=True)
        rows.append(m)
        x = jnp.where(x == m, -jnp.inf, x)
    return rows


def _peer_route_kernel(x_ref, wq_ref, sub_ref, s1_ref, s2_ref, e2_ref, thr_ref, m1_ref, zinv_ref):
    q = _dot(x_ref[...], wq_ref[...])
    for h in range(PEER_HEADS):
        base = h * 2 * PEER_HALF
        s1 = _dot_nt_hp(sub_ref[h, 0], q[:, base:base + PEER_HALF])
        s2 = _dot_nt_hp(sub_ref[h, 1], q[:, base + PEER_HALF:base + 2 * PEER_HALF])
        top1 = _top_rows(s1, PEER_TOPK)
        top2 = _top_rows(s2, PEER_TOPK)
        cand = jnp.concatenate([top1[c] + top2[d] for c, d in CAND_PAIRS]
                               + [jnp.full_like(top1[0], -jnp.inf)] * (-len(CAND_PAIRS) % 8), axis=0)
        best = _top_rows(cand, PEER_TOPK)
        thr = best[-1]
        m = top1[0] + top2[0]
        z = jnp.sum(jnp.where(cand >= thr, jnp.exp(cand - m), 0.0), axis=0, keepdims=True)
        s1_ref[h] = s1
        s2_ref[h] = s2
        e2_ref[h] = jnp.exp(s2 - top2[0])
        thr_ref[h:h + 1, :] = thr
        m1_ref[h:h + 1, :] = top1[0]
        zinv_ref[h:h + 1, :] = 1.0 / z


def _peer_route(x1b, wq, subkeys):
    rows = x1b.shape[0]
    tm = LANE
    stat = pl.BlockSpec((PEER_HEADS, tm), lambda i: (0, i))
    big = pl.BlockSpec((PEER_HEADS, PEER_NKEYS, tm), lambda i: (0, 0, i))
    stat_shape = jax.ShapeDtypeStruct((PEER_HEADS, rows), F32)
    big_shape = jax.ShapeDtypeStruct((PEER_HEADS, PEER_NKEYS, rows), F32)
    return pl.pallas_call(
        _peer_route_kernel,
        grid=(rows // tm,),
        in_specs=[pl.BlockSpec((tm, D_MODEL), lambda i: (i, 0)),
                  pl.BlockSpec(wq.shape, lambda i: (0, 0)),
                  pl.BlockSpec(subkeys.shape, lambda i: (0, 0, 0, 0))],
        out_specs=[big, big, big, stat, stat, stat],
        out_shape=[big_shape, big_shape, big_shape, stat_shape, stat_shape, stat_shape],
        compiler_params=_params("parallel"),
    )(x1b, wq, subkeys)


EXPERT_ROWS = 8
MXU_DEPTH = 256


def _peer_dense_kernel(xt_ref, u_ref, vt_ref, s1_ref, s2_ref, e2_ref, thr_ref, m1_ref, zinv_ref, o_ref, *scratch):
    n_sub = EXPERT_ROWS * PEER_NKEYS // MXU_DEPTH
    ht_refs, g_refs, a_refs = scratch[:n_sub], scratch[n_sub:2 * n_sub], scratch[2 * n_sub:]
    j = pl.program_id(1)
    tm = xt_ref.shape[1]

    @pl.when(j == 0)
    def _():
        o_ref[...] = jnp.zeros(o_ref.shape, F32)

    def gates(p):
        for e in range(MXU_DEPTH // PEER_NKEYS):
            i = j * EXPERT_ROWS + p * (MXU_DEPTH // PEER_NKEYS) + e
            rows = slice(e * PEER_NKEYS, (e + 1) * PEER_NKEYS)
            s1_i = [s1_ref[h, pl.ds(i, 1), :] for h in range(PEER_HEADS)]
            f_i = [jnp.exp(s1_i[h] - m1_ref[h:h + 1, :]) * zinv_ref[h:h + 1, :] for h in range(PEER_HEADS)]
            for c in range(0, tm, LANE):
                cols = slice(c, c + LANE)
                gate = None
                for h in range(PEER_HEADS):
                    g = jnp.where(s1_i[h][:, cols] + s2_ref[h, :, cols] >= thr_ref[h:h + 1, cols],
                                  e2_ref[h, :, cols] * f_i[h][:, cols], 0.0)
                    gate = g if gate is None else gate + g
                g_refs[p][rows, cols] = gate

    def up(p):
        r = p * MXU_DEPTH
        ht_refs[p][...] = _dot(u_ref[r:r + MXU_DEPTH, :], xt_ref[...])

    def down(p):
        r = p * MXU_DEPTH
        he = ht_refs[p][...]
        a_refs[p][...] = (0.5 * he * (1.0 + lax.erf(he * (2.0 ** -0.5))) * g_refs[p][...]).astype(BF16)
        o_ref[...] += _dot(vt_ref[:, r:r + MXU_DEPTH], a_refs[p][...])

    gates(0)
    up(0)
    for p in range(n_sub):
        if p + 1 < n_sub:
            gates(p + 1)
            up(p + 1)
        down(p)


def _peer_dense(x1t, u, vt, s1, s2, e2, thr, m1, zinv, tm):
    rows = x1t.shape[1]
    eb = EXPERT_ROWS * PEER_NKEYS
    stat = pl.BlockSpec((PEER_HEADS, tm), lambda i, j: (0, i))
    big = pl.BlockSpec((PEER_HEADS, PEER_NKEYS, tm), lambda i, j: (0, 0, i))
    return pl.pallas_call(
        _peer_dense_kernel,
        grid=(rows // tm, PEER_EXPERTS // eb),
        in_specs=[pl.BlockSpec((D_MODEL, tm), lambda i, j: (0, i)),
                  pl.BlockSpec((eb, D_MODEL), lambda i, j: (j, 0)),
                  pl.BlockSpec((D_MODEL, eb), lambda i, j: (0, j)),
                  big, big, big, stat, stat, stat],
        out_specs=pl.BlockSpec((D_MODEL, tm), lambda i, j: (0, i)),
        out_shape=jax.ShapeDtypeStruct((D_MODEL, rows), F32),
        scratch_shapes=([pltpu.VMEM((MXU_DEPTH, tm), F32)] * (2 * eb // MXU_DEPTH)
                        + [pltpu.VMEM((MXU_DEPTH, tm), BF16)] * (eb // MXU_DEPTH)),
        compiler_params=_params("parallel", "arbitrary"),
    )(x1t, u, vt, s1, s2, e2, thr, m1, zinv)


def _rel_buckets(dist):
    max_exact = REL_BUCKETS // 2
    d = np.maximum(dist, 0)
    ratio = np.log(np.maximum(d, 1).astype(np.float32) / np.float32(max_exact)) / np.float32(math.log(REL_MAX_DIST / max_exact))
    log_b = max_exact + (ratio * np.float32(REL_BUCKETS - max_exact)).astype(np.int32)
    return np.where(d < max_exact, d, np.minimum(log_b, REL_BUCKETS - 1)).astype(np.int32)


def _prompt_bias_tiles(rel_bias):
    kk = np.arange(LANE)[:, None]
    qq = np.arange(LANE)[None, :]
    tiles = []
    for delta in range(3):
        b = rel_bias[_rel_buckets(delta * LANE + qq - kk)]
        b = b.reshape(LANE, LANE, KV_HEADS, GROUP).transpose(2, 0, 3, 1).reshape(KV_HEADS, LANE, GROUP * LANE)
        tiles.append(b)
    return jnp.stack(tiles, axis=1).astype(F32)


def _sample_bias_tiles(rel_bias, past_len, dec_seq):
    q = np.arange(Q_PAD)[:, None]
    off = np.arange(PAGE)[None, :]
    qpos = past_len + np.minimum(q, dec_seq - 1)
    dists = [qpos - 0 * off - (past_len - 2 * PAGE), qpos - (past_len - PAGE + off), qpos - (past_len + np.minimum(off, dec_seq - 1))]
    tiles = []
    for d in dists:
        b = rel_bias[_rel_buckets(d + 0 * off)]
        tiles.append(b.transpose(2, 0, 1).reshape(ATTN_HEADS * Q_PAD, PAGE))
    return jnp.stack(tiles).astype(F32)


def kernel(x_prompt, x_sample, cache_k, cache_v, cache_kidx, state_wkv, state_shift, page_table, meta_tokens, ln_in_g, ln_in_b, rel_bias, w_in, mu_shift, w0, w_up, a0, a_up, g_up, k_k, k_a, r_k, gn_g, gn_b, w_o, ln1_g, ln1_b, peer_wq, peer_subkeys, peer_u, peer_v, ln2_g, ln2_b):
    n_batch, seq, _ = x_prompt.shape
    dec_batch, dec_seq, _ = x_sample.shape
    n_pages = page_table.shape[1]
    past_len = n_pages * PAGE
    t_len = seq + N_META
    n_blocks = -(-t_len // LANE)
    t_pad = n_blocks * LANE
    rows_p = n_batch * t_pad
    rows_s = dec_batch * dec_seq
    assert rows_s == LANE and Q_PAD >= dec_seq and n_pages % PAGES_PER_STEP == 0
    assert past_len >= 2 * PAGE + REL_MAX_DIST
    rows = -(-(rows_p + rows_s) // ROW_ALIGN) * ROW_ALIGN
    layer = 0

    meta = jnp.broadcast_to(meta_tokens[None], (n_batch, N_META, D_MODEL))
    xp = jnp.pad(jnp.concatenate([meta, x_prompt], axis=1), ((0, 0), (0, t_pad - t_len), (0, 0)))
    x_all = jnp.concatenate([xp.reshape(rows_p, D_MODEL), x_sample.reshape(rows_s, D_MODEL),
                             jnp.zeros((rows - rows_p - rows_s, D_MODEL), F32)], axis=0)
    xn, xb = _ln_in(x_all, ln_in_g, ln_in_b, 256)

    w = w_in[layer]
    c0 = RWKV_COLS
    w_rwkv = jnp.pad(w[:, :c0], ((0, 0), (0, RWKV_PAD - RWKV_COLS))).astype(BF16)
    w_qq = jnp.concatenate([w[:, c0:c0 + ATTN_WIDTH], w[:, c0 + ATTN_WIDTH + 2 * KV_WIDTH:c0 + 2 * ATTN_WIDTH + 2 * KV_WIDTH]], axis=1).astype(BF16)
    c_ki = c0 + 2 * ATTN_WIDTH + 2 * KV_WIDTH
    w_kvi = jnp.concatenate([w[:, c0 + ATTN_WIDTH:c0 + ATTN_WIDTH + 2 * KV_WIDTH], w[:, c_ki:c_ki + IDX_DIM],
                             jnp.pad(w[:, c_ki + IDX_DIM:], ((0, 0), (0, LANE - IDX_HEADS)))], axis=1).astype(BF16)
    feat = _matmul(xb, w_rwkv, 640, RWKV_PAD // 3)
    qq = _matmul(xb, w_qq, 640, 1024)
    kvi = _matmul(xb, w_kvi, 640, w_kvi.shape[1])

    def prompt_rows(a):
        return a[:rows_p].reshape(n_batch, t_pad, -1)[:, :t_len]

    def sample_rows(a):
        return a[rows_p:rows_p + rows_s].reshape(dec_batch, dec_seq, -1)

    ones_bd = jnp.asarray(np.kron(np.eye(RWKV_HEADS), np.ones((HEAD_DIM, HEAD_DIM))), BF16)
    pad_cols = lambda a: jnp.pad(a, ((0, 0), (0, RWKV_PAD - RWKV_COLS)))
    init = jnp.zeros((dec_batch, dec_seq, RWKV_PAD), F32).at[:, 0].set(pad_cols(state_shift[layer]))
    init = jnp.concatenate([jnp.zeros((LANE, RWKV_PAD), F32), init.reshape(rows_s, RWKV_PAD)], axis=0)
    wup = jnp.pad(w_up[layer], ((0, ICLR_LORA), (0, 0)))
    aup = jnp.pad(a_up[layer], ((DECAY_LORA, 0), (0, 0)))
    gup = jnp.pad(g_up[layer], ((0, GATE_PAD - GATE_LORA), (0, 0)))
    vec = lambda a: a.reshape(1, -1)
    pre = _rwkv_pre(feat, init, vec(pad_cols(mu_shift[layer][None])), vec(w0[layer]), vec(a0[layer]), vec(k_k[layer]),
                    vec(k_a[layer]), vec(r_k[layer]), wup, aup, gup, ones_bd, rows_p // LANE, n_blocks, dec_seq)
    r_all, w_all, k_all, v_all, kn_all, b_all, g_all, bonus_all = pre

    def to_scan(a, take, nb, steps):
        return take(a).reshape(nb, steps, RWKV_HEADS, HEAD_DIM).transpose(1, 3, 0, 2).reshape(steps, HEAD_DIM, nb * RWKV_HEADS)

    def from_scan(y, nb, steps):
        return y.reshape(steps, HEAD_DIM, nb, RWKV_HEADS).transpose(2, 0, 3, 1).reshape(nb, steps, RWKV_WIDTH)

    def state_in(s):
        nb = s.shape[0]
        return s.transpose(2, 3, 0, 1).reshape(HEAD_DIM, HEAD_DIM, nb * RWKV_HEADS)

    def state_out(s, nb):
        return s.reshape(HEAD_DIM, HEAD_DIM, nb, RWKV_HEADS).transpose(2, 3, 0, 1)

    seqs = (r_all, w_all, k_all, v_all, kn_all, b_all)
    tc = next(c for c in (48, 43, 32, 16, 8, 4, 2, 1) if t_len % c == 0)
    y_p, wkv_p = _rwkv_scan(*[to_scan(a, prompt_rows, n_batch, t_len) for a in seqs],
                            jnp.zeros((HEAD_DIM, HEAD_DIM, n_batch * RWKV_HEADS), F32), tc)
    y_s, wkv_s = _rwkv_scan(*[to_scan(a, sample_rows, dec_batch, dec_seq) for a in seqs],
                            state_in(state_wkv[layer]), dec_seq)
    ys_all = jnp.concatenate([
        jnp.pad(from_scan(y_p, n_batch, t_len), ((0, 0), (0, t_pad - t_len), (0, 0))).reshape(rows_p, RWKV_WIDTH),
        from_scan(y_s, dec_batch, dec_seq).reshape(rows_s, RWKV_WIDTH),
        jnp.zeros((rows - rows_p - rows_s, RWKV_WIDTH), F32)], axis=0)

    ya_p = _prompt_attn(qq, kvi, _prompt_bias_tiles(rel_bias), n_batch, n_blocks, min(IDX_TOPK, t_len // 4))

    qq_s, kvi_s = sample_rows(qq), sample_rows(kvi)

    def pad_q(a):
        a = jnp.pad(a.transpose(0, 2, 1, 3), ((0, 0), (0, 0), (0, Q_PAD - dec_seq), (0, 0)))
        return a.reshape(dec_batch, -1, a.shape[-1])

    qi8 = pad_q(qq_s[..., ATTN_WIDTH:].reshape(dec_batch, dec_seq, IDX_HEADS, IDX_DIM))
    wi_s = kvi_s[..., 2 * KV_WIDTH + IDX_DIM:2 * KV_WIDTH + IDX_DIM + IDX_HEADS]
    wb = jnp.broadcast_to(pad_q(wi_s[..., None]), (dec_batch, IDX_HEADS * Q_PAD, IDX_DIM))
    q8 = pad_q(qq_s[..., :ATTN_WIDTH].reshape(dec_batch, dec_seq, ATTN_HEADS, HEAD_DIM))
    pad_keys = lambda a: jnp.pad(a, ((0, 0), (0, PAGE - dec_seq), (0, 0)))
    ki_new = pad_keys(kvi_s[..., 2 * KV_WIDTH:2 * KV_WIDTH + IDX_DIM])
    k_new = pad_keys(kvi_s[..., :KV_WIDTH])
    v_new = pad_keys(kvi_s[..., KV_WIDTH:2 * KV_WIDTH])
    n_pool = cache_k.shape[1]
    sc = _sample_scores(page_table, qi8, wb, cache_kidx[layer])
    sel = _sample_select(sc, qi8, wb, ki_new, dec_seq, min(IDX_TOPK, (past_len + dec_seq) // 4))
    o_s = _sample_attn(page_table, q8, sel, k_new, v_new, _sample_bias_tiles(rel_bias, past_len, dec_seq),
                       cache_k[layer].reshape(n_pool, PAGE, KV_WIDTH), cache_v[layer].reshape(n_pool, PAGE, KV_WIDTH))
    ya_s = o_s.reshape(dec_batch, ATTN_HEADS, Q_PAD, HEAD_DIM)[:, :, :dec_seq].transpose(0, 2, 1, 3).reshape(rows_s, ATTN_WIDTH)
    ya_all = jnp.concatenate([ya_p, ya_s, jnp.zeros((rows - rows_p - rows_s, ATTN_WIDTH), F32)], axis=0)

    wo = w_o[layer].astype(BF16)
    x1, x1b, x1t = _mix(ys_all, bonus_all, g_all, ya_all, xn, vec(gn_g[layer]), vec(gn_b[layer]), ones_bd,
                        wo[:RWKV_WIDTH], wo[RWKV_WIDTH:], vec(ln1_g[layer]), vec(ln1_b[layer]), 256)
    routing = _peer_route(x1b, peer_wq[layer].astype(BF16), peer_subkeys[layer])
    peer_t = _peer_dense(x1t, peer_u[layer].astype(BF16), peer_v[layer].T.astype(BF16), *routing, 512)
    y = _ln_out(x1, peer_t, ln2_g[layer], ln2_b[layer], 256)

    feat_p, feat_s = prompt_rows(feat), sample_rows(feat)
    kvi_p = prompt_rows(kvi)
    kv4 = lambda a, nb, steps: a.reshape(nb, steps, KV_HEADS, HEAD_DIM)[None]
    return (
        prompt_rows(y)[:, N_META:], sample_rows(y),
        kv4(kvi_p[..., :KV_WIDTH], n_batch, t_len), kv4(kvi_p[..., KV_WIDTH:2 * KV_WIDTH], n_batch, t_len),
        kvi_p[..., 2 * KV_WIDTH:2 * KV_WIDTH + IDX_DIM][None],
        state_out(wkv_p, n_batch)[None], feat_p[:, -1, :RWKV_COLS][None],
        kv4(kvi_s[..., :KV_WIDTH], dec_batch, dec_seq), kv4(kvi_s[..., KV_WIDTH:2 * KV_WIDTH], dec_batch, dec_seq),
        kvi_s[..., 2 * KV_WIDTH:2 * KV_WIDTH + IDX_DIM][None],
        state_out(wkv_s, dec_batch)[None], feat_s[:, -1, :RWKV_COLS][None],
    )
```

```python
import functools
import math

import numpy as np
import jax
import jax.numpy as jnp
from jax import lax
from jax.experimental import pallas as pl
from jax.experimental.pallas import tpu as pltpu

F32, BF16, I32 = jnp.float32, jnp.bfloat16, jnp.int32

D_MODEL = 2048
N_META = 16
HEAD_DIM = 64
RWKV_WIDTH = 1024
ATTN_WIDTH = 1024
RWKV_HEADS = 16
ATTN_HEADS = 16
KV_HEADS = 4
GROUP = 4
KV_WIDTH = 256
DECAY_LORA = 64
ICLR_LORA = 64
GATE_LORA = 160
RWKV_COLS = 3 * RWKV_WIDTH + DECAY_LORA + ICLR_LORA + GATE_LORA
RWKV_PAD = 3456
LORA_WA = DECAY_LORA + ICLR_LORA
GATE_PAD = RWKV_PAD - 3 * RWKV_WIDTH - LORA_WA
GN_EPS = 64e-5
IDX_HEADS = 8
IDX_DIM = 128
IDX_TOPK = 256
REL_BUCKETS = 32
REL_MAX_DIST = 128
PEER_HEADS = 8
PEER_NKEYS = 128
PEER_HALF = 128
PEER_TOPK = 16
PEER_EXPERTS = PEER_NKEYS * PEER_NKEYS
DN_ALPHA = 2.0 ** 0.25
LN_EPS = 1e-5
PAGE = 128
LANE = 128
ROW_ALIGN = 2560
VMEM_LIMIT = 56 * 1024 * 1024
INT_MIN = -2 ** 31
NEG_BIG = -1e30
NEG_INF_KEY = int(np.int32(np.uint32(0xFF800000) ^ np.uint32(0x7FFFFFFF)))


def _params(*sem):
    return pltpu.CompilerParams(dimension_semantics=sem, vmem_limit_bytes=VMEM_LIMIT)


def _dot(a, b):
    return jnp.dot(a, b, preferred_element_type=F32)


def _dot_nt(a, b):
    return lax.dot_general(a, b, (((1,), (1,)), ((), ())), preferred_element_type=F32)


def _split2(x):
    hi = x.astype(BF16)
    lo = (x - hi.astype(F32)).astype(BF16)
    return hi, lo


def _split3(x):
    hi = x.astype(BF16)
    r1 = x - hi.astype(F32)
    mid = r1.astype(BF16)
    lo = (r1 - mid.astype(F32)).astype(BF16)
    return hi, mid, lo


def _dot_hp(a, b):
    ah, al = _split2(a)
    bh, bl = _split2(b)
    return _dot(ah, bh) + (_dot(ah, bl) + _dot(al, bh))


def _dot_nt_hp(a, b):
    ah, al = _split2(a)
    bh, bl = _split2(b)
    return _dot_nt(ah, bh) + (_dot_nt(ah, bl) + _dot_nt(al, bh))


def _segsum(x, ones_bd):
    hi, mid, lo = _split3(x)
    return _dot(hi, ones_bd) + (_dot(mid, ones_bd) + _dot(lo, ones_bd))


def _float_key(x):
    bits = pltpu.bitcast(x, I32)
    return bits ^ (lax.shift_right_arithmetic(bits, 31) & 0x7FFFFFFF)


def _kth_largest_key(count_ge, n_sel, shape):
    def body(it, ans_u):
        bit = lax.shift_left(jnp.int32(1), 31 - it)
        cand_u = ans_u | bit
        cnt = count_ge(cand_u ^ INT_MIN)
        return jnp.where(cnt >= n_sel, cand_u, ans_u)

    ans_u = lax.fori_loop(0, 32, body, jnp.zeros(shape, I32))
    return ans_u ^ INT_MIN


def _tie_cutoff(count_eq_below, need, nbits, shape):
    def body(it, cut):
        cand = cut | lax.shift_left(jnp.int32(1), nbits - 1 - it)
        return jnp.where(count_eq_below(cand) <= need, cand, cut)

    return lax.fori_loop(0, nbits, body, jnp.zeros(shape, I32))


def _ln(x, g, b):
    mu = jnp.mean(x, axis=-1, keepdims=True)
    xc = x - mu
    var = jnp.mean(xc * xc, axis=-1, keepdims=True)
    return xc * lax.rsqrt(var + LN_EPS) * g + b


def _ln_in_kernel(x_ref, g_ref, b_ref, xn_ref, xb_ref):
    y = _ln(x_ref[...], g_ref[...], b_ref[...])
    xn_ref[...] = y
    xb_ref[...] = y.astype(BF16)


def _ln_in(x, g, b, tm):
    rows = x.shape[0]
    row = pl.BlockSpec((tm, D_MODEL), lambda i: (i, 0))
    vec = pl.BlockSpec((1, D_MODEL), lambda i: (0, 0))
    return pl.pallas_call(
        _ln_in_kernel,
        grid=(rows // tm,),
        in_specs=[row, vec, vec],
        out_specs=[row, row],
        out_shape=[jax.ShapeDtypeStruct((rows, D_MODEL), F32), jax.ShapeDtypeStruct((rows, D_MODEL), BF16)],
        compiler_params=_params("parallel"),
    )(x, g.reshape(1, -1), b.reshape(1, -1))


def _ln_out_kernel(x_ref, pt_ref, g_ref, b_ref, o_ref):
    o_ref[...] = _ln(DN_ALPHA * x_ref[...] + pt_ref[...].T, g_ref[...], b_ref[...])


def _ln_out(x, pt, g, b, tm):
    rows = x.shape[0]
    row = pl.BlockSpec((tm, D_MODEL), lambda i: (i, 0))
    vec = pl.BlockSpec((1, D_MODEL), lambda i: (0, 0))
    return pl.pallas_call(
        _ln_out_kernel,
        grid=(rows // tm,),
        in_specs=[row, pl.BlockSpec((D_MODEL, tm), lambda i: (0, i)), vec, vec],
        out_specs=row,
        out_shape=jax.ShapeDtypeStruct((rows, D_MODEL), F32),
        compiler_params=_params("parallel"),
    )(x, pt, g.reshape(1, -1), b.reshape(1, -1))


def _mm_kernel(x_ref, w_ref, o_ref):
    o_ref[...] = _dot(x_ref[...], w_ref[...])


def _matmul(xb, w, tm, tn):
    m, k = xb.shape
    n = w.shape[1]
    return pl.pallas_call(
        _mm_kernel,
        grid=(m // tm, n // tn),
        in_specs=[pl.BlockSpec((tm, k), lambda i, j: (i, 0)), pl.BlockSpec((k, tn), lambda i, j: (0, j))],
        out_specs=pl.BlockSpec((tm, tn), lambda i, j: (i, j)),
        out_shape=jax.ShapeDtypeStruct((m, n), F32),
        compiler_params=_params("parallel", "arbitrary"),
    )(xb, w)


def _rwkv_pre_kernel(cur_ref, prev8_ref, init_ref, mu_ref, w0_ref, a0_ref, kk_ref, ka_ref, rk_ref,
                     wup_ref, aup_ref, gup_ref, ones_ref, *out_refs, n_prompt_tiles, tiles_per_batch, dec_seq):
    i = pl.program_id(0)
    cur = cur_ref[...]
    row = lax.broadcasted_iota(I32, cur.shape, 0)
    prev = jnp.where(row == 0, jnp.broadcast_to(prev8_ref[7:8, :], cur.shape), pltpu.roll(cur, 1, axis=0))
    batch_start = ((i % tiles_per_batch) == 0).astype(I32)
    first_prompt = jnp.where(row == 0, batch_start, 0)
    first_sample = jnp.where(row % dec_seq == 0, 1, 0)
    first = jnp.where(i < n_prompt_tiles, first_prompt, first_sample)
    prev = jnp.where(first > 0, init_ref[...], prev)

    xm = cur + (prev - cur) * mu_ref[...]
    r = xm[:, 0:RWKV_WIDTH]
    k = xm[:, RWKV_WIDTH:2 * RWKV_WIDTH]
    v = xm[:, 2 * RWKV_WIDTH:3 * RWKV_WIDTH]
    wa = xm[:, 3 * RWKV_WIDTH:3 * RWKV_WIDTH + LORA_WA]
    gl = xm[:, 3 * RWKV_WIDTH + LORA_WA:]
    ones_bd = ones_ref[...]

    nz = -(w0_ref[...] + _dot_hp(jnp.tanh(wa), wup_ref[...]))
    softplus = jnp.maximum(nz, 0.0) + jnp.log1p(jnp.exp(-jnp.abs(nz)))
    decay = jnp.exp(-jnp.exp(-softplus - 0.5))
    a = jax.nn.sigmoid(a0_ref[...] + _dot_hp(wa, aup_ref[...]))
    g = _dot_hp(jax.nn.sigmoid(gl), gup_ref[...])
    kn = k * kk_ref[...]
    kn = kn / jnp.maximum(jnp.sqrt(_segsum(kn * kn, ones_bd)), 1e-12)
    k_h = k * (1.0 + (a - 1.0) * ka_ref[...])
    vals = (r, decay, k_h, v, kn, kn * a, g, _segsum(r * k_h * rk_ref[...], ones_bd) * v)
    n_out = len(vals)

    @pl.when(i < n_prompt_tiles)
    def _():
        for o_ref, val in zip(out_refs[:n_out], vals):
            o_ref[...] = val

    @pl.when(i >= n_prompt_tiles)
    def _():
        for o_ref, val in zip(out_refs[n_out:], vals):
            o_ref[...] = val


def _rwkv_pre(feat, init, mu, w0, a0, k_k, k_a, r_k, wup, aup, gup, ones_bd, n_prompt_tiles, tiles_per_batch, dec_seq):
    tm = LANE
    n_tiles = feat.shape[0] // tm
    vec = lambda n: pl.BlockSpec((1, n), lambda i: (0, 0))
    full = lambda a: pl.BlockSpec(a.shape, lambda i: (0, 0))
    out_p = pl.BlockSpec((tm, RWKV_WIDTH), lambda i: (jnp.minimum(i, n_prompt_tiles - 1), 0))
    out_t = pl.BlockSpec((tm, RWKV_WIDTH), lambda i: (jnp.maximum(i - n_prompt_tiles, 0), 0))
    shape_p = jax.ShapeDtypeStruct((n_prompt_tiles * tm, RWKV_WIDTH), F32)
    shape_t = jax.ShapeDtypeStruct(((n_tiles - n_prompt_tiles) * tm, RWKV_WIDTH), F32)
    kern = functools.partial(_rwkv_pre_kernel, n_prompt_tiles=n_prompt_tiles, tiles_per_batch=tiles_per_batch, dec_seq=dec_seq)
    outs = pl.pallas_call(
        kern,
        grid=(n_tiles,),
        in_specs=[
            pl.BlockSpec((tm, RWKV_PAD), lambda i: (i, 0)),
            pl.BlockSpec((8, RWKV_PAD), lambda i: (jnp.maximum(i * (tm // 8) - 1, 0), 0)),
            pl.BlockSpec((tm, RWKV_PAD), lambda i: (jnp.where(i == n_prompt_tiles, 1, 0), 0)),
            vec(RWKV_PAD), vec(RWKV_WIDTH), vec(RWKV_WIDTH), vec(RWKV_WIDTH), vec(RWKV_WIDTH), vec(RWKV_WIDTH),
            full(wup), full(aup), full(gup), full(ones_bd),
        ],
        out_specs=[out_p] * 8 + [out_t] * 8,
        out_shape=[shape_p] * 8 + [shape_t] * 8,
        compiler_params=_params("arbitrary"),
    )(feat, feat, init, mu, w0, a0, k_k, k_a, r_k, wup, aup, gup, ones_bd)
    return outs[:8], outs[8:]


def _rwkv_scan_kernel(r_ref, w_ref, k_ref, v_ref, kn_ref, b_ref, s0_ref, y_ref, s_ref):
    @pl.when(pl.program_id(1) == 0)
    def _():
        s_ref[...] = s0_ref[...]

    def step(t, carry):
        kn_t = kn_ref[t]
        w_t = w_ref[t]
        b_t = b_ref[t]
        k_t = k_ref[t]
        r_t = r_ref[t]

        def value_row(vi, c):
            s_v = s_ref[vi]
            s_kn = jnp.sum(s_v * kn_t, axis=0, keepdims=True)
            s_new = s_v * w_t - s_kn * b_t + v_ref[t, pl.ds(vi, 1), :] * k_t
            s_ref[vi] = s_new
            y_ref[t, pl.ds(vi, 1), :] = jnp.sum(s_new * r_t, axis=0, keepdims=True)
            return c

        return lax.fori_loop(0, HEAD_DIM, value_row, carry, unroll=2)

    lax.fori_loop(0, r_ref.shape[0], step, 0)


def _rwkv_scan(r, w, k, v, kn, b, s0, tc):
    steps, _, pairs = r.shape
    seq = pl.BlockSpec((tc, HEAD_DIM, LANE), lambda p, c: (c, 0, p))
    state = pl.BlockSpec((HEAD_DIM, HEAD_DIM, LANE), lambda p, c: (0, 0, p))
    return pl.pallas_call(
        _rwkv_scan_kernel,
        grid=(pairs // LANE, steps // tc),
        in_specs=[seq] * 6 + [state],
        out_specs=[seq, state],
        out_shape=[jax.ShapeDtypeStruct(r.shape, F32), jax.ShapeDtypeStruct(s0.shape, F32)],
        compiler_params=_params("parallel", "arbitrary"),
    )(r, w, k, v, kn, b, s0)


def _prompt_attn_kernel(qq_ref, kvi_ref, wi_ref, bias_ref, y_ref,
                        vt_ref, kb_ref, kib_ref, qn_ref, key_ref, sel_ref, cut_ref, acc_ref, *, n_blocks, n_sel):
    i = pl.program_id(1)
    n_kb = i + 1
    gq = GROUP * LANE

    @pl.when(i == 0)
    def _():
        for j in range(n_blocks):
            vt_ref[j] = kvi_ref[j * LANE:(j + 1) * LANE, KV_WIDTH:2 * KV_WIDTH].T.astype(BF16)
        for n in range(KV_HEADS):
            kb_ref[n] = kvi_ref[:, n * HEAD_DIM:(n + 1) * HEAD_DIM].astype(BF16)
        kib_ref[...] = kvi_ref[:, 2 * KV_WIDTH:2 * KV_WIDTH + IDX_DIM].astype(BF16)

    kpos0 = lax.broadcasted_iota(I32, (LANE, LANE), 0)
    qpos = i * LANE + lax.broadcasted_iota(I32, (LANE, LANE), 1)

    qi_all = jnp.concatenate(
        [qq_ref[:, ATTN_WIDTH + h * IDX_DIM:ATTN_WIDTH + (h + 1) * IDX_DIM] for h in range(IDX_HEADS)], axis=0).astype(BF16)
    w_t = wi_ref[...].T
    w_flat = jnp.concatenate([w_t[h:h + 1, :] for h in range(IDX_HEADS)], axis=1)

    def score_block(j, c):
        r0 = pl.multiple_of(j * LANE, LANE)
        ki = kib_ref[pl.ds(r0, LANE), :]
        s = jnp.maximum(_dot_nt(ki, qi_all), 0.0) * w_flat
        acc = s[:, 0:LANE]
        for h in range(1, IDX_HEADS):
            acc = acc + s[:, h * LANE:(h + 1) * LANE]
        acc = jnp.where(acc == 0.0, 0.0, acc)
        acc = jnp.where(kpos0 + r0 <= qpos, acc, -jnp.inf)
        key_ref[pl.ds(r0, LANE), :] = _float_key(acc)
        return c

    lax.fori_loop(0, n_kb, score_block, 0)

    row1 = (1, LANE)

    def count(flag):
        def body(j, acc):
            r0 = pl.multiple_of(j * LANE, LANE)
            return acc + flag(key_ref[pl.ds(r0, LANE), :], r0)

        return jnp.sum(lax.fori_loop(0, n_kb, body, jnp.zeros((LANE, LANE), F32)), axis=0, keepdims=True)

    thr = _kth_largest_key(lambda t: count(lambda k, r0: jnp.where(k >= t, 1.0, 0.0)), n_sel, row1)
    need = n_sel - count(lambda k, r0: jnp.where(k > thr, 1.0, 0.0))
    n_tied = count(lambda k, r0: jnp.where(k == thr, 1.0, 0.0))
    nbits = (n_blocks * LANE).bit_length()
    cut_ref[...] = jnp.full(cut_ref.shape, 2 ** nbits, I32)

    @pl.when(jnp.max(n_tied - need) > 0.0)
    def _():
        cut = _tie_cutoff(
            lambda c: count(lambda k, r0: jnp.where(k == thr, jnp.where(kpos0 + r0 < c, 1.0, 0.0), 0.0)), need, nbits, row1)
        cut_ref[...] = jnp.broadcast_to(cut, cut_ref.shape)

    cut = cut_ref[0:1, :]

    def select_block(j, c):
        r0 = pl.multiple_of(j * LANE, LANE)
        k = key_ref[pl.ds(r0, LANE), :]
        kpos = kpos0 + r0
        chosen = jnp.where(k > thr, 1.0, jnp.where(k == thr, jnp.where(kpos < cut, 1.0, 0.0), 0.0))
        sel_ref[pl.ds(r0, LANE), :] = jnp.where(kpos <= qpos, chosen, 0.0)
        return c

    lax.fori_loop(0, n_kb, select_block, 0)

    for n in range(KV_HEADS):
        q_n = jnp.concatenate(
            [qq_ref[:, (GROUP * n + g) * HEAD_DIM:(GROUP * n + g + 1) * HEAD_DIM] for g in range(GROUP)], axis=0)
        qn_ref[n] = (q_n * HEAD_DIM ** -0.5).astype(BF16)
    acc_ref[...] = jnp.zeros(acc_ref.shape, F32)

    def key_block(j, carry):
        ms, ls = carry
        r0 = pl.multiple_of(j * LANE, LANE)
        mask = sel_ref[pl.ds(r0, LANE), :] > 0.5
        near = jnp.minimum(i - j, 2)
        new_ms, new_ls = [], []
        for n in range(KV_HEADS):
            s = _dot_nt(kb_ref[n, pl.ds(r0, LANE), :], qn_ref[n]) + bias_ref[n, near]
            s = jnp.concatenate([jnp.where(mask, s[:, g * LANE:(g + 1) * LANE], NEG_BIG) for g in range(GROUP)], axis=1)
            m_new = jnp.maximum(ms[n], jnp.max(s, axis=0, keepdims=True))
            alpha = jnp.exp(ms[n] - m_new)
            p = jnp.exp(s - m_new)
            new_ls.append(alpha * ls[n] + jnp.sum(p, axis=0, keepdims=True))
            new_ms.append(m_new)
            acc_ref[n] = alpha * acc_ref[n] + _dot(vt_ref[j, n * HEAD_DIM:(n + 1) * HEAD_DIM, :], p.astype(BF16))
        return tuple(new_ms), tuple(new_ls)

    init = (tuple(jnp.full((1, gq), NEG_BIG, F32) for _ in range(KV_HEADS)),
            tuple(jnp.zeros((1, gq), F32) for _ in range(KV_HEADS)))
    _, ls = lax.fori_loop(0, n_kb, key_block, init)
    outs = []
    for n in range(KV_HEADS):
        o = acc_ref[n] / ls[n]
        outs += [o[:, g * LANE:(g + 1) * LANE].T for g in range(GROUP)]
    y_ref[...] = jnp.concatenate(outs, axis=1)


def _prompt_attn(qq, kvi, bias_tiles, n_batch, n_blocks, n_sel):
    t_pad = n_blocks * LANE
    kern = functools.partial(_prompt_attn_kernel, n_blocks=n_blocks, n_sel=n_sel)
    return pl.pallas_call(
        kern,
        grid=(n_batch, n_blocks),
        in_specs=[
            pl.BlockSpec((LANE, 2 * ATTN_WIDTH), lambda b, i: (b * n_blocks + i, 0)),
            pl.BlockSpec((t_pad, kvi.shape[1]), lambda b, i: (b, 0)),
            pl.BlockSpec((LANE, LANE), lambda b, i: (b * n_blocks + i, (2 * KV_WIDTH + IDX_DIM) // LANE)),
            pl.BlockSpec(bias_tiles.shape, lambda b, i: (0, 0, 0, 0)),
        ],
        out_specs=pl.BlockSpec((LANE, ATTN_WIDTH), lambda b, i: (b * n_blocks + i, 0)),
        out_shape=jax.ShapeDtypeStruct((n_batch * t_pad, ATTN_WIDTH), F32),
        scratch_shapes=[
            pltpu.VMEM((n_blocks, KV_WIDTH, LANE), BF16),
            pltpu.VMEM((KV_HEADS, t_pad, HEAD_DIM), BF16),
            pltpu.VMEM((t_pad, IDX_DIM), BF16),
            pltpu.VMEM((KV_HEADS, GROUP * LANE, HEAD_DIM), BF16),
            pltpu.VMEM((t_pad, LANE), I32),
            pltpu.VMEM((t_pad, LANE), F32),
            pltpu.VMEM((8, LANE), I32),
            pltpu.VMEM((KV_HEADS, HEAD_DIM, GROUP * LANE), F32),
        ],
        compiler_params=_params("parallel", "arbitrary"),
    )(qq, kvi, kvi, bias_tiles)


PAGES_PER_STEP = 8
Q_PAD = 8


def _sample_score_kernel(pt_ref, qi_ref, wb_ref, *refs):
    page_refs, out_ref = refs[:PAGES_PER_STEP], refs[PAGES_PER_STEP]
    qi = qi_ref[...].astype(BF16)
    wb = wb_ref[...]
    for u in range(PAGES_PER_STEP):
        s = jnp.maximum(_dot_nt(qi, page_refs[u][...].astype(BF16)), 0.0) * wb
        acc = s[0:Q_PAD]
        for h in range(1, IDX_HEADS):
            acc = acc + s[h * Q_PAD:(h + 1) * Q_PAD]
        out_ref[u] = acc


def _sample_scores(page_table, qi8, wb, cache_kidx):
    n_batch, n_pages = page_table.shape
    page_spec = lambda u: pl.BlockSpec((None, PAGE, IDX_DIM), lambda b, s, pt: (pt[b, s * PAGES_PER_STEP + u], 0, 0))
    per_batch = pl.BlockSpec((None, IDX_HEADS * Q_PAD, IDX_DIM), lambda b, s, pt: (b, 0, 0))
    return pl.pallas_call(
        _sample_score_kernel,
        grid_spec=pltpu.PrefetchScalarGridSpec(
            num_scalar_prefetch=1,
            grid=(n_batch, n_pages // PAGES_PER_STEP),
            in_specs=[per_batch, per_batch] + [page_spec(u) for u in range(PAGES_PER_STEP)],
            out_specs=pl.BlockSpec((None, PAGES_PER_STEP, Q_PAD, PAGE), lambda b, s, pt: (b, s, 0, 0)),
        ),
        out_shape=jax.ShapeDtypeStruct((n_batch, n_pages, Q_PAD, PAGE), F32),
        compiler_params=_params("parallel", "arbitrary"),
    )(page_table, qi8, wb, *([cache_kidx] * PAGES_PER_STEP))


def _sample_select_kernel(sc_ref, qi_ref, wb_ref, kin_ref, sel_ref, key_ref, *, n_pages, dec_seq, n_sel):
    qrow = lax.broadcasted_iota(I32, (Q_PAD, PAGE), 0)
    lane = lax.broadcasted_iota(I32, (Q_PAD, PAGE), 1)
    s = jnp.maximum(_dot_nt(qi_ref[...].astype(BF16), kin_ref[...].astype(BF16)), 0.0) * wb_ref[...]
    acc = s[0:Q_PAD]
    for h in range(1, IDX_HEADS):
        acc = acc + s[h * Q_PAD:(h + 1) * Q_PAD]
    new_valid = jnp.where(lane < dec_seq, jnp.where(lane <= qrow, 1, 0), 0) > 0
    past = sc_ref[...]
    key_ref[0:n_pages] = _float_key(jnp.where(past == 0.0, 0.0, past))
    key_ref[n_pages] = _float_key(jnp.where(new_valid, jnp.where(acc == 0.0, 0.0, acc), -jnp.inf))

    def lane_count(x):
        return jnp.sum(jnp.sum(x, axis=0), axis=1, keepdims=True)

    col1 = (Q_PAD, 1)
    thr = _kth_largest_key(lambda t: lane_count(jnp.where(key_ref[...] >= t, 1.0, 0.0)), n_sel, col1)
    keys = key_ref[...]
    need = n_sel - lane_count(jnp.where(keys > thr, 1.0, 0.0))
    shape3 = (n_pages + 1, Q_PAD, PAGE)
    kidx = lax.broadcasted_iota(I32, shape3, 0) * PAGE + lax.broadcasted_iota(I32, shape3, 2)
    cut = _tie_cutoff(
        lambda c: lane_count(jnp.where(key_ref[...] == thr, jnp.where(kidx < c, 1.0, 0.0), 0.0)),
        need, ((n_pages + 1) * PAGE).bit_length(), col1)
    chosen = jnp.where(keys > thr, 1.0, jnp.where(keys == thr, jnp.where(kidx < cut, 1.0, 0.0), 0.0))
    sel_ref[0:n_pages] = chosen[0:n_pages]
    sel_ref[n_pages] = jnp.where(new_valid, chosen[n_pages], 0.0)


def _sample_select(sc, qi8, wb, ki_new, dec_seq, n_sel):
    n_batch, n_pages = sc.shape[:2]
    kern = functools.partial(_sample_select_kernel, n_pages=n_pages, dec_seq=dec_seq, n_sel=n_sel)
    per_batch = lambda a: pl.BlockSpec((None,) + a.shape[1:], lambda b: (b,) + (0,) * (a.ndim - 1))
    return pl.pallas_call(
        kern,
        grid=(n_batch,),
        in_specs=[per_batch(sc), per_batch(qi8), per_batch(wb), per_batch(ki_new)],
        out_specs=pl.BlockSpec((None, n_pages + 1, Q_PAD, PAGE), lambda b: (b, 0, 0, 0)),
        out_shape=jax.ShapeDtypeStruct((n_batch, n_pages + 1, Q_PAD, PAGE), F32),
        scratch_shapes=[pltpu.VMEM((n_pages + 1, Q_PAD, PAGE), I32)],
        compiler_params=_params("parallel"),
    )(sc, qi8, wb, ki_new)


def _sample_attn_kernel(pt_ref, q_ref, sel_ref, selnew_ref, knew_ref, vnew_ref, bias_ref, *refs, n_steps):
    k_refs = refs[:PAGES_PER_STEP]
    v_refs = refs[PAGES_PER_STEP:2 * PAGES_PER_STEP]
    o_ref, m_ref, l_ref, acc_ref = refs[2 * PAGES_PER_STEP:]
    s_id = pl.program_id(1)
    rows = KV_HEADS * GROUP * Q_PAD
    per_kv = GROUP * Q_PAD
    q = (q_ref[...] * HEAD_DIM ** -0.5).astype(BF16)

    @pl.when(s_id == 0)
    def _():
        m_ref[...] = jnp.full(m_ref.shape, NEG_BIG, F32)
        l_ref[...] = jnp.zeros(l_ref.shape, F32)
        acc_ref[...] = jnp.zeros(acc_ref.shape, F32)

    def attend(k_pages, v_pages, sel_pages, bias_pages):
        s_blocks, sel_blocks = [], []
        for kp, sp, bp in zip(k_pages, sel_pages, bias_pages):
            kb = kp.astype(BF16)
            s = jnp.concatenate(
                [_dot(q[n * per_kv:(n + 1) * per_kv], kb[n * HEAD_DIM:(n + 1) * HEAD_DIM]) for n in range(KV_HEADS)],
                axis=0) + bp
            s_blocks.append(s)
            sel_blocks.append(jnp.concatenate([sp] * (KV_HEADS * GROUP), axis=0) > 0.5)
        s = jnp.concatenate(s_blocks, axis=1)
        sel = jnp.concatenate(sel_blocks, axis=1)
        s = jnp.where(sel, s, NEG_BIG)
        m_old = m_ref[...]
        m_new = jnp.maximum(m_old, jnp.max(s, axis=1, keepdims=True))
        alpha = jnp.exp(m_old - m_new)
        p = jnp.where(sel, jnp.exp(s - m_new), 0.0)
        l_ref[...] = alpha * l_ref[...] + jnp.sum(p, axis=1, keepdims=True)
        m_ref[...] = m_new
        pb = p.astype(BF16)
        pv = None
        for u, vp in enumerate(v_pages):
            vb = vp.astype(BF16)
            pu = pb[:, u * PAGE:(u + 1) * PAGE]
            part = jnp.concatenate(
                [_dot_nt(pu[n * per_kv:(n + 1) * per_kv], vb[n * HEAD_DIM:(n + 1) * HEAD_DIM]) for n in range(KV_HEADS)], axis=0)
            pv = part if pv is None else pv + part
        acc_ref[...] = alpha * acc_ref[...] + pv

    @pl.when(s_id < n_steps)
    def _():
        far, near = bias_ref[0], bias_ref[1]
        biases = [far] * PAGES_PER_STEP
        last = s_id == n_steps - 1
        biases[-1] = jnp.where(last, near, far)
        attend([r[...] for r in k_refs], [r[...] for r in v_refs], [sel_ref[u] for u in range(PAGES_PER_STEP)], biases)

    @pl.when(s_id == n_steps)
    def _():
        attend([knew_ref[...]], [vnew_ref[...]], [selnew_ref[...]], [bias_ref[2]])
        o_ref[...] = acc_ref[...] / l_ref[...]


def _sample_attn(page_table, q8, sel, k_new, v_new, bias_tiles, cache_k, cache_v):
    n_batch, n_pages = page_table.shape
    n_steps = n_pages // PAGES_PER_STEP
    rows = KV_HEADS * GROUP * Q_PAD
    kern = functools.partial(_sample_attn_kernel, n_steps=n_steps)

    def page_spec(u):
        return pl.BlockSpec((None, KV_WIDTH, PAGE),
                            lambda b, s, pt: (pt[b, jnp.minimum(s, n_steps - 1) * PAGES_PER_STEP + u], 0, 0))

    per_batch = lambda a: pl.BlockSpec((None,) + a.shape[1:], lambda b, s, pt: (b,) + (0,) * (a.ndim - 1))
    return pl.pallas_call(
        kern,
        grid_spec=pltpu.PrefetchScalarGridSpec(
            num_scalar_prefetch=1,
            grid=(n_batch, n_steps + 1),
            in_specs=[
                per_batch(q8),
                pl.BlockSpec((None, PAGES_PER_STEP, Q_PAD, PAGE), lambda b, s, pt: (b, jnp.minimum(s, n_steps - 1), 0, 0)),
                pl.BlockSpec((None, None, Q_PAD, PAGE), lambda b, s, pt: (b, n_pages, 0, 0)),
                per_batch(k_new), per_batch(v_new),
                pl.BlockSpec(bias_tiles.shape, lambda b, s, pt: (0, 0, 0)),
            ] + [page_spec(u) for u in range(PAGES_PER_STEP)] * 2,
            out_specs=pl.BlockSpec((None, rows, HEAD_DIM), lambda b, s, pt: (b, 0, 0)),
            scratch_shapes=[pltpu.VMEM((rows, 1), F32), pltpu.VMEM((rows, 1), F32), pltpu.VMEM((rows, HEAD_DIM), F32)],
        ),
        out_shape=jax.ShapeDtypeStruct((n_batch, rows, HEAD_DIM), F32),
        compiler_params=_params("parallel", "arbitrary"),
    )(page_table, q8, sel, sel, k_new, v_new, bias_tiles, *([cache_k] * PAGES_PER_STEP), *([cache_v] * PAGES_PER_STEP))


def _mix_kernel(ysp_ref, yst_ref, bonusp_ref, bonust_ref, gp_ref, gt_ref, yap_ref, yat_ref,
                xn_ref, gng_ref, gnb_ref, ones_ref, wor_ref, woa_ref, lg_ref, lb_ref,
                x1_ref, x1b_ref, x1t_ref, *, n_prompt_tiles):
    is_prompt = pl.program_id(0) < n_prompt_tiles
    pick = lambda p_ref, t_ref: jnp.where(is_prompt, p_ref[...], t_ref[...])
    ones_bd = ones_ref[...]
    ys = pick(ysp_ref, yst_ref)
    inv = 1.0 / HEAD_DIM
    yc = ys - _segsum(ys, ones_bd) * inv
    var = _segsum(yc * yc, ones_bd) * inv
    yr = (yc * lax.rsqrt(var + GN_EPS) * gng_ref[...] + gnb_ref[...] + pick(bonusp_ref, bonust_ref)) * pick(gp_ref, gt_ref)
    mix = _dot(yr.astype(BF16), wor_ref[...]) + _dot(pick(yap_ref, yat_ref).astype(BF16), woa_ref[...])
    x1 = _ln(DN_ALPHA * xn_ref[...] + mix, lg_ref[...], lb_ref[...])
    x1_ref[...] = x1
    x1b_ref[...] = x1.astype(BF16)
    x1t_ref[...] = x1.T.astype(BF16)


def _mix(ys, bonus, g, ya, xn, gn_g, gn_b, ones_bd, wo_r, wo_a, ln_g, ln_b, tm):
    rows = xn.shape[0]
    n_prompt_tiles = ys[0].shape[0] // tm
    half_p = pl.BlockSpec((tm, RWKV_WIDTH), lambda i: (jnp.minimum(i, n_prompt_tiles - 1), 0))
    half_t = pl.BlockSpec((tm, RWKV_WIDTH), lambda i: (jnp.maximum(i - n_prompt_tiles, 0), 0))
    row = pl.BlockSpec((tm, D_MODEL), lambda i: (i, 0))
    vec = lambda n: pl.BlockSpec((1, n), lambda i: (0, 0))
    full = lambda a: pl.BlockSpec(a.shape, lambda i: (0, 0))
    return pl.pallas_call(
        functools.partial(_mix_kernel, n_prompt_tiles=n_prompt_tiles),
        grid=(rows // tm,),
        in_specs=[half_p, half_t] * 4 + [row, vec(RWKV_WIDTH), vec(RWKV_WIDTH), full(ones_bd), full(wo_r), full(wo_a),
                                         vec(D_MODEL), vec(D_MODEL)],
        out_specs=[row, row, pl.BlockSpec((D_MODEL, tm), lambda i: (0, i))],
        out_shape=[jax.ShapeDtypeStruct((rows, D_MODEL), F32), jax.ShapeDtypeStruct((rows, D_MODEL), BF16),
                   jax.ShapeDtypeStruct((D_MODEL, rows), BF16)],
        compiler_params=_params("parallel"),
    )(*ys, *bonus, *g, *ya, xn, gn_g, gn_b, ones_bd, wo_r, wo_a, ln_g, ln_b)


CAND_PAIRS = [(c, d) for c in range(PEER_TOPK) for d in range(PEER_TOPK) if (c + 1) * (d + 1) <= PEER_TOPK]


def _top_rows(x, n):
    rows = []
    for _ in range(n):
        m = jnp.max(x, axis=0, keepdims=True)
        rows.append(m)
        x = jnp.where(x == m, -jnp.inf, x)
    return rows


def _peer_route_kernel(x_ref, wq_ref, sub_ref, s1_ref, s2_ref, e2_ref, thr_ref, m1_ref, zinv_ref):
    q = _dot(x_ref[...], wq_ref[...])
    for h in range(PEER_HEADS):
        base = h * 2 * PEER_HALF
        s1 = _dot_nt_hp(sub_ref[h, 0], q[:, base:base + PEER_HALF])
        s2 = _dot_nt_hp(sub_ref[h, 1], q[:, base + PEER_HALF:base + 2 * PEER_HALF])
        top1 = _top_rows(s1, PEER_TOPK)
        top2 = _top_rows(s2, PEER_TOPK)
        cand = jnp.concatenate([top1[c] + top2[d] for c, d in CAND_PAIRS]
                               + [jnp.full_like(top1[0], -jnp.inf)] * (-len(CAND_PAIRS) % 8), axis=0)
        best = _top_rows(cand, PEER_TOPK)
        thr = best[-1]
        m = top1[0] + top2[0]
        z = jnp.sum(jnp.where(cand >= thr, jnp.exp(cand - m), 0.0), axis=0, keepdims=True)
        s1_ref[h] = s1
        s2_ref[h] = s2
        e2_ref[h] = jnp.exp(s2 - top2[0])
        thr_ref[h:h + 1, :] = thr
        m1_ref[h:h + 1, :] = top1[0]
        zinv_ref[h:h + 1, :] = 1.0 / z


def _peer_route(x1b, wq, subkeys):
    rows = x1b.shape[0]
    tm = LANE
    stat = pl.BlockSpec((PEER_HEADS, tm), lambda i: (0, i))
    big = pl.BlockSpec((PEER_HEADS, PEER_NKEYS, tm), lambda i: (0, 0, i))
    stat_shape = jax.ShapeDtypeStruct((PEER_HEADS, rows), F32)
    big_shape = jax.ShapeDtypeStruct((PEER_HEADS, PEER_NKEYS, rows), F32)
    return pl.pallas_call(
        _peer_route_kernel,
        grid=(rows // tm,),
        in_specs=[pl.BlockSpec((tm, D_MODEL), lambda i: (i, 0)),
                  pl.BlockSpec(wq.shape, lambda i: (0, 0)),
                  pl.BlockSpec(subkeys.shape, lambda i: (0, 0, 0, 0))],
        out_specs=[big, big, big, stat, stat, stat],
        out_shape=[big_shape, big_shape, big_shape, stat_shape, stat_shape, stat_shape],
        compiler_params=_params("parallel"),
    )(x1b, wq, subkeys)


EXPERT_ROWS = 8
MXU_DEPTH = 256


def _peer_dense_kernel(xt_ref, u_ref, vt_ref, s1_ref, s2_ref, e2_ref, thr_ref, m1_ref, zinv_ref, o_ref, *scratch):
    n_sub = EXPERT_ROWS * PEER_NKEYS // MXU_DEPTH
    ht_refs, g_refs, a_refs = scratch[:n_sub], scratch[n_sub:2 * n_sub], scratch[2 * n_sub:]
    j = pl.program_id(1)
    tm = xt_ref.shape[1]

    @pl.when(j == 0)
    def _():
        o_ref[...] = jnp.zeros(o_ref.shape, F32)

    per = MXU_DEPTH // PEER_NKEYS
    half = PEER_NKEYS // 2

    def gate_rows(p):
        out = []
        for e in range(p * per, (p + 1) * per):
            s1_i = [s1_ref[h, e:e + 1, :] for h in range(PEER_HEADS)]
            out.append((s1_i, [jnp.exp(s1_i[h] - m1_ref[h:h + 1, :]) * zinv_ref[h:h + 1, :] for h in range(PEER_HEADS)]))
        return out

    def gate_block(p, rows_p, c, jh):
        cols = slice(c, c + LANE)
        jr = slice(jh * half, (jh + 1) * half)
        acc = [None] * per
        for h in range(PEER_HEADS):
            s2, e2, thr = s2_ref[h, jr, cols], e2_ref[h, jr, cols], thr_ref[h:h + 1, cols]
            for e in range(per):
                s1_i, f_i = rows_p[e]
                g = jnp.where(s1_i[h][:, cols] + s2 >= thr, e2 * f_i[h][:, cols], 0.0)
                acc[e] = g if acc[e] is None else acc[e] + g
        for e in range(per):
            g_refs[p][e * PEER_NKEYS + jh * half:e * PEER_NKEYS + (jh + 1) * half, cols] = acc[e]

    def gate_blocks(p):
        rows_p = gate_rows(p)
        return [functools.partial(gate_block, p, rows_p, c, jh) for c in range(0, tm, LANE) for jh in range(2)]

    def up_piece(p, k):
        r = p * MXU_DEPTH + k * PEER_NKEYS
        ht_refs[p][k * PEER_NKEYS:(k + 1) * PEER_NKEYS, :] = _dot(u_ref[r:r + PEER_NKEYS, :], xt_ref[...])

    def act(p):
        he = ht_refs[p][...]
        a_refs[p][...] = (0.5 * he * (1.0 + lax.erf(he * (2.0 ** -0.5))) * g_refs[p][...]).astype(BF16)

    down_rows = 512

    def down_piece(p, m):
        r = p * MXU_DEPTH
        rows = slice(m * down_rows, (m + 1) * down_rows)
        o_ref[rows, :] += _dot(vt_ref[rows, r:r + MXU_DEPTH], a_refs[p][...])

    def run(pieces):
        for piece in pieces:
            piece()

    ups = lambda p: [functools.partial(up_piece, p, k) for k in range(per)]
    downs = lambda p: [functools.partial(down_piece, p, m) for m in range(D_MODEL // down_rows)]
    run(gate_blocks(0) + ups(0))
    for p in range(n_sub):
        if p + 1 < n_sub:
            run(gate_blocks(p + 1) + ups(p + 1))
        act(p)
        run(downs(p))


def _peer_dense(x1t, u, vt, s1, s2, e2, thr, m1, zinv, tm):
    rows = x1t.shape[1]
    eb = EXPERT_ROWS * PEER_NKEYS
    stat = pl.BlockSpec((PEER_HEADS, tm), lambda i, j: (0, i))
    big = pl.BlockSpec((PEER_HEADS, PEER_NKEYS, tm), lambda i, j: (0, 0, i))
    return pl.pallas_call(
        _peer_dense_kernel,
        grid=(rows // tm, PEER_EXPERTS // eb),
        in_specs=[pl.BlockSpec((D_MODEL, tm), lambda i, j: (0, i)),
                  pl.BlockSpec((eb, D_MODEL), lambda i, j: (j, 0)),
                  pl.BlockSpec((D_MODEL, eb), lambda i, j: (0, j)),
                  pl.BlockSpec((PEER_HEADS, EXPERT_ROWS, tm), lambda i, j: (0, j, i)),
                  big, big, stat, stat, stat],
        out_specs=pl.BlockSpec((D_MODEL, tm), lambda i, j: (0, i)),
        out_shape=jax.ShapeDtypeStruct((D_MODEL, rows), F32),
        scratch_shapes=([pltpu.VMEM((MXU_DEPTH, tm), F32)] * (2 * eb // MXU_DEPTH)
                        + [pltpu.VMEM((MXU_DEPTH, tm), BF16)] * (eb // MXU_DEPTH)),
        compiler_params=_params("parallel", "arbitrary"),
    )(x1t, u, vt, s1, s2, e2, thr, m1, zinv)


def _rel_buckets(dist):
    max_exact = REL_BUCKETS // 2
    d = np.maximum(dist, 0)
    ratio = np.log(np.maximum(d, 1).astype(np.float32) / np.float32(max_exact)) / np.float32(math.log(REL_MAX_DIST / max_exact))
    log_b = max_exact + (ratio * np.float32(REL_BUCKETS - max_exact)).astype(np.int32)
    return np.where(d < max_exact, d, np.minimum(log_b, REL_BUCKETS - 1)).astype(np.int32)


def _bias_lookup(rel_bias, dist):
    onehot = np.eye(REL_BUCKETS, dtype=np.float32)[_rel_buckets(dist)]
    return jnp.dot(jnp.asarray(onehot), rel_bias.astype(F32), precision=lax.Precision.HIGHEST)


def _prompt_bias_tiles(rel_bias):
    kk = np.arange(LANE)[:, None]
    qq = np.arange(LANE)[None, :]
    tiles = []
    for delta in range(3):
        b = _bias_lookup(rel_bias, delta * LANE + qq - kk)
        b = b.reshape(LANE, LANE, KV_HEADS, GROUP).transpose(2, 0, 3, 1).reshape(KV_HEADS, LANE, GROUP * LANE)
        tiles.append(b)
    return jnp.stack(tiles, axis=1).astype(F32)


def _sample_bias_tiles(rel_bias, past_len, dec_seq):
    q = np.arange(Q_PAD)[:, None]
    off = np.arange(PAGE)[None, :]
    qpos = past_len + np.minimum(q, dec_seq - 1)
    dists = [qpos - 0 * off - (past_len - 2 * PAGE), qpos - (past_len - PAGE + off), qpos - (past_len + np.minimum(off, dec_seq - 1))]
    tiles = []
    for d in dists:
        b = _bias_lookup(rel_bias, d + 0 * off)
        tiles.append(b.transpose(2, 0, 1).reshape(ATTN_HEADS * Q_PAD, PAGE))
    return jnp.stack(tiles).astype(F32)


def kernel(x_prompt, x_sample, cache_k, cache_v, cache_kidx, state_wkv, state_shift, page_table, meta_tokens, ln_in_g, ln_in_b, rel_bias, w_in, mu_shift, w0, w_up, a0, a_up, g_up, k_k, k_a, r_k, gn_g, gn_b, w_o, ln1_g, ln1_b, peer_wq, peer_subkeys, peer_u, peer_v, ln2_g, ln2_b):
    n_batch, seq, _ = x_prompt.shape
    dec_batch, dec_seq, _ = x_sample.shape
    n_pages = page_table.shape[1]
    past_len = n_pages * PAGE
    t_len = seq + N_META
    n_blocks = -(-t_len // LANE)
    t_pad = n_blocks * LANE
    rows_p = n_batch * t_pad
    rows_s = dec_batch * dec_seq
    assert rows_s == LANE and Q_PAD >= dec_seq and n_pages % PAGES_PER_STEP == 0
    assert past_len >= 2 * PAGE + REL_MAX_DIST
    rows = -(-(rows_p + rows_s) // ROW_ALIGN) * ROW_ALIGN
    layer = 0

    meta = jnp.broadcast_to(meta_tokens[None], (n_batch, N_META, D_MODEL))
    xp = jnp.pad(jnp.concatenate([meta, x_prompt], axis=1), ((0, 0), (0, t_pad - t_len), (0, 0)))
    x_all = jnp.concatenate([xp.reshape(rows_p, D_MODEL), x_sample.reshape(rows_s, D_MODEL),
                             jnp.zeros((rows - rows_p - rows_s, D_MODEL), F32)], axis=0)
    xn, xb = _ln_in(x_all, ln_in_g, ln_in_b, 256)

    w = w_in[layer]
    c0 = RWKV_COLS
    w_rwkv = jnp.pad(w[:, :c0], ((0, 0), (0, RWKV_PAD - RWKV_COLS))).astype(BF16)
    w_qq = jnp.concatenate([w[:, c0:c0 + ATTN_WIDTH], w[:, c0 + ATTN_WIDTH + 2 * KV_WIDTH:c0 + 2 * ATTN_WIDTH + 2 * KV_WIDTH]], axis=1).astype(BF16)
    c_ki = c0 + 2 * ATTN_WIDTH + 2 * KV_WIDTH
    w_kvi = jnp.concatenate([w[:, c0 + ATTN_WIDTH:c0 + ATTN_WIDTH + 2 * KV_WIDTH], w[:, c_ki:c_ki + IDX_DIM],
                             jnp.pad(w[:, c_ki + IDX_DIM:], ((0, 0), (0, LANE - IDX_HEADS)))], axis=1).astype(BF16)
    feat = _matmul(xb, w_rwkv, 640, RWKV_PAD // 3)
    qq = _matmul(xb, w_qq, 640, 1024)
    kvi = _matmul(xb, w_kvi, 640, w_kvi.shape[1])

    def prompt_rows(a):
        return a[:rows_p].reshape(n_batch, t_pad, -1)[:, :t_len]

    def sample_rows(a):
        return a[rows_p:rows_p + rows_s].reshape(dec_batch, dec_seq, -1)

    ones_bd = jnp.asarray(np.kron(np.eye(RWKV_HEADS), np.ones((HEAD_DIM, HEAD_DIM))), BF16)
    pad_cols = lambda a: jnp.pad(a, ((0, 0), (0, RWKV_PAD - RWKV_COLS)))
    init = jnp.zeros((dec_batch, dec_seq, RWKV_PAD), F32).at[:, 0].set(pad_cols(state_shift[layer]))
    init = jnp.concatenate([jnp.zeros((LANE, RWKV_PAD), F32), init.reshape(rows_s, RWKV_PAD)], axis=0)
    wup = jnp.pad(w_up[layer], ((0, ICLR_LORA), (0, 0)))
    aup = jnp.pad(a_up[layer], ((DECAY_LORA, 0), (0, 0)))
    gup = jnp.pad(g_up[layer], ((0, GATE_PAD - GATE_LORA), (0, 0)))
    vec = lambda a: a.reshape(1, -1)
    pre = _rwkv_pre(feat, init, vec(pad_cols(mu_shift[layer][None])), vec(w0[layer]), vec(a0[layer]), vec(k_k[layer]),
                    vec(k_a[layer]), vec(r_k[layer]), wup, aup, gup, ones_bd, rows_p // LANE, n_blocks, dec_seq)
    pre_p, pre_t = pre

    def to_scan(a, nb, steps):
        return a.reshape(nb, steps, RWKV_HEADS, HEAD_DIM).transpose(1, 3, 0, 2).reshape(steps, HEAD_DIM, nb * RWKV_HEADS)

    take_p = lambda a: a.reshape(n_batch, t_pad, -1)[:, :t_len]
    take_s = lambda a: a[:rows_s].reshape(dec_batch, dec_seq, -1)

    def from_scan(y, nb, steps):
        return y.reshape(steps, HEAD_DIM, nb, RWKV_HEADS).transpose(2, 0, 3, 1).reshape(nb, steps, RWKV_WIDTH)

    def state_in(s):
        nb = s.shape[0]
        return s.transpose(2, 3, 0, 1).reshape(HEAD_DIM, HEAD_DIM, nb * RWKV_HEADS)

    def state_out(s, nb):
        return s.reshape(HEAD_DIM, HEAD_DIM, nb, RWKV_HEADS).transpose(2, 3, 0, 1)

    tc = next(c for c in (48, 43, 32, 16, 8, 4, 2, 1) if t_len % c == 0)
    y_p, wkv_p = _rwkv_scan(*[to_scan(take_p(a), n_batch, t_len) for a in pre_p[:6]],
                            jnp.zeros((HEAD_DIM, HEAD_DIM, n_batch * RWKV_HEADS), F32), tc)
    y_s, wkv_s = _rwkv_scan(*[to_scan(take_s(a), dec_batch, dec_seq) for a in pre_t[:6]],
                            state_in(state_wkv[layer]), dec_seq)
    rows_t = rows - rows_p
    pad_tail = lambda a: jnp.pad(a, ((0, rows_t - rows_s), (0, 0)))
    ys = (jnp.pad(from_scan(y_p, n_batch, t_len), ((0, 0), (0, t_pad - t_len), (0, 0))).reshape(rows_p, RWKV_WIDTH),
          pad_tail(from_scan(y_s, dec_batch, dec_seq).reshape(rows_s, RWKV_WIDTH)))

    ya_p = _prompt_attn(qq, kvi, _prompt_bias_tiles(rel_bias), n_batch, n_blocks, min(IDX_TOPK, t_len // 4))

    qq_s, kvi_s = sample_rows(qq), sample_rows(kvi)

    def pad_q(a):
        a = jnp.pad(a.transpose(0, 2, 1, 3), ((0, 0), (0, 0), (0, Q_PAD - dec_seq), (0, 0)))
        return a.reshape(dec_batch, -1, a.shape[-1])

    qi8 = pad_q(qq_s[..., ATTN_WIDTH:].reshape(dec_batch, dec_seq, IDX_HEADS, IDX_DIM))
    wi_s = kvi_s[..., 2 * KV_WIDTH + IDX_DIM:2 * KV_WIDTH + IDX_DIM + IDX_HEADS]
    wb = jnp.broadcast_to(pad_q(wi_s[..., None]), (dec_batch, IDX_HEADS * Q_PAD, IDX_DIM))
    q8 = pad_q(qq_s[..., :ATTN_WIDTH].reshape(dec_batch, dec_seq, ATTN_HEADS, HEAD_DIM))
    pad_keys = lambda a: jnp.pad(a, ((0, 0), (0, PAGE - dec_seq), (0, 0)))
    ki_new = pad_keys(kvi_s[..., 2 * KV_WIDTH:2 * KV_WIDTH + IDX_DIM])
    k_new = pad_keys(kvi_s[..., :KV_WIDTH])
    v_new = pad_keys(kvi_s[..., KV_WIDTH:2 * KV_WIDTH])
    n_pool = cache_k.shape[1]
    pages_t = lambda c: c[layer].transpose(0, 2, 3, 1).reshape(n_pool, KV_WIDTH, PAGE)
    sc = _sample_scores(page_table, qi8, wb, cache_kidx[layer])
    sel = _sample_select(sc, qi8, wb, ki_new, dec_seq, min(IDX_TOPK, (past_len + dec_seq) // 4))
    o_s = _sample_attn(page_table, q8, sel, k_new.transpose(0, 2, 1), v_new.transpose(0, 2, 1),
                       _sample_bias_tiles(rel_bias, past_len, dec_seq), pages_t(cache_k), pages_t(cache_v))
    ya_s = o_s.reshape(dec_batch, ATTN_HEADS, Q_PAD, HEAD_DIM)[:, :, :dec_seq].transpose(0, 2, 1, 3).reshape(rows_s, ATTN_WIDTH)

    wo = w_o[layer].astype(BF16)
    x1, x1b, x1t = _mix(ys, (pre_p[7], pre_t[7]), (pre_p[6], pre_t[6]), (ya_p, pad_tail(ya_s)), xn,
                        vec(gn_g[layer]), vec(gn_b[layer]), ones_bd,
                        wo[:RWKV_WIDTH], wo[RWKV_WIDTH:], vec(ln1_g[layer]), vec(ln1_b[layer]), 256)
    routing = _peer_route(x1b, peer_wq[layer].astype(BF16), peer_subkeys[layer])
    peer_t = _peer_dense(x1t, peer_u[layer].astype(BF16), peer_v[layer].T.astype(BF16), *routing, 512)
    y = _ln_out(x1, peer_t, ln2_g[layer], ln2_b[layer], 256)

    last_p = feat[jnp.arange(n_batch) * t_pad + (t_len - 1)]
    last_s = feat[rows_p + jnp.arange(dec_batch) * dec_seq + (dec_seq - 1)]
    kvi_p = prompt_rows(kvi)
    kv4 = lambda a, nb, steps: a.reshape(nb, steps, KV_HEADS, HEAD_DIM)[None]
    return (
        prompt_rows(y)[:, N_META:], sample_rows(y),
        kv4(kvi_p[..., :KV_WIDTH], n_batch, t_len), kv4(kvi_p[..., KV_WIDTH:2 * KV_WIDTH], n_batch, t_len),
        kvi_p[..., 2 * KV_WIDTH:2 * KV_WIDTH + IDX_DIM][None],
        state_out(wkv_p, n_batch)[None], last_p[:, :RWKV_COLS][None],
        kv4(kvi_s[..., :KV_WIDTH], dec_batch, dec_seq), kv4(kvi_s[..., KV_WIDTH:2 * KV_WIDTH], dec_batch, dec_seq),
        kvi_s[..., 2 * KV_WIDTH:2 * KV_WIDTH + IDX_DIM][None],
        state_out(wkv_s, dec_batch)[None], last_s[:, :RWKV_COLS][None],
    )
```

```python
import functools
import math

import numpy as np
import jax
import jax.numpy as jnp
from jax import lax
from jax.experimental import pallas as pl
from jax.experimental.pallas import tpu as pltpu

F32, BF16, I32 = jnp.float32, jnp.bfloat16, jnp.int32

D_MODEL = 2048
N_META = 16
HEAD_DIM = 64
RWKV_WIDTH = 1024
ATTN_WIDTH = 1024
RWKV_HEADS = 16
ATTN_HEADS = 16
KV_HEADS = 4
GROUP = 4
KV_WIDTH = 256
DECAY_LORA = 64
ICLR_LORA = 64
GATE_LORA = 160
RWKV_COLS = 3 * RWKV_WIDTH + DECAY_LORA + ICLR_LORA + GATE_LORA
RWKV_PAD = 3456
LORA_WA = DECAY_LORA + ICLR_LORA
GATE_PAD = RWKV_PAD - 3 * RWKV_WIDTH - LORA_WA
GN_EPS = 64e-5
IDX_HEADS = 8
IDX_DIM = 128
IDX_TOPK = 256
REL_BUCKETS = 32
REL_MAX_DIST = 128
PEER_HEADS = 8
PEER_NKEYS = 128
PEER_HALF = 128
PEER_TOPK = 16
PEER_EXPERTS = PEER_NKEYS * PEER_NKEYS
DN_ALPHA = 2.0 ** 0.25
LN_EPS = 1e-5
PAGE = 128
LANE = 128
ROW_ALIGN = 2560
VMEM_LIMIT = 56 * 1024 * 1024
INT_MIN = -2 ** 31
NEG_BIG = -1e30
NEG_INF_KEY = int(np.int32(np.uint32(0xFF800000) ^ np.uint32(0x7FFFFFFF)))


def _params(*sem):
    return pltpu.CompilerParams(dimension_semantics=sem, vmem_limit_bytes=VMEM_LIMIT)


def _dot(a, b):
    return jnp.dot(a, b, preferred_element_type=F32)


def _dot_nt(a, b):
    return lax.dot_general(a, b, (((1,), (1,)), ((), ())), preferred_element_type=F32)


def _split2(x):
    hi = x.astype(BF16)
    lo = (x - hi.astype(F32)).astype(BF16)
    return hi, lo


def _split3(x):
    hi = x.astype(BF16)
    r1 = x - hi.astype(F32)
    mid = r1.astype(BF16)
    lo = (r1 - mid.astype(F32)).astype(BF16)
    return hi, mid, lo


def _dot_hp(a, b):
    ah, al = _split2(a)
    bh, bl = _split2(b)
    return _dot(ah, bh) + (_dot(ah, bl) + _dot(al, bh))


def _dot_nt_hp(a, b):
    ah, al = _split2(a)
    bh, bl = _split2(b)
    return _dot_nt(ah, bh) + (_dot_nt(ah, bl) + _dot_nt(al, bh))


def _segsum(x, ones_bd):
    hi, mid, lo = _split3(x)
    return _dot(hi, ones_bd) + (_dot(mid, ones_bd) + _dot(lo, ones_bd))


def _float_key(x):
    bits = pltpu.bitcast(x, I32)
    return bits ^ (lax.shift_right_arithmetic(bits, 31) & 0x7FFFFFFF)


def _kth_largest_key(count_ge, n_sel, shape):
    def body(it, ans_u):
        bit = lax.shift_left(jnp.int32(1), 31 - it)
        cand_u = ans_u | bit
        cnt = count_ge(cand_u ^ INT_MIN)
        return jnp.where(cnt >= n_sel, cand_u, ans_u)

    ans_u = lax.fori_loop(0, 32, body, jnp.zeros(shape, I32))
    return ans_u ^ INT_MIN


def _tie_cutoff(count_eq_below, need, nbits, shape):
    def body(it, cut):
        cand = cut | lax.shift_left(jnp.int32(1), nbits - 1 - it)
        return jnp.where(count_eq_below(cand) <= need, cand, cut)

    return lax.fori_loop(0, nbits, body, jnp.zeros(shape, I32))


def _ln(x, g, b):
    mu = jnp.mean(x, axis=-1, keepdims=True)
    xc = x - mu
    var = jnp.mean(xc * xc, axis=-1, keepdims=True)
    return xc * lax.rsqrt(var + LN_EPS) * g + b


def _ln_in_kernel(x_ref, g_ref, b_ref, xn_ref, xb_ref):
    y = _ln(x_ref[...], g_ref[...], b_ref[...])
    xn_ref[...] = y
    xb_ref[...] = y.astype(BF16)


def _ln_in(x, g, b, tm):
    rows = x.shape[0]
    row = pl.BlockSpec((tm, D_MODEL), lambda i: (i, 0))
    vec = pl.BlockSpec((1, D_MODEL), lambda i: (0, 0))
    return pl.pallas_call(
        _ln_in_kernel,
        grid=(rows // tm,),
        in_specs=[row, vec, vec],
        out_specs=[row, row],
        out_shape=[jax.ShapeDtypeStruct((rows, D_MODEL), F32), jax.ShapeDtypeStruct((rows, D_MODEL), BF16)],
        compiler_params=_params("parallel"),
    )(x, g.reshape(1, -1), b.reshape(1, -1))


def _ln_out_kernel(x_ref, pt_ref, g_ref, b_ref, op_ref, ot_ref, *, n_prompt_tiles):
    y = _ln(DN_ALPHA * x_ref[...] + pt_ref[...].T, g_ref[...], b_ref[...])

    @pl.when(pl.program_id(0) < n_prompt_tiles)
    def _():
        op_ref[...] = y

    @pl.when(pl.program_id(0) >= n_prompt_tiles)
    def _():
        ot_ref[...] = y


def _ln_out(x, pt, g, b, tm, rows_p):
    rows = x.shape[0]
    n_prompt_tiles = rows_p // tm
    row = pl.BlockSpec((tm, D_MODEL), lambda i: (i, 0))
    vec = pl.BlockSpec((1, D_MODEL), lambda i: (0, 0))
    return pl.pallas_call(
        functools.partial(_ln_out_kernel, n_prompt_tiles=n_prompt_tiles),
        grid=(rows // tm,),
        in_specs=[row, pl.BlockSpec((D_MODEL, tm), lambda i: (0, i)), vec, vec],
        out_specs=[pl.BlockSpec((tm, D_MODEL), lambda i: (jnp.minimum(i, n_prompt_tiles - 1), 0)),
                   pl.BlockSpec((tm, D_MODEL), lambda i: (jnp.maximum(i - n_prompt_tiles, 0), 0))],
        out_shape=[jax.ShapeDtypeStruct((rows_p, D_MODEL), F32), jax.ShapeDtypeStruct((rows - rows_p, D_MODEL), F32)],
        compiler_params=_params("arbitrary"),
    )(x, pt, g.reshape(1, -1), b.reshape(1, -1))


def _mm_kernel(x_ref, w_ref, o_ref):
    o_ref[...] = _dot(x_ref[...], w_ref[...])


def _matmul(xb, w, tm, tn):
    m, k = xb.shape
    n = w.shape[1]
    return pl.pallas_call(
        _mm_kernel,
        grid=(m // tm, n // tn),
        in_specs=[pl.BlockSpec((tm, k), lambda i, j: (i, 0)), pl.BlockSpec((k, tn), lambda i, j: (0, j))],
        out_specs=pl.BlockSpec((tm, tn), lambda i, j: (i, j)),
        out_shape=jax.ShapeDtypeStruct((m, n), F32),
        compiler_params=_params("parallel", "arbitrary"),
    )(xb, w)


def _rwkv_pre_kernel(cur_ref, prev8_ref, init_ref, mu_ref, w0_ref, a0_ref, kk_ref, ka_ref, rk_ref,
                     wup_ref, aup_ref, gup_ref, ones_ref, *out_refs, n_prompt_tiles, tiles_per_batch, dec_seq):
    i = pl.program_id(0)
    cur = cur_ref[...]
    row = lax.broadcasted_iota(I32, cur.shape, 0)
    prev = jnp.where(row == 0, jnp.broadcast_to(prev8_ref[7:8, :], cur.shape), pltpu.roll(cur, 1, axis=0))
    batch_start = ((i % tiles_per_batch) == 0).astype(I32)
    first_prompt = jnp.where(row == 0, batch_start, 0)
    first_sample = jnp.where(row % dec_seq == 0, 1, 0)
    first = jnp.where(i < n_prompt_tiles, first_prompt, first_sample)
    prev = jnp.where(first > 0, init_ref[...], prev)

    xm = cur + (prev - cur) * mu_ref[...]
    r = xm[:, 0:RWKV_WIDTH]
    k = xm[:, RWKV_WIDTH:2 * RWKV_WIDTH]
    v = xm[:, 2 * RWKV_WIDTH:3 * RWKV_WIDTH]
    wa = xm[:, 3 * RWKV_WIDTH:3 * RWKV_WIDTH + LORA_WA]
    gl = xm[:, 3 * RWKV_WIDTH + LORA_WA:]
    ones_bd = ones_ref[...]

    nz = -(w0_ref[...] + _dot_hp(jnp.tanh(wa), wup_ref[...]))
    softplus = jnp.maximum(nz, 0.0) + jnp.log1p(jnp.exp(-jnp.abs(nz)))
    decay = jnp.exp(-jnp.exp(-softplus - 0.5))
    a = jax.nn.sigmoid(a0_ref[...] + _dot_hp(wa, aup_ref[...]))
    g = _dot_hp(jax.nn.sigmoid(gl), gup_ref[...])
    kn = k * kk_ref[...]
    kn = kn / jnp.maximum(jnp.sqrt(_segsum(kn * kn, ones_bd)), 1e-12)
    k_h = k * (1.0 + (a - 1.0) * ka_ref[...])
    vals = (r, decay, k_h, v, kn, kn * a, g, _segsum(r * k_h * rk_ref[...], ones_bd) * v)
    n_out = len(vals)

    @pl.when(i < n_prompt_tiles)
    def _():
        for o_ref, val in zip(out_refs[:n_out], vals):
            o_ref[...] = val

    @pl.when(i >= n_prompt_tiles)
    def _():
        for o_ref, val in zip(out_refs[n_out:], vals):
            o_ref[...] = val


def _rwkv_pre(feat, init, mu, w0, a0, k_k, k_a, r_k, wup, aup, gup, ones_bd, n_prompt_tiles, tiles_per_batch, dec_seq):
    tm = LANE
    n_tiles = feat.shape[0] // tm
    vec = lambda n: pl.BlockSpec((1, n), lambda i: (0, 0))
    full = lambda a: pl.BlockSpec(a.shape, lambda i: (0, 0))
    out_p = pl.BlockSpec((tm, RWKV_WIDTH), lambda i: (jnp.minimum(i, n_prompt_tiles - 1), 0))
    out_t = pl.BlockSpec((tm, RWKV_WIDTH), lambda i: (jnp.maximum(i - n_prompt_tiles, 0), 0))
    shape_p = jax.ShapeDtypeStruct((n_prompt_tiles * tm, RWKV_WIDTH), F32)
    shape_t = jax.ShapeDtypeStruct(((n_tiles - n_prompt_tiles) * tm, RWKV_WIDTH), F32)
    kern = functools.partial(_rwkv_pre_kernel, n_prompt_tiles=n_prompt_tiles, tiles_per_batch=tiles_per_batch, dec_seq=dec_seq)
    outs = pl.pallas_call(
        kern,
        grid=(n_tiles,),
        in_specs=[
            pl.BlockSpec((tm, RWKV_PAD), lambda i: (i, 0)),
            pl.BlockSpec((8, RWKV_PAD), lambda i: (jnp.maximum(i * (tm // 8) - 1, 0), 0)),
            pl.BlockSpec((tm, RWKV_PAD), lambda i: (jnp.where(i == n_prompt_tiles, 1, 0), 0)),
            vec(RWKV_PAD), vec(RWKV_WIDTH), vec(RWKV_WIDTH), vec(RWKV_WIDTH), vec(RWKV_WIDTH), vec(RWKV_WIDTH),
            full(wup), full(aup), full(gup), full(ones_bd),
        ],
        out_specs=[out_p] * 8 + [out_t] * 8,
        out_shape=[shape_p] * 8 + [shape_t] * 8,
        compiler_params=_params("arbitrary"),
    )(feat, feat, init, mu, w0, a0, k_k, k_a, r_k, wup, aup, gup, ones_bd)
    return outs[:8], outs[8:]


def _rwkv_scan_kernel(r_ref, w_ref, k_ref, v_ref, kn_ref, b_ref, s0_ref, y_ref, s_ref, *, n_chunks):
    c_id = pl.program_id(1)

    @pl.when(c_id == 0)
    def _():
        s_ref[...] = s0_ref[...]

    @pl.when(c_id >= n_chunks)
    def _():
        y_ref[...] = jnp.zeros(y_ref.shape, F32)

    def step(t, carry):
        kn_t = kn_ref[t]
        w_t = w_ref[t]
        b_t = b_ref[t]
        k_t = k_ref[t]
        r_t = r_ref[t]

        def value_row(vi, c):
            s_v = s_ref[vi]
            s_kn = jnp.sum(s_v * kn_t, axis=0, keepdims=True)
            s_new = s_v * w_t - s_kn * b_t + v_ref[t, pl.ds(vi, 1), :] * k_t
            s_ref[vi] = s_new
            y_ref[t, pl.ds(vi, 1), :] = jnp.sum(s_new * r_t, axis=0, keepdims=True)
            return c

        return lax.fori_loop(0, HEAD_DIM, value_row, carry, unroll=16)

    @pl.when(c_id < n_chunks)
    def _():
        lax.fori_loop(0, r_ref.shape[0], step, 0)


def _rwkv_scan(r, w, k, v, kn, b, s0, tc, n_steps):
    steps, _, pairs = r.shape
    seq = pl.BlockSpec((tc, HEAD_DIM, LANE), lambda p, c: (c, 0, p))
    state = pl.BlockSpec((HEAD_DIM, HEAD_DIM, LANE), lambda p, c: (0, 0, p))
    return pl.pallas_call(
        functools.partial(_rwkv_scan_kernel, n_chunks=n_steps // tc),
        grid=(pairs // LANE, steps // tc),
        in_specs=[seq] * 6 + [state],
        out_specs=[seq, state],
        out_shape=[jax.ShapeDtypeStruct(r.shape, F32), jax.ShapeDtypeStruct(s0.shape, F32)],
        compiler_params=_params("parallel", "arbitrary"),
    )(r, w, k, v, kn, b, s0)


def _prompt_attn_kernel(qq_ref, kvi_ref, wi_ref, bias_ref, y_ref,
                        vt_ref, kb_ref, kib_ref, qn_ref, key_ref, sel_ref, cut_ref, acc_ref, *, n_blocks, n_sel):
    i = pl.program_id(1)
    n_kb = i + 1
    gq = GROUP * LANE

    @pl.when(i == 0)
    def _():
        for j in range(n_blocks):
            vt_ref[j] = kvi_ref[j * LANE:(j + 1) * LANE, KV_WIDTH:2 * KV_WIDTH].T.astype(BF16)
        for n in range(KV_HEADS):
            kb_ref[n] = kvi_ref[:, n * HEAD_DIM:(n + 1) * HEAD_DIM].astype(BF16)
        kib_ref[...] = kvi_ref[:, 2 * KV_WIDTH:2 * KV_WIDTH + IDX_DIM].astype(BF16)

    kpos0 = lax.broadcasted_iota(I32, (LANE, LANE), 0)
    qpos = i * LANE + lax.broadcasted_iota(I32, (LANE, LANE), 1)

    qi_all = jnp.concatenate(
        [qq_ref[:, ATTN_WIDTH + h * IDX_DIM:ATTN_WIDTH + (h + 1) * IDX_DIM] for h in range(IDX_HEADS)], axis=0).astype(BF16)
    w_t = wi_ref[...].T
    w_flat = jnp.concatenate([w_t[h:h + 1, :] for h in range(IDX_HEADS)], axis=1)

    def score_block(j, c):
        r0 = pl.multiple_of(j * LANE, LANE)
        ki = kib_ref[pl.ds(r0, LANE), :]
        s = jnp.maximum(_dot_nt(ki, qi_all), 0.0) * w_flat
        acc = s[:, 0:LANE]
        for h in range(1, IDX_HEADS):
            acc = acc + s[:, h * LANE:(h + 1) * LANE]
        acc = jnp.where(acc == 0.0, 0.0, acc)
        acc = jnp.where(kpos0 + r0 <= qpos, acc, -jnp.inf)
        key_ref[pl.ds(r0, LANE), :] = _float_key(acc)
        return c

    lax.fori_loop(0, n_kb, score_block, 0)

    row1 = (1, LANE)

    def count(flag):
        def body(j, acc):
            r0 = pl.multiple_of(j * LANE, LANE)
            return acc + flag(key_ref[pl.ds(r0, LANE), :], r0)

        return jnp.sum(lax.fori_loop(0, n_kb, body, jnp.zeros((LANE, LANE), F32)), axis=0, keepdims=True)

    thr = _kth_largest_key(lambda t: count(lambda k, r0: jnp.where(k >= t, 1.0, 0.0)), n_sel, row1)
    need = n_sel - count(lambda k, r0: jnp.where(k > thr, 1.0, 0.0))
    n_tied = count(lambda k, r0: jnp.where(k == thr, 1.0, 0.0))
    nbits = (n_blocks * LANE).bit_length()
    cut_ref[...] = jnp.full(cut_ref.shape, 2 ** nbits, I32)

    @pl.when(jnp.max(n_tied - need) > 0.0)
    def _():
        cut = _tie_cutoff(
            lambda c: count(lambda k, r0: jnp.where(k == thr, jnp.where(kpos0 + r0 < c, 1.0, 0.0), 0.0)), need, nbits, row1)
        cut_ref[...] = jnp.broadcast_to(cut, cut_ref.shape)

    cut = cut_ref[0:1, :]

    def select_block(j, c):
        r0 = pl.multiple_of(j * LANE, LANE)
        k = key_ref[pl.ds(r0, LANE), :]
        kpos = kpos0 + r0
        chosen = jnp.where(k > thr, 1.0, jnp.where(k == thr, jnp.where(kpos < cut, 1.0, 0.0), 0.0))
        sel_ref[pl.ds(r0, LANE), :] = jnp.where(kpos <= qpos, chosen, 0.0)
        return c

    lax.fori_loop(0, n_kb, select_block, 0)

    for n in range(KV_HEADS):
        q_n = jnp.concatenate(
            [qq_ref[:, (GROUP * n + g) * HEAD_DIM:(GROUP * n + g + 1) * HEAD_DIM] for g in range(GROUP)], axis=0)
        qn_ref[n] = (q_n * HEAD_DIM ** -0.5).astype(BF16)
    acc_ref[...] = jnp.zeros(acc_ref.shape, F32)

    def key_block(j, carry):
        ms, ls = carry
        r0 = pl.multiple_of(j * LANE, LANE)
        mask = sel_ref[pl.ds(r0, LANE), :] > 0.5
        near = jnp.minimum(i - j, 2)
        new_ms, new_ls = [], []
        for n in range(KV_HEADS):
            s = _dot_nt(kb_ref[n, pl.ds(r0, LANE), :], qn_ref[n]) + bias_ref[n, near]
            s = jnp.concatenate([jnp.where(mask, s[:, g * LANE:(g + 1) * LANE], NEG_BIG) for g in range(GROUP)], axis=1)
            m_new = jnp.maximum(ms[n], jnp.max(s, axis=0, keepdims=True))
            alpha = jnp.exp(ms[n] - m_new)
            p = jnp.exp(s - m_new)
            new_ls.append(alpha * ls[n] + jnp.sum(p, axis=0, keepdims=True))
            new_ms.append(m_new)
            acc_ref[n] = alpha * acc_ref[n] + _dot(vt_ref[j, n * HEAD_DIM:(n + 1) * HEAD_DIM, :], p.astype(BF16))
        return tuple(new_ms), tuple(new_ls)

    init = (tuple(jnp.full((1, gq), NEG_BIG, F32) for _ in range(KV_HEADS)),
            tuple(jnp.zeros((1, gq), F32) for _ in range(KV_HEADS)))
    _, ls = lax.fori_loop(0, n_kb, key_block, init)
    outs = []
    for n in range(KV_HEADS):
        o = acc_ref[n] / ls[n]
        outs += [o[:, g * LANE:(g + 1) * LANE].T for g in range(GROUP)]
    y_ref[...] = jnp.concatenate(outs, axis=1)


def _prompt_attn(qq, kvi, bias_tiles, n_batch, n_blocks, n_sel):
    t_pad = n_blocks * LANE
    kern = functools.partial(_prompt_attn_kernel, n_blocks=n_blocks, n_sel=n_sel)
    return pl.pallas_call(
        kern,
        grid=(n_batch, n_blocks),
        in_specs=[
            pl.BlockSpec((LANE, 2 * ATTN_WIDTH), lambda b, i: (b * n_blocks + i, 0)),
            pl.BlockSpec((t_pad, kvi.shape[1]), lambda b, i: (b, 0)),
            pl.BlockSpec((LANE, LANE), lambda b, i: (b * n_blocks + i, (2 * KV_WIDTH + IDX_DIM) // LANE)),
            pl.BlockSpec(bias_tiles.shape, lambda b, i: (0, 0, 0, 0)),
        ],
        out_specs=pl.BlockSpec((LANE, ATTN_WIDTH), lambda b, i: (b * n_blocks + i, 0)),
        out_shape=jax.ShapeDtypeStruct((n_batch * t_pad, ATTN_WIDTH), F32),
        scratch_shapes=[
            pltpu.VMEM((n_blocks, KV_WIDTH, LANE), BF16),
            pltpu.VMEM((KV_HEADS, t_pad, HEAD_DIM), BF16),
            pltpu.VMEM((t_pad, IDX_DIM), BF16),
            pltpu.VMEM((KV_HEADS, GROUP * LANE, HEAD_DIM), BF16),
            pltpu.VMEM((t_pad, LANE), I32),
            pltpu.VMEM((t_pad, LANE), F32),
            pltpu.VMEM((8, LANE), I32),
            pltpu.VMEM((KV_HEADS, HEAD_DIM, GROUP * LANE), F32),
        ],
        compiler_params=_params("parallel", "arbitrary"),
    )(qq, kvi, kvi, bias_tiles)


PAGES_PER_STEP = 8
Q_PAD = 8


def _sample_score_kernel(pt_ref, qi_ref, wb_ref, *refs):
    page_refs, out_ref = refs[:PAGES_PER_STEP], refs[PAGES_PER_STEP]
    qi = qi_ref[...].astype(BF16)
    wb = wb_ref[...]
    for u in range(PAGES_PER_STEP):
        s = jnp.maximum(_dot_nt(qi, page_refs[u][...].astype(BF16)), 0.0) * wb
        acc = s[0:Q_PAD]
        for h in range(1, IDX_HEADS):
            acc = acc + s[h * Q_PAD:(h + 1) * Q_PAD]
        out_ref[u] = acc


def _sample_scores(page_table, qi8, wb, cache_kidx):
    n_batch, n_pages = page_table.shape
    page_spec = lambda u: pl.BlockSpec((None, PAGE, IDX_DIM), lambda b, s, pt: (pt[b, s * PAGES_PER_STEP + u], 0, 0))
    per_batch = pl.BlockSpec((None, IDX_HEADS * Q_PAD, IDX_DIM), lambda b, s, pt: (b, 0, 0))
    return pl.pallas_call(
        _sample_score_kernel,
        grid_spec=pltpu.PrefetchScalarGridSpec(
            num_scalar_prefetch=1,
            grid=(n_batch, n_pages // PAGES_PER_STEP),
            in_specs=[per_batch, per_batch] + [page_spec(u) for u in range(PAGES_PER_STEP)],
            out_specs=pl.BlockSpec((None, PAGES_PER_STEP, Q_PAD, PAGE), lambda b, s, pt: (b, s, 0, 0)),
        ),
        out_shape=jax.ShapeDtypeStruct((n_batch, n_pages, Q_PAD, PAGE), F32),
        compiler_params=_params("parallel", "arbitrary"),
    )(page_table, qi8, wb, *([cache_kidx] * PAGES_PER_STEP))


def _sample_select_kernel(sc_ref, qi_ref, wb_ref, kin_ref, sel_ref, key_ref, *, n_pages, dec_seq, n_sel):
    qrow = lax.broadcasted_iota(I32, (Q_PAD, PAGE), 0)
    lane = lax.broadcasted_iota(I32, (Q_PAD, PAGE), 1)
    s = jnp.maximum(_dot_nt(qi_ref[...].astype(BF16), kin_ref[...].astype(BF16)), 0.0) * wb_ref[...]
    acc = s[0:Q_PAD]
    for h in range(1, IDX_HEADS):
        acc = acc + s[h * Q_PAD:(h + 1) * Q_PAD]
    new_valid = jnp.where(lane < dec_seq, jnp.where(lane <= qrow, 1, 0), 0) > 0
    past = sc_ref[...]
    key_ref[0:n_pages] = _float_key(jnp.where(past == 0.0, 0.0, past))
    key_ref[n_pages] = _float_key(jnp.where(new_valid, jnp.where(acc == 0.0, 0.0, acc), -jnp.inf))

    def lane_count(x):
        return jnp.sum(jnp.sum(x, axis=0), axis=1, keepdims=True)

    col1 = (Q_PAD, 1)
    thr = _kth_largest_key(lambda t: lane_count(jnp.where(key_ref[...] >= t, 1.0, 0.0)), n_sel, col1)
    keys = key_ref[...]
    need = n_sel - lane_count(jnp.where(keys > thr, 1.0, 0.0))
    shape3 = (n_pages + 1, Q_PAD, PAGE)
    kidx = lax.broadcasted_iota(I32, shape3, 0) * PAGE + lax.broadcasted_iota(I32, shape3, 2)
    cut = _tie_cutoff(
        lambda c: lane_count(jnp.where(key_ref[...] == thr, jnp.where(kidx < c, 1.0, 0.0), 0.0)),
        need, ((n_pages + 1) * PAGE).bit_length(), col1)
    chosen = jnp.where(keys > thr, 1.0, jnp.where(keys == thr, jnp.where(kidx < cut, 1.0, 0.0), 0.0))
    sel_ref[0:n_pages] = chosen[0:n_pages]
    sel_ref[n_pages] = jnp.where(new_valid, chosen[n_pages], 0.0)


def _sample_select(sc, qi8, wb, ki_new, dec_seq, n_sel):
    n_batch, n_pages = sc.shape[:2]
    kern = functools.partial(_sample_select_kernel, n_pages=n_pages, dec_seq=dec_seq, n_sel=n_sel)
    per_batch = lambda a: pl.BlockSpec((None,) + a.shape[1:], lambda b: (b,) + (0,) * (a.ndim - 1))
    return pl.pallas_call(
        kern,
        grid=(n_batch,),
        in_specs=[per_batch(sc), per_batch(qi8), per_batch(wb), per_batch(ki_new)],
        out_specs=pl.BlockSpec((None, n_pages + 1, Q_PAD, PAGE), lambda b: (b, 0, 0, 0)),
        out_shape=jax.ShapeDtypeStruct((n_batch, n_pages + 1, Q_PAD, PAGE), F32),
        scratch_shapes=[pltpu.VMEM((n_pages + 1, Q_PAD, PAGE), I32)],
        compiler_params=_params("parallel"),
    )(sc, qi8, wb, ki_new)


def _sample_attn_kernel(pt_ref, q_ref, sel_ref, selnew_ref, knew_ref, vnew_ref, bias_ref, *refs, n_steps):
    k_refs = refs[:PAGES_PER_STEP]
    v_refs = refs[PAGES_PER_STEP:2 * PAGES_PER_STEP]
    o_ref, m_ref, l_ref, acc_ref = refs[2 * PAGES_PER_STEP:]
    s_id = pl.program_id(1)
    rows = KV_HEADS * GROUP * Q_PAD
    per_kv = GROUP * Q_PAD
    q = (q_ref[...] * HEAD_DIM ** -0.5).astype(BF16)

    @pl.when(s_id == 0)
    def _():
        m_ref[...] = jnp.full(m_ref.shape, NEG_BIG, F32)
        l_ref[...] = jnp.zeros(l_ref.shape, F32)
        acc_ref[...] = jnp.zeros(acc_ref.shape, F32)

    def attend(k_pages, v_pages, sel_pages, bias_pages):
        s_blocks, sel_blocks = [], []
        for kp, sp, bp in zip(k_pages, sel_pages, bias_pages):
            kb = kp.astype(BF16)
            s = jnp.concatenate(
                [_dot(q[n * per_kv:(n + 1) * per_kv], kb[n * HEAD_DIM:(n + 1) * HEAD_DIM]) for n in range(KV_HEADS)],
                axis=0) + bp
            s_blocks.append(s)
            sel_blocks.append(jnp.concatenate([sp] * (KV_HEADS * GROUP), axis=0) > 0.5)
        s = jnp.concatenate(s_blocks, axis=1)
        sel = jnp.concatenate(sel_blocks, axis=1)
        s = jnp.where(sel, s, NEG_BIG)
        m_old = m_ref[...]
        m_new = jnp.maximum(m_old, jnp.max(s, axis=1, keepdims=True))
        alpha = jnp.exp(m_old - m_new)
        p = jnp.where(sel, jnp.exp(s - m_new), 0.0)
        l_ref[...] = alpha * l_ref[...] + jnp.sum(p, axis=1, keepdims=True)
        m_ref[...] = m_new
        pb = p.astype(BF16)
        pv = None
        for u, vp in enumerate(v_pages):
            vb = vp.astype(BF16)
            pu = pb[:, u * PAGE:(u + 1) * PAGE]
            part = jnp.concatenate(
                [_dot_nt(pu[n * per_kv:(n + 1) * per_kv], vb[n * HEAD_DIM:(n + 1) * HEAD_DIM]) for n in range(KV_HEADS)], axis=0)
            pv = part if pv is None else pv + part
        acc_ref[...] = alpha * acc_ref[...] + pv

    @pl.when(s_id < n_steps)
    def _():
        far, near = bias_ref[0], bias_ref[1]
        biases = [far] * PAGES_PER_STEP
        last = s_id == n_steps - 1
        biases[-1] = jnp.where(last, near, far)
        attend([r[...] for r in k_refs], [r[...] for r in v_refs], [sel_ref[u] for u in range(PAGES_PER_STEP)], biases)

    @pl.when(s_id == n_steps)
    def _():
        attend([knew_ref[...]], [vnew_ref[...]], [selnew_ref[...]], [bias_ref[2]])
        o_ref[...] = acc_ref[...] / l_ref[...]


def _sample_attn(page_table, q8, sel, k_new, v_new, bias_tiles, cache_k, cache_v):
    n_batch, n_pages = page_table.shape
    n_steps = n_pages // PAGES_PER_STEP
    rows = KV_HEADS * GROUP * Q_PAD
    kern = functools.partial(_sample_attn_kernel, n_steps=n_steps)

    def page_spec(u):
        return pl.BlockSpec((None, KV_WIDTH, PAGE),
                            lambda b, s, pt: (pt[b, jnp.minimum(s, n_steps - 1) * PAGES_PER_STEP + u], 0, 0))

    per_batch = lambda a: pl.BlockSpec((None,) + a.shape[1:], lambda b, s, pt: (b,) + (0,) * (a.ndim - 1))
    return pl.pallas_call(
        kern,
        grid_spec=pltpu.PrefetchScalarGridSpec(
            num_scalar_prefetch=1,
            grid=(n_batch, n_steps + 1),
            in_specs=[
                per_batch(q8),
                pl.BlockSpec((None, PAGES_PER_STEP, Q_PAD, PAGE), lambda b, s, pt: (b, jnp.minimum(s, n_steps - 1), 0, 0)),
                pl.BlockSpec((None, None, Q_PAD, PAGE), lambda b, s, pt: (b, n_pages, 0, 0)),
                per_batch(k_new), per_batch(v_new),
                pl.BlockSpec(bias_tiles.shape, lambda b, s, pt: (0, 0, 0)),
            ] + [page_spec(u) for u in range(PAGES_PER_STEP)] * 2,
            out_specs=pl.BlockSpec((None, rows, HEAD_DIM), lambda b, s, pt: (b, 0, 0)),
            scratch_shapes=[pltpu.VMEM((rows, 1), F32), pltpu.VMEM((rows, 1), F32), pltpu.VMEM((rows, HEAD_DIM), F32)],
        ),
        out_shape=jax.ShapeDtypeStruct((n_batch, rows, HEAD_DIM), F32),
        compiler_params=_params("parallel", "arbitrary"),
    )(page_table, q8, sel, sel, k_new, v_new, bias_tiles, *([cache_k] * PAGES_PER_STEP), *([cache_v] * PAGES_PER_STEP))


def _mix_kernel(ysp_ref, yst_ref, bonusp_ref, bonust_ref, gp_ref, gt_ref, yap_ref, yat_ref,
                xn_ref, gng_ref, gnb_ref, ones_ref, wor_ref, woa_ref, lg_ref, lb_ref,
                x1_ref, x1b_ref, x1t_ref, *, n_prompt_tiles):
    is_prompt = pl.program_id(0) < n_prompt_tiles
    pick = lambda p_ref, t_ref: jnp.where(is_prompt, p_ref[...], t_ref[...])
    ones_bd = ones_ref[...]
    ys = pick(ysp_ref, yst_ref)
    inv = 1.0 / HEAD_DIM
    yc = ys - _segsum(ys, ones_bd) * inv
    var = _segsum(yc * yc, ones_bd) * inv
    yr = (yc * lax.rsqrt(var + GN_EPS) * gng_ref[...] + gnb_ref[...] + pick(bonusp_ref, bonust_ref)) * pick(gp_ref, gt_ref)
    mix = _dot(yr.astype(BF16), wor_ref[...]) + _dot(pick(yap_ref, yat_ref).astype(BF16), woa_ref[...])
    x1 = _ln(DN_ALPHA * xn_ref[...] + mix, lg_ref[...], lb_ref[...])
    x1_ref[...] = x1
    x1b_ref[...] = x1.astype(BF16)
    x1t_ref[...] = x1.T.astype(BF16)


def _mix(ys, bonus, g, ya, xn, gn_g, gn_b, ones_bd, wo_r, wo_a, ln_g, ln_b, tm):
    rows = xn.shape[0]
    n_prompt_tiles = ys[0].shape[0] // tm
    half_p = pl.BlockSpec((tm, RWKV_WIDTH), lambda i: (jnp.minimum(i, n_prompt_tiles - 1), 0))
    half_t = pl.BlockSpec((tm, RWKV_WIDTH), lambda i: (jnp.maximum(i - n_prompt_tiles, 0), 0))
    row = pl.BlockSpec((tm, D_MODEL), lambda i: (i, 0))
    vec = lambda n: pl.BlockSpec((1, n), lambda i: (0, 0))
    full = lambda a: pl.BlockSpec(a.shape, lambda i: (0, 0))
    return pl.pallas_call(
        functools.partial(_mix_kernel, n_prompt_tiles=n_prompt_tiles),
        grid=(rows // tm,),
        in_specs=[half_p, half_t] * 4 + [row, vec(RWKV_WIDTH), vec(RWKV_WIDTH), full(ones_bd), full(wo_r), full(wo_a),
                                         vec(D_MODEL), vec(D_MODEL)],
        out_specs=[row, row, pl.BlockSpec((D_MODEL, tm), lambda i: (0, i))],
        out_shape=[jax.ShapeDtypeStruct((rows, D_MODEL), F32), jax.ShapeDtypeStruct((rows, D_MODEL), BF16),
                   jax.ShapeDtypeStruct((D_MODEL, rows), BF16)],
        compiler_params=_params("parallel"),
    )(*ys, *bonus, *g, *ya, xn, gn_g, gn_b, ones_bd, wo_r, wo_a, ln_g, ln_b)


CAND_PAIRS = [(c, d) for c in range(PEER_TOPK) for d in range(PEER_TOPK) if (c + 1) * (d + 1) <= PEER_TOPK]


def _top_rows(x, n):
    rows = []
    for _ in range(n):
        m = jnp.max(x, axis=0, keepdims=True)
        rows.append(m)
        x = jnp.where(x == m, -jnp.inf, x)
    return rows


def _peer_route_kernel(x_ref, wq_ref, sub_ref, s1_ref, s2_ref, e2_ref, thr_ref, m1_ref, zinv_ref):
    q = _dot(x_ref[...], wq_ref[...])
    for h in range(PEER_HEADS):
        base = h * 2 * PEER_HALF
        s1 = _dot_nt_hp(sub_ref[h, 0], q[:, base:base + PEER_HALF])
        s2 = _dot_nt_hp(sub_ref[h, 1], q[:, base + PEER_HALF:base + 2 * PEER_HALF])
        top1 = _top_rows(s1, PEER_TOPK)
        top2 = _top_rows(s2, PEER_TOPK)
        cand = jnp.concatenate([top1[c] + top2[d] for c, d in CAND_PAIRS]
                               + [jnp.full_like(top1[0], -jnp.inf)] * (-len(CAND_PAIRS) % 8), axis=0)
        best = _top_rows(cand, PEER_TOPK)
        thr = best[-1]
        m = top1[0] + top2[0]
        z = jnp.sum(jnp.where(cand >= thr, jnp.exp(cand - m), 0.0), axis=0, keepdims=True)
        s1_ref[h] = s1
        s2_ref[h] = s2
        e2_ref[h] = jnp.exp(s2 - top2[0])
        thr_ref[h:h + 1, :] = thr
        m1_ref[h:h + 1, :] = top1[0]
        zinv_ref[h:h + 1, :] = 1.0 / z


def _peer_route(x1b, wq, subkeys):
    rows = x1b.shape[0]
    tm = LANE
    stat = pl.BlockSpec((PEER_HEADS, tm), lambda i: (0, i))
    big = pl.BlockSpec((PEER_HEADS, PEER_NKEYS, tm), lambda i: (0, 0, i))
    stat_shape = jax.ShapeDtypeStruct((PEER_HEADS, rows), F32)
    big_shape = jax.ShapeDtypeStruct((PEER_HEADS, PEER_NKEYS, rows), F32)
    return pl.pallas_call(
        _peer_route_kernel,
        grid=(rows // tm,),
        in_specs=[pl.BlockSpec((tm, D_MODEL), lambda i: (i, 0)),
                  pl.BlockSpec(wq.shape, lambda i: (0, 0)),
                  pl.BlockSpec(subkeys.shape, lambda i: (0, 0, 0, 0))],
        out_specs=[big, big, big, stat, stat, stat],
        out_shape=[big_shape, big_shape, big_shape, stat_shape, stat_shape, stat_shape],
        compiler_params=_params("parallel"),
    )(x1b, wq, subkeys)


EXPERT_ROWS = 8
MXU_DEPTH = 256


def _peer_dense_kernel(xt_ref, u_ref, vt_ref, s1_ref, s2_ref, e2_ref, thr_ref, m1_ref, zinv_ref, o_ref, *scratch):
    n_sub = EXPERT_ROWS * PEER_NKEYS // MXU_DEPTH
    ht_refs, g_refs, a_refs = scratch[:n_sub], scratch[n_sub:2 * n_sub], scratch[2 * n_sub:]
    j = pl.program_id(1)
    tm = xt_ref.shape[1]

    @pl.when(j == 0)
    def _():
        o_ref[...] = jnp.zeros(o_ref.shape, F32)

    per = MXU_DEPTH // PEER_NKEYS
    half = PEER_NKEYS // 2

    def gate_rows(p):
        out = []
        for e in range(p * per, (p + 1) * per):
            s1_i = [s1_ref[h, e:e + 1, :] for h in range(PEER_HEADS)]
            out.append((s1_i, [jnp.exp(s1_i[h] - m1_ref[h:h + 1, :]) * zinv_ref[h:h + 1, :] for h in range(PEER_HEADS)]))
        return out

    def gate_block(p, rows_p, c, jh):
        cols = slice(c, c + LANE)
        jr = slice(jh * half, (jh + 1) * half)
        acc = [None] * per
        for h in range(PEER_HEADS):
            s2, e2, thr = s2_ref[h, jr, cols], e2_ref[h, jr, cols], thr_ref[h:h + 1, cols]
            for e in range(per):
                s1_i, f_i = rows_p[e]
                g = jnp.where(s1_i[h][:, cols] + s2 >= thr, e2 * f_i[h][:, cols], 0.0)
                acc[e] = g if acc[e] is None else acc[e] + g
        for e in range(per):
            g_refs[p][e * PEER_NKEYS + jh * half:e * PEER_NKEYS + (jh + 1) * half, cols] = acc[e]

    def gate_blocks(p):
        rows_p = gate_rows(p)
        return [functools.partial(gate_block, p, rows_p, c, jh) for c in range(0, tm, LANE) for jh in range(2)]

    def up_piece(p, k):
        r = p * MXU_DEPTH + k * PEER_NKEYS
        ht_refs[p][k * PEER_NKEYS:(k + 1) * PEER_NKEYS, :] = _dot(u_ref[r:r + PEER_NKEYS, :], xt_ref[...])

    def act(p):
        he = ht_refs[p][...]
        a_refs[p][...] = (0.5 * he * (1.0 + lax.erf(he * (2.0 ** -0.5))) * g_refs[p][...]).astype(BF16)

    down_rows = 512

    def down_piece(p, m):
        r = p * MXU_DEPTH
        rows = slice(m * down_rows, (m + 1) * down_rows)
        o_ref[rows, :] += _dot(vt_ref[rows, r:r + MXU_DEPTH], a_refs[p][...])

    def run(pieces):
        for piece in pieces:
            piece()

    ups = lambda p: [functools.partial(up_piece, p, k) for k in range(per)]
    downs = lambda p: [functools.partial(down_piece, p, m) for m in range(D_MODEL // down_rows)]
    run(gate_blocks(0) + ups(0))
    for p in range(n_sub):
        if p + 1 < n_sub:
            run(gate_blocks(p + 1) + ups(p + 1))
        act(p)
        run(downs(p))


def _peer_dense(x1t, u, vt, s1, s2, e2, thr, m1, zinv, tm):
    rows = x1t.shape[1]
    eb = EXPERT_ROWS * PEER_NKEYS
    stat = pl.BlockSpec((PEER_HEADS, tm), lambda i, j: (0, i))
    big = pl.BlockSpec((PEER_HEADS, PEER_NKEYS, tm), lambda i, j: (0, 0, i))
    return pl.pallas_call(
        _peer_dense_kernel,
        grid=(rows // tm, PEER_EXPERTS // eb),
        in_specs=[pl.BlockSpec((D_MODEL, tm), lambda i, j: (0, i)),
                  pl.BlockSpec((eb, D_MODEL), lambda i, j: (j, 0)),
                  pl.BlockSpec((D_MODEL, eb), lambda i, j: (0, j)),
                  pl.BlockSpec((PEER_HEADS, EXPERT_ROWS, tm), lambda i, j: (0, j, i)),
                  big, big, stat, stat, stat],
        out_specs=pl.BlockSpec((D_MODEL, tm), lambda i, j: (0, i)),
        out_shape=jax.ShapeDtypeStruct((D_MODEL, rows), F32),
        scratch_shapes=([pltpu.VMEM((MXU_DEPTH, tm), F32)] * (2 * eb // MXU_DEPTH)
                        + [pltpu.VMEM((MXU_DEPTH, tm), BF16)] * (eb // MXU_DEPTH)),
        compiler_params=_params("parallel", "arbitrary"),
    )(x1t, u, vt, s1, s2, e2, thr, m1, zinv)


def _rel_buckets(dist):
    max_exact = REL_BUCKETS // 2
    d = np.maximum(dist, 0)
    ratio = np.log(np.maximum(d, 1).astype(np.float32) / np.float32(max_exact)) / np.float32(math.log(REL_MAX_DIST / max_exact))
    log_b = max_exact + (ratio * np.float32(REL_BUCKETS - max_exact)).astype(np.int32)
    return np.where(d < max_exact, d, np.minimum(log_b, REL_BUCKETS - 1)).astype(np.int32)


def _bias_lookup(rel_bias, dist):
    onehot = np.eye(REL_BUCKETS, dtype=np.float32)[_rel_buckets(dist)]
    return jnp.dot(jnp.asarray(onehot), rel_bias.astype(F32), precision=lax.Precision.HIGHEST)


def _prompt_bias_tiles(rel_bias):
    kk = np.arange(LANE)[:, None]
    qq = np.arange(LANE)[None, :]
    tiles = []
    for delta in range(3):
        b = _bias_lookup(rel_bias, delta * LANE + qq - kk)
        b = b.reshape(LANE, LANE, KV_HEADS, GROUP).transpose(2, 0, 3, 1).reshape(KV_HEADS, LANE, GROUP * LANE)
        tiles.append(b)
    return jnp.stack(tiles, axis=1).astype(F32)


def _sample_bias_tiles(rel_bias, past_len, dec_seq):
    q = np.arange(Q_PAD)[:, None]
    off = np.arange(PAGE)[None, :]
    qpos = past_len + np.minimum(q, dec_seq - 1)
    dists = [qpos - 0 * off - (past_len - 2 * PAGE), qpos - (past_len - PAGE + off), qpos - (past_len + np.minimum(off, dec_seq - 1))]
    tiles = []
    for d in dists:
        b = _bias_lookup(rel_bias, d + 0 * off)
        tiles.append(b.transpose(2, 0, 1).reshape(ATTN_HEADS * Q_PAD, PAGE))
    return jnp.stack(tiles).astype(F32)


def kernel(x_prompt, x_sample, cache_k, cache_v, cache_kidx, state_wkv, state_shift, page_table, meta_tokens, ln_in_g, ln_in_b, rel_bias, w_in, mu_shift, w0, w_up, a0, a_up, g_up, k_k, k_a, r_k, gn_g, gn_b, w_o, ln1_g, ln1_b, peer_wq, peer_subkeys, peer_u, peer_v, ln2_g, ln2_b):
    n_batch, seq, _ = x_prompt.shape
    dec_batch, dec_seq, _ = x_sample.shape
    n_pages = page_table.shape[1]
    past_len = n_pages * PAGE
    t_len = seq + N_META
    n_blocks = -(-t_len // LANE)
    t_pad = n_blocks * LANE
    rows_p = n_batch * t_pad
    rows_s = dec_batch * dec_seq
    assert rows_s == LANE and Q_PAD >= dec_seq and n_pages % PAGES_PER_STEP == 0
    assert past_len >= 2 * PAGE + REL_MAX_DIST
    rows = -(-(rows_p + rows_s) // ROW_ALIGN) * ROW_ALIGN
    layer = 0

    meta = jnp.broadcast_to(meta_tokens[None], (n_batch, N_META, D_MODEL))
    xp = jnp.pad(jnp.concatenate([meta, x_prompt], axis=1), ((0, 0), (0, t_pad - t_len), (0, 0)))
    x_all = jnp.concatenate([xp.reshape(rows_p, D_MODEL), x_sample.reshape(rows_s, D_MODEL),
                             jnp.zeros((rows - rows_p - rows_s, D_MODEL), F32)], axis=0)
    xn, xb = _ln_in(x_all, ln_in_g, ln_in_b, 256)

    w = w_in[layer]
    c0 = RWKV_COLS
    w_rwkv = jnp.pad(w[:, :c0], ((0, 0), (0, RWKV_PAD - RWKV_COLS))).astype(BF16)
    w_qq = jnp.concatenate([w[:, c0:c0 + ATTN_WIDTH], w[:, c0 + ATTN_WIDTH + 2 * KV_WIDTH:c0 + 2 * ATTN_WIDTH + 2 * KV_WIDTH]], axis=1).astype(BF16)
    c_ki = c0 + 2 * ATTN_WIDTH + 2 * KV_WIDTH
    w_kvi = jnp.concatenate([w[:, c0 + ATTN_WIDTH:c0 + ATTN_WIDTH + 2 * KV_WIDTH], w[:, c_ki:c_ki + IDX_DIM],
                             jnp.pad(w[:, c_ki + IDX_DIM:], ((0, 0), (0, LANE - IDX_HEADS)))], axis=1).astype(BF16)
    feat = _matmul(xb, w_rwkv, 640, RWKV_PAD // 3)
    qq = _matmul(xb, w_qq, 640, 1024)
    kvi = _matmul(xb, w_kvi, 640, w_kvi.shape[1])

    def prompt_rows(a):
        return a[:rows_p].reshape(n_batch, t_pad, -1)[:, :t_len]

    def sample_rows(a):
        return a[rows_p:rows_p + rows_s].reshape(dec_batch, dec_seq, -1)

    ones_bd = jnp.asarray(np.kron(np.eye(RWKV_HEADS), np.ones((HEAD_DIM, HEAD_DIM))), BF16)
    pad_cols = lambda a: jnp.pad(a, ((0, 0), (0, RWKV_PAD - RWKV_COLS)))
    init = jnp.zeros((dec_batch, dec_seq, RWKV_PAD), F32).at[:, 0].set(pad_cols(state_shift[layer]))
    init = jnp.concatenate([jnp.zeros((LANE, RWKV_PAD), F32), init.reshape(rows_s, RWKV_PAD)], axis=0)
    wup = jnp.pad(w_up[layer], ((0, ICLR_LORA), (0, 0)))
    aup = jnp.pad(a_up[layer], ((DECAY_LORA, 0), (0, 0)))
    gup = jnp.pad(g_up[layer], ((0, GATE_PAD - GATE_LORA), (0, 0)))
    vec = lambda a: a.reshape(1, -1)
    pre = _rwkv_pre(feat, init, vec(pad_cols(mu_shift[layer][None])), vec(w0[layer]), vec(a0[layer]), vec(k_k[layer]),
                    vec(k_a[layer]), vec(r_k[layer]), wup, aup, gup, ones_bd, rows_p // LANE, n_blocks, dec_seq)
    pre_p, pre_t = pre

    def to_scan(a, nb, steps):
        return a.reshape(nb, steps, RWKV_HEADS, HEAD_DIM).transpose(1, 3, 0, 2).reshape(steps, HEAD_DIM, nb * RWKV_HEADS)

    take_p = lambda a: a.reshape(n_batch, t_pad, -1)
    take_s = lambda a: a[:rows_s].reshape(dec_batch, dec_seq, -1)

    def from_scan(y, nb, steps):
        return y.reshape(steps, HEAD_DIM, nb, RWKV_HEADS).transpose(2, 0, 3, 1).reshape(nb, steps, RWKV_WIDTH)

    def state_in(s):
        nb = s.shape[0]
        return s.transpose(2, 3, 0, 1).reshape(HEAD_DIM, HEAD_DIM, nb * RWKV_HEADS)

    def state_out(s, nb):
        return s.reshape(HEAD_DIM, HEAD_DIM, nb, RWKV_HEADS).transpose(2, 3, 0, 1)

    tc = max(d for d in range(1, 49) if math.gcd(t_len, t_pad) % d == 0)
    y_p, wkv_p = _rwkv_scan(*[to_scan(take_p(a), n_batch, t_pad) for a in pre_p[:6]],
                            jnp.zeros((HEAD_DIM, HEAD_DIM, n_batch * RWKV_HEADS), F32), tc, t_len)
    y_s, wkv_s = _rwkv_scan(*[to_scan(take_s(a), dec_batch, dec_seq) for a in pre_t[:6]],
                            state_in(state_wkv[layer]), dec_seq, dec_seq)
    rows_t = rows - rows_p
    pad_tail = lambda a: jnp.pad(a, ((0, rows_t - rows_s), (0, 0)))
    ys = (from_scan(y_p, n_batch, t_pad).reshape(rows_p, RWKV_WIDTH),
          pad_tail(from_scan(y_s, dec_batch, dec_seq).reshape(rows_s, RWKV_WIDTH)))

    ya_p = _prompt_attn(qq, kvi, _prompt_bias_tiles(rel_bias), n_batch, n_blocks, min(IDX_TOPK, t_len // 4))

    qq_s, kvi_s = sample_rows(qq), sample_rows(kvi)

    def pad_q(a):
        a = jnp.pad(a.transpose(0, 2, 1, 3), ((0, 0), (0, 0), (0, Q_PAD - dec_seq), (0, 0)))
        return a.reshape(dec_batch, -1, a.shape[-1])

    qi8 = pad_q(qq_s[..., ATTN_WIDTH:].reshape(dec_batch, dec_seq, IDX_HEADS, IDX_DIM))
    wi_s = kvi_s[..., 2 * KV_WIDTH + IDX_DIM:2 * KV_WIDTH + IDX_DIM + IDX_HEADS]
    wb = jnp.broadcast_to(pad_q(wi_s[..., None]), (dec_batch, IDX_HEADS * Q_PAD, IDX_DIM))
    q8 = pad_q(qq_s[..., :ATTN_WIDTH].reshape(dec_batch, dec_seq, ATTN_HEADS, HEAD_DIM))
    pad_keys = lambda a: jnp.pad(a, ((0, 0), (0, PAGE - dec_seq), (0, 0)))
    ki_new = pad_keys(kvi_s[..., 2 * KV_WIDTH:2 * KV_WIDTH + IDX_DIM])
    k_new = pad_keys(kvi_s[..., :KV_WIDTH])
    v_new = pad_keys(kvi_s[..., KV_WIDTH:2 * KV_WIDTH])
    n_pool = cache_k.shape[1]
    pages_t = lambda c: c[layer].transpose(0, 2, 3, 1).reshape(n_pool, KV_WIDTH, PAGE)
    sc = _sample_scores(page_table, qi8, wb, cache_kidx[layer])
    sel = _sample_select(sc, qi8, wb, ki_new, dec_seq, min(IDX_TOPK, (past_len + dec_seq) // 4))
    o_s = _sample_attn(page_table, q8, sel, k_new.transpose(0, 2, 1), v_new.transpose(0, 2, 1),
                       _sample_bias_tiles(rel_bias, past_len, dec_seq), pages_t(cache_k), pages_t(cache_v))
    ya_s = o_s.reshape(dec_batch, ATTN_HEADS, Q_PAD, HEAD_DIM)[:, :, :dec_seq].transpose(0, 2, 1, 3).reshape(rows_s, ATTN_WIDTH)

    wo = w_o[layer].astype(BF16)
    x1, x1b, x1t = _mix(ys, (pre_p[7], pre_t[7]), (pre_p[6], pre_t[6]), (ya_p, pad_tail(ya_s)), xn,
                        vec(gn_g[layer]), vec(gn_b[layer]), ones_bd,
                        wo[:RWKV_WIDTH], wo[RWKV_WIDTH:], vec(ln1_g[layer]), vec(ln1_b[layer]), 256)
    routing = _peer_route(x1b, peer_wq[layer].astype(BF16), peer_subkeys[layer])
    peer_t = _peer_dense(x1t, peer_u[layer].astype(BF16), peer_v[layer].T.astype(BF16), *routing, 512)
    y_p, y_t = _ln_out(x1, peer_t, ln2_g[layer], ln2_b[layer], 256, rows_p)

    last_p = feat[jnp.arange(n_batch) * t_pad + (t_len - 1)]
    last_s = feat[rows_p + jnp.arange(dec_batch) * dec_seq + (dec_seq - 1)]
    kvi_p = prompt_rows(kvi)
    kv4 = lambda a, nb, steps: a.reshape(nb, steps, KV_HEADS, HEAD_DIM)[None]
    return (
        y_p.reshape(n_batch, t_pad, D_MODEL)[:, N_META:t_len], y_t[:rows_s].reshape(dec_batch, dec_seq, D_MODEL),
        kv4(kvi_p[..., :KV_WIDTH], n_batch, t_len), kv4(kvi_p[..., KV_WIDTH:2 * KV_WIDTH], n_batch, t_len),
        kvi_p[..., 2 * KV_WIDTH:2 * KV_WIDTH + IDX_DIM][None],
        state_out(wkv_p, n_batch)[None], last_p[:, :RWKV_COLS][None],
        kv4(kvi_s[..., :KV_WIDTH], dec_batch, dec_seq), kv4(kvi_s[..., KV_WIDTH:2 * KV_WIDTH], dec_batch, dec_seq),
        kvi_s[..., 2 * KV_WIDTH:2 * KV_WIDTH + IDX_DIM][None],
        state_out(wkv_s, dec_batch)[None], last_s[:, :RWKV_COLS][None],
    )
```

```python
import functools
import math

import numpy as np
import jax
import jax.numpy as jnp
from jax import lax
from jax.experimental import pallas as pl
from jax.experimental.pallas import tpu as pltpu

F32, BF16, I32 = jnp.float32, jnp.bfloat16, jnp.int32

D_MODEL = 2048
N_META = 16
HEAD_DIM = 64
RWKV_WIDTH = 1024
ATTN_WIDTH = 1024
RWKV_HEADS = 16
ATTN_HEADS = 16
KV_HEADS = 4
GROUP = 4
KV_WIDTH = 256
DECAY_LORA = 64
ICLR_LORA = 64
GATE_LORA = 160
RWKV_COLS = 3 * RWKV_WIDTH + DECAY_LORA + ICLR_LORA + GATE_LORA
RWKV_PAD = 3456
LORA_WA = DECAY_LORA + ICLR_LORA
GATE_PAD = RWKV_PAD - 3 * RWKV_WIDTH - LORA_WA
GN_EPS = 64e-5
IDX_HEADS = 8
IDX_DIM = 128
IDX_TOPK = 256
REL_BUCKETS = 32
REL_MAX_DIST = 128
PEER_HEADS = 8
PEER_NKEYS = 128
PEER_HALF = 128
PEER_TOPK = 16
PEER_EXPERTS = PEER_NKEYS * PEER_NKEYS
DN_ALPHA = 2.0 ** 0.25
LN_EPS = 1e-5
PAGE = 128
LANE = 128
ROW_ALIGN = 2560
VMEM_LIMIT = 56 * 1024 * 1024
INT_MIN = -2 ** 31
NEG_BIG = -1e30
NEG_INF_KEY = int(np.int32(np.uint32(0xFF800000) ^ np.uint32(0x7FFFFFFF)))


def _params(*sem):
    return pltpu.CompilerParams(dimension_semantics=sem, vmem_limit_bytes=VMEM_LIMIT)


def _dot(a, b):
    return jnp.dot(a, b, preferred_element_type=F32)


def _dot_nt(a, b):
    return lax.dot_general(a, b, (((1,), (1,)), ((), ())), preferred_element_type=F32)


def _split2(x):
    hi = x.astype(BF16)
    lo = (x - hi.astype(F32)).astype(BF16)
    return hi, lo


def _split3(x):
    hi = x.astype(BF16)
    r1 = x - hi.astype(F32)
    mid = r1.astype(BF16)
    lo = (r1 - mid.astype(F32)).astype(BF16)
    return hi, mid, lo


def _dot_hp(a, b):
    ah, al = _split2(a)
    bh, bl = _split2(b)
    return _dot(ah, bh) + (_dot(ah, bl) + _dot(al, bh))


def _dot_nt_hp(a, b):
    ah, al = _split2(a)
    bh, bl = _split2(b)
    return _dot_nt(ah, bh) + (_dot_nt(ah, bl) + _dot_nt(al, bh))


def _segsum(x, ones_bd):
    hi, mid, lo = _split3(x)
    return _dot(hi, ones_bd) + (_dot(mid, ones_bd) + _dot(lo, ones_bd))


def _float_key(x):
    bits = pltpu.bitcast(x, I32)
    return bits ^ (lax.shift_right_arithmetic(bits, 31) & 0x7FFFFFFF)


def _kth_largest_key(count_ge, n_sel, shape):
    def body(it, ans_u):
        bit = lax.shift_left(jnp.int32(1), 31 - it)
        cand_u = ans_u | bit
        cnt = count_ge(cand_u ^ INT_MIN)
        return jnp.where(cnt >= n_sel, cand_u, ans_u)

    ans_u = lax.fori_loop(0, 32, body, jnp.zeros(shape, I32))
    return ans_u ^ INT_MIN


def _tie_cutoff(count_eq_below, need, nbits, shape):
    def body(it, cut):
        cand = cut | lax.shift_left(jnp.int32(1), nbits - 1 - it)
        return jnp.where(count_eq_below(cand) <= need, cand, cut)

    return lax.fori_loop(0, nbits, body, jnp.zeros(shape, I32))


def _ln(x, g, b):
    mu = jnp.mean(x, axis=-1, keepdims=True)
    xc = x - mu
    var = jnp.mean(xc * xc, axis=-1, keepdims=True)
    return xc * lax.rsqrt(var + LN_EPS) * g + b


def _ln_in_kernel(x_ref, g_ref, b_ref, xn_ref, xb_ref):
    y = _ln(x_ref[...], g_ref[...], b_ref[...])
    xn_ref[...] = y
    xb_ref[...] = y.astype(BF16)


def _ln_in(x, g, b, tm):
    rows = x.shape[0]
    row = pl.BlockSpec((tm, D_MODEL), lambda i: (i, 0))
    vec = pl.BlockSpec((1, D_MODEL), lambda i: (0, 0))
    return pl.pallas_call(
        _ln_in_kernel,
        grid=(rows // tm,),
        in_specs=[row, vec, vec],
        out_specs=[row, row],
        out_shape=[jax.ShapeDtypeStruct((rows, D_MODEL), F32), jax.ShapeDtypeStruct((rows, D_MODEL), BF16)],
        compiler_params=_params("parallel"),
    )(x, g.reshape(1, -1), b.reshape(1, -1))


def _ln_out_kernel(x_ref, pt_ref, g_ref, b_ref, op_ref, ot_ref, *, n_prompt_tiles):
    y = _ln(DN_ALPHA * x_ref[...] + pt_ref[...].T, g_ref[...], b_ref[...])

    @pl.when(pl.program_id(0) < n_prompt_tiles)
    def _():
        op_ref[...] = y

    @pl.when(pl.program_id(0) >= n_prompt_tiles)
    def _():
        ot_ref[...] = y


def _ln_out(x, pt, g, b, tm, rows_p):
    rows = x.shape[0]
    n_prompt_tiles = rows_p // tm
    row = pl.BlockSpec((tm, D_MODEL), lambda i: (i, 0))
    vec = pl.BlockSpec((1, D_MODEL), lambda i: (0, 0))
    return pl.pallas_call(
        functools.partial(_ln_out_kernel, n_prompt_tiles=n_prompt_tiles),
        grid=(rows // tm,),
        in_specs=[row, pl.BlockSpec((D_MODEL, tm), lambda i: (0, i)), vec, vec],
        out_specs=[pl.BlockSpec((tm, D_MODEL), lambda i: (jnp.minimum(i, n_prompt_tiles - 1), 0)),
                   pl.BlockSpec((tm, D_MODEL), lambda i: (jnp.maximum(i - n_prompt_tiles, 0), 0))],
        out_shape=[jax.ShapeDtypeStruct((rows_p, D_MODEL), F32), jax.ShapeDtypeStruct((rows - rows_p, D_MODEL), F32)],
        compiler_params=_params("arbitrary"),
    )(x, pt, g.reshape(1, -1), b.reshape(1, -1))


def _mm_kernel(x_ref, w_ref, o_ref):
    o_ref[...] = _dot(x_ref[...], w_ref[...])


def _matmul(xb, w, tm, tn):
    m, k = xb.shape
    n = w.shape[1]
    return pl.pallas_call(
        _mm_kernel,
        grid=(m // tm, n // tn),
        in_specs=[pl.BlockSpec((tm, k), lambda i, j: (i, 0)), pl.BlockSpec((k, tn), lambda i, j: (0, j))],
        out_specs=pl.BlockSpec((tm, tn), lambda i, j: (i, j)),
        out_shape=jax.ShapeDtypeStruct((m, n), F32),
        compiler_params=_params("parallel", "arbitrary"),
    )(xb, w)


def _rwkv_pre_kernel(cur_ref, prev8_ref, init_ref, mu_ref, w0_ref, a0_ref, kk_ref, ka_ref, rk_ref,
                     wup_ref, aup_ref, gup_ref, ones_ref, *out_refs, n_prompt_tiles, tiles_per_batch, dec_seq):
    i = pl.program_id(0)
    cur = cur_ref[...]
    row = lax.broadcasted_iota(I32, cur.shape, 0)
    prev = jnp.where(row == 0, jnp.broadcast_to(prev8_ref[7:8, :], cur.shape), pltpu.roll(cur, 1, axis=0))
    batch_start = ((i % tiles_per_batch) == 0).astype(I32)
    first_prompt = jnp.where(row == 0, batch_start, 0)
    first_sample = jnp.where(row % dec_seq == 0, 1, 0)
    first = jnp.where(i < n_prompt_tiles, first_prompt, first_sample)
    prev = jnp.where(first > 0, init_ref[...], prev)

    xm = cur + (prev - cur) * mu_ref[...]
    r = xm[:, 0:RWKV_WIDTH]
    k = xm[:, RWKV_WIDTH:2 * RWKV_WIDTH]
    v = xm[:, 2 * RWKV_WIDTH:3 * RWKV_WIDTH]
    wa = xm[:, 3 * RWKV_WIDTH:3 * RWKV_WIDTH + LORA_WA]
    gl = xm[:, 3 * RWKV_WIDTH + LORA_WA:]
    ones_bd = ones_ref[...]

    nz = -(w0_ref[...] + _dot_hp(jnp.tanh(wa), wup_ref[...]))
    softplus = jnp.maximum(nz, 0.0) + jnp.log1p(jnp.exp(-jnp.abs(nz)))
    decay = jnp.exp(-jnp.exp(-softplus - 0.5))
    a = jax.nn.sigmoid(a0_ref[...] + _dot_hp(wa, aup_ref[...]))
    g = _dot_hp(jax.nn.sigmoid(gl), gup_ref[...])
    kn = k * kk_ref[...]
    kn = kn / jnp.maximum(jnp.sqrt(_segsum(kn * kn, ones_bd)), 1e-12)
    k_h = k * (1.0 + (a - 1.0) * ka_ref[...])
    vals = (r, decay, k_h, v, kn, kn * a, g, _segsum(r * k_h * rk_ref[...], ones_bd) * v)
    n_out = len(vals)

    @pl.when(i < n_prompt_tiles)
    def _():
        for o_ref, val in zip(out_refs[:n_out], vals):
            o_ref[...] = val

    @pl.when(i >= n_prompt_tiles)
    def _():
        for o_ref, val in zip(out_refs[n_out:], vals):
            o_ref[...] = val


def _rwkv_pre(feat, init, mu, w0, a0, k_k, k_a, r_k, wup, aup, gup, ones_bd, n_prompt_tiles, tiles_per_batch, dec_seq):
    tm = LANE
    n_tiles = feat.shape[0] // tm
    vec = lambda n: pl.BlockSpec((1, n), lambda i: (0, 0))
    full = lambda a: pl.BlockSpec(a.shape, lambda i: (0, 0))
    out_p = pl.BlockSpec((tm, RWKV_WIDTH), lambda i: (jnp.minimum(i, n_prompt_tiles - 1), 0))
    out_t = pl.BlockSpec((tm, RWKV_WIDTH), lambda i: (jnp.maximum(i - n_prompt_tiles, 0), 0))
    shape_p = jax.ShapeDtypeStruct((n_prompt_tiles * tm, RWKV_WIDTH), F32)
    shape_t = jax.ShapeDtypeStruct(((n_tiles - n_prompt_tiles) * tm, RWKV_WIDTH), F32)
    kern = functools.partial(_rwkv_pre_kernel, n_prompt_tiles=n_prompt_tiles, tiles_per_batch=tiles_per_batch, dec_seq=dec_seq)
    outs = pl.pallas_call(
        kern,
        grid=(n_tiles,),
        in_specs=[
            pl.BlockSpec((tm, RWKV_PAD), lambda i: (i, 0)),
            pl.BlockSpec((8, RWKV_PAD), lambda i: (jnp.maximum(i * (tm // 8) - 1, 0), 0)),
            pl.BlockSpec((tm, RWKV_PAD), lambda i: (jnp.where(i == n_prompt_tiles, 1, 0), 0)),
            vec(RWKV_PAD), vec(RWKV_WIDTH), vec(RWKV_WIDTH), vec(RWKV_WIDTH), vec(RWKV_WIDTH), vec(RWKV_WIDTH),
            full(wup), full(aup), full(gup), full(ones_bd),
        ],
        out_specs=[out_p] * 8 + [out_t] * 8,
        out_shape=[shape_p] * 8 + [shape_t] * 8,
        compiler_params=_params("arbitrary"),
    )(feat, feat, init, mu, w0, a0, k_k, k_a, r_k, wup, aup, gup, ones_bd)
    return outs[:8], outs[8:]


def _rwkv_scan_kernel(r_ref, w_ref, k_ref, v_ref, kn_ref, b_ref, s0_ref, y_ref, s_ref, *, n_chunks):
    c_id = pl.program_id(1)

    @pl.when(c_id == 0)
    def _():
        s_ref[...] = s0_ref[...]

    @pl.when(c_id >= n_chunks)
    def _():
        y_ref[...] = jnp.zeros(y_ref.shape, F32)

    def step(t, carry):
        kn_t = kn_ref[t]
        w_t = w_ref[t]
        b_t = b_ref[t]
        k_t = k_ref[t]
        r_t = r_ref[t]

        def value_row(vi, c):
            s_v = s_ref[vi]
            s_kn = jnp.sum(s_v * kn_t, axis=0, keepdims=True)
            s_new = s_v * w_t - s_kn * b_t + v_ref[t, pl.ds(vi, 1), :] * k_t
            s_ref[vi] = s_new
            y_ref[t, pl.ds(vi, 1), :] = jnp.sum(s_new * r_t, axis=0, keepdims=True)
            return c

        return lax.fori_loop(0, HEAD_DIM, value_row, carry, unroll=16)

    @pl.when(c_id < n_chunks)
    def _():
        lax.fori_loop(0, r_ref.shape[0], step, 0)


def _rwkv_scan(r, w, k, v, kn, b, s0, tc, n_steps):
    steps, _, pairs = r.shape
    seq = pl.BlockSpec((tc, HEAD_DIM, LANE), lambda p, c: (c, 0, p))
    state = pl.BlockSpec((HEAD_DIM, HEAD_DIM, LANE), lambda p, c: (0, 0, p))
    return pl.pallas_call(
        functools.partial(_rwkv_scan_kernel, n_chunks=n_steps // tc),
        grid=(pairs // LANE, steps // tc),
        in_specs=[seq] * 6 + [state],
        out_specs=[seq, state],
        out_shape=[jax.ShapeDtypeStruct(r.shape, F32), jax.ShapeDtypeStruct(s0.shape, F32)],
        compiler_params=_params("parallel", "arbitrary"),
    )(r, w, k, v, kn, b, s0)


def _prompt_attn_kernel(qq_ref, kvi_ref, wi_ref, bias_ref, y_ref,
                        vt_ref, kb_ref, kib_ref, qn_ref, key_ref, sel_ref, cut_ref, acc_ref, *, n_blocks, n_sel):
    i = pl.program_id(1)
    n_kb = i + 1
    gq = GROUP * LANE

    @pl.when(i == 0)
    def _():
        t_pad = n_blocks * LANE
        for j in range(n_blocks):
            vt_ref[j] = kvi_ref[j * LANE:(j + 1) * LANE, KV_WIDTH:2 * KV_WIDTH].T.astype(BF16)
        for n in range(KV_HEADS):
            kb_ref[n, 0:t_pad, :] = kvi_ref[:, n * HEAD_DIM:(n + 1) * HEAD_DIM].astype(BF16)
        kib_ref[...] = kvi_ref[:, 2 * KV_WIDTH:2 * KV_WIDTH + IDX_DIM].astype(BF16)
        for j in range(n_blocks, vt_ref.shape[0]):
            vt_ref[j] = jnp.zeros(vt_ref.shape[1:], BF16)
            kb_ref[:, j * LANE:(j + 1) * LANE, :] = jnp.zeros((KV_HEADS, LANE, HEAD_DIM), BF16)

    kpos0 = lax.broadcasted_iota(I32, (LANE, LANE), 0)
    qpos = i * LANE + lax.broadcasted_iota(I32, (LANE, LANE), 1)

    qi_all = jnp.concatenate(
        [qq_ref[:, ATTN_WIDTH + h * IDX_DIM:ATTN_WIDTH + (h + 1) * IDX_DIM] for h in range(IDX_HEADS)], axis=0).astype(BF16)
    w_t = wi_ref[...].T
    w_flat = jnp.concatenate([w_t[h:h + 1, :] for h in range(IDX_HEADS)], axis=1)

    def score_block(j, c):
        r0 = pl.multiple_of(j * LANE, LANE)
        ki = kib_ref[pl.ds(r0, LANE), :]
        s = jnp.maximum(_dot_nt(ki, qi_all), 0.0) * w_flat
        acc = s[:, 0:LANE]
        for h in range(1, IDX_HEADS):
            acc = acc + s[:, h * LANE:(h + 1) * LANE]
        acc = jnp.where(acc == 0.0, 0.0, acc)
        acc = jnp.where(kpos0 + r0 <= qpos, acc, -jnp.inf)
        key_ref[pl.ds(r0, LANE), :] = _float_key(acc)
        return c

    lax.fori_loop(0, n_kb, score_block, 0)

    row1 = (1, LANE)

    def count(flag):
        def body(j, acc):
            r0 = pl.multiple_of(j * LANE, LANE)
            return acc + flag(key_ref[pl.ds(r0, LANE), :], r0)

        return jnp.sum(lax.fori_loop(0, n_kb, body, jnp.zeros((LANE, LANE), F32)), axis=0, keepdims=True)

    thr = _kth_largest_key(lambda t: count(lambda k, r0: jnp.where(k >= t, 1.0, 0.0)), n_sel, row1)
    need = n_sel - count(lambda k, r0: jnp.where(k > thr, 1.0, 0.0))
    n_tied = count(lambda k, r0: jnp.where(k == thr, 1.0, 0.0))
    nbits = (n_blocks * LANE).bit_length()
    cut_ref[...] = jnp.full(cut_ref.shape, 2 ** nbits, I32)

    @pl.when(jnp.max(n_tied - need) > 0.0)
    def _():
        cut = _tie_cutoff(
            lambda c: count(lambda k, r0: jnp.where(k == thr, jnp.where(kpos0 + r0 < c, 1.0, 0.0), 0.0)), need, nbits, row1)
        cut_ref[...] = jnp.broadcast_to(cut, cut_ref.shape)

    cut = cut_ref[0:1, :]

    def select_block(j, c):
        r0 = pl.multiple_of(j * LANE, LANE)
        k = key_ref[pl.ds(r0, LANE), :]
        kpos = kpos0 + r0
        chosen = jnp.where(k > thr, 1.0, jnp.where(k == thr, jnp.where(kpos < cut, 1.0, 0.0), 0.0))
        sel_ref[pl.ds(r0, LANE), :] = jnp.where(kpos <= qpos, chosen, 0.0)
        return c

    lax.fori_loop(0, n_kb, select_block, 0)

    @pl.when(n_kb % 2 == 1)
    def _():
        sel_ref[pl.ds(pl.multiple_of(n_kb * LANE, LANE), LANE), :] = jnp.zeros((LANE, LANE), F32)

    for n in range(KV_HEADS):
        q_n = jnp.concatenate(
            [qq_ref[:, (GROUP * n + g) * HEAD_DIM:(GROUP * n + g + 1) * HEAD_DIM] for g in range(GROUP)], axis=0)
        qn_ref[n] = (q_n * HEAD_DIM ** -0.5).astype(BF16)
    acc_ref[...] = jnp.zeros(acc_ref.shape, F32)
    pair = 2 * LANE

    def key_blocks(jj, carry):
        ms, ls = carry
        r0 = pl.multiple_of(jj * pair, pair)
        mask = sel_ref[pl.ds(r0, pair), :] > 0.5
        near = (jnp.clip(i - 2 * jj, 0, 2), jnp.clip(i - 2 * jj - 1, 0, 2))
        new_ms, new_ls = [], []
        for n in range(KV_HEADS):
            bias = jnp.concatenate([bias_ref[n, near[0]], bias_ref[n, near[1]]], axis=0)
            s = _dot_nt(kb_ref[n, pl.ds(r0, pair), :], qn_ref[n]) + bias
            s = jnp.concatenate([jnp.where(mask, s[:, g * LANE:(g + 1) * LANE], NEG_BIG) for g in range(GROUP)], axis=1)
            m_new = jnp.maximum(ms[n], jnp.max(s, axis=0, keepdims=True))
            alpha = jnp.exp(ms[n] - m_new)
            p = jnp.exp(s - m_new)
            new_ls.append(alpha * ls[n] + jnp.sum(p, axis=0, keepdims=True))
            new_ms.append(m_new)
            rows = slice(n * HEAD_DIM, (n + 1) * HEAD_DIM)
            vt = jnp.concatenate([vt_ref[2 * jj, rows, :], vt_ref[2 * jj + 1, rows, :]], axis=1)
            acc_ref[n] = alpha * acc_ref[n] + _dot(vt, p.astype(BF16))
        return tuple(new_ms), tuple(new_ls)

    init = (tuple(jnp.full((1, gq), NEG_BIG, F32) for _ in range(KV_HEADS)),
            tuple(jnp.zeros((1, gq), F32) for _ in range(KV_HEADS)))
    _, ls = lax.fori_loop(0, (n_kb + 1) // 2, key_blocks, init)
    outs = []
    for n in range(KV_HEADS):
        o = acc_ref[n] / ls[n]
        outs += [o[:, g * LANE:(g + 1) * LANE].T for g in range(GROUP)]
    y_ref[...] = jnp.concatenate(outs, axis=1)


def _prompt_attn(qq, kvi, bias_tiles, n_batch, n_blocks, n_sel):
    t_pad = n_blocks * LANE
    n_even = n_blocks + n_blocks % 2
    kern = functools.partial(_prompt_attn_kernel, n_blocks=n_blocks, n_sel=n_sel)
    return pl.pallas_call(
        kern,
        grid=(n_batch, n_blocks),
        in_specs=[
            pl.BlockSpec((LANE, 2 * ATTN_WIDTH), lambda b, i: (b * n_blocks + i, 0)),
            pl.BlockSpec((t_pad, kvi.shape[1]), lambda b, i: (b, 0)),
            pl.BlockSpec((LANE, LANE), lambda b, i: (b * n_blocks + i, (2 * KV_WIDTH + IDX_DIM) // LANE)),
            pl.BlockSpec(bias_tiles.shape, lambda b, i: (0, 0, 0, 0)),
        ],
        out_specs=pl.BlockSpec((LANE, ATTN_WIDTH), lambda b, i: (b * n_blocks + i, 0)),
        out_shape=jax.ShapeDtypeStruct((n_batch * t_pad, ATTN_WIDTH), F32),
        scratch_shapes=[
            pltpu.VMEM((n_even, KV_WIDTH, LANE), BF16),
            pltpu.VMEM((KV_HEADS, n_even * LANE, HEAD_DIM), BF16),
            pltpu.VMEM((t_pad, IDX_DIM), BF16),
            pltpu.VMEM((KV_HEADS, GROUP * LANE, HEAD_DIM), BF16),
            pltpu.VMEM((t_pad, LANE), I32),
            pltpu.VMEM((n_even * LANE, LANE), F32),
            pltpu.VMEM((8, LANE), I32),
            pltpu.VMEM((KV_HEADS, HEAD_DIM, GROUP * LANE), F32),
        ],
        compiler_params=_params("parallel", "arbitrary"),
    )(qq, kvi, kvi, bias_tiles)


PAGES_PER_STEP = 16
Q_PAD = 8


def _sample_score_kernel(pt_ref, qi_ref, wb_ref, *refs):
    page_refs, out_ref = refs[:PAGES_PER_STEP], refs[PAGES_PER_STEP]
    qi = qi_ref[...].astype(BF16)
    wb = wb_ref[...]
    for u in range(PAGES_PER_STEP):
        s = jnp.maximum(_dot_nt(qi, page_refs[u][...].astype(BF16)), 0.0) * wb
        acc = s[0:Q_PAD]
        for h in range(1, IDX_HEADS):
            acc = acc + s[h * Q_PAD:(h + 1) * Q_PAD]
        out_ref[u] = acc


def _sample_scores(page_table, qi8, wb, cache_kidx):
    n_batch, n_pages = page_table.shape
    page_spec = lambda u: pl.BlockSpec((None, PAGE, IDX_DIM), lambda b, s, pt: (pt[b, s * PAGES_PER_STEP + u], 0, 0))
    per_batch = pl.BlockSpec((None, IDX_HEADS * Q_PAD, IDX_DIM), lambda b, s, pt: (b, 0, 0))
    return pl.pallas_call(
        _sample_score_kernel,
        grid_spec=pltpu.PrefetchScalarGridSpec(
            num_scalar_prefetch=1,
            grid=(n_batch, n_pages // PAGES_PER_STEP),
            in_specs=[per_batch, per_batch] + [page_spec(u) for u in range(PAGES_PER_STEP)],
            out_specs=pl.BlockSpec((None, PAGES_PER_STEP, Q_PAD, PAGE), lambda b, s, pt: (b, s, 0, 0)),
        ),
        out_shape=jax.ShapeDtypeStruct((n_batch, n_pages, Q_PAD, PAGE), F32),
        compiler_params=_params("parallel", "arbitrary"),
    )(page_table, qi8, wb, *([cache_kidx] * PAGES_PER_STEP))


def _sample_select_kernel(sc_ref, qi_ref, wb_ref, kin_ref, sel_ref, key_ref, *, n_pages, dec_seq, n_sel):
    qrow = lax.broadcasted_iota(I32, (Q_PAD, PAGE), 0)
    lane = lax.broadcasted_iota(I32, (Q_PAD, PAGE), 1)
    s = jnp.maximum(_dot_nt(qi_ref[...].astype(BF16), kin_ref[...].astype(BF16)), 0.0) * wb_ref[...]
    acc = s[0:Q_PAD]
    for h in range(1, IDX_HEADS):
        acc = acc + s[h * Q_PAD:(h + 1) * Q_PAD]
    new_valid = jnp.where(lane < dec_seq, jnp.where(lane <= qrow, 1, 0), 0) > 0
    past = sc_ref[...]
    key_ref[0:n_pages] = _float_key(jnp.where(past == 0.0, 0.0, past))
    key_ref[n_pages] = _float_key(jnp.where(new_valid, jnp.where(acc == 0.0, 0.0, acc), -jnp.inf))

    def lane_count(x):
        return jnp.sum(jnp.sum(x, axis=0), axis=1, keepdims=True)

    col1 = (Q_PAD, 1)
    thr = _kth_largest_key(lambda t: lane_count(jnp.where(key_ref[...] >= t, 1.0, 0.0)), n_sel, col1)
    keys = key_ref[...]
    need = n_sel - lane_count(jnp.where(keys > thr, 1.0, 0.0))
    shape3 = (n_pages + 1, Q_PAD, PAGE)
    kidx = lax.broadcasted_iota(I32, shape3, 0) * PAGE + lax.broadcasted_iota(I32, shape3, 2)
    cut = _tie_cutoff(
        lambda c: lane_count(jnp.where(key_ref[...] == thr, jnp.where(kidx < c, 1.0, 0.0), 0.0)),
        need, ((n_pages + 1) * PAGE).bit_length(), col1)
    chosen = jnp.where(keys > thr, 1.0, jnp.where(keys == thr, jnp.where(kidx < cut, 1.0, 0.0), 0.0))
    sel_ref[0:n_pages] = chosen[0:n_pages]
    sel_ref[n_pages] = jnp.where(new_valid, chosen[n_pages], 0.0)


def _sample_select(sc, qi8, wb, ki_new, dec_seq, n_sel):
    n_batch, n_pages = sc.shape[:2]
    kern = functools.partial(_sample_select_kernel, n_pages=n_pages, dec_seq=dec_seq, n_sel=n_sel)
    per_batch = lambda a: pl.BlockSpec((None,) + a.shape[1:], lambda b: (b,) + (0,) * (a.ndim - 1))
    return pl.pallas_call(
        kern,
        grid=(n_batch,),
        in_specs=[per_batch(sc), per_batch(qi8), per_batch(wb), per_batch(ki_new)],
        out_specs=pl.BlockSpec((None, n_pages + 1, Q_PAD, PAGE), lambda b: (b, 0, 0, 0)),
        out_shape=jax.ShapeDtypeStruct((n_batch, n_pages + 1, Q_PAD, PAGE), F32),
        scratch_shapes=[pltpu.VMEM((n_pages + 1, Q_PAD, PAGE), I32)],
        compiler_params=_params("parallel"),
    )(sc, qi8, wb, ki_new)


def _sample_attn_kernel(pt_ref, q_ref, sel_ref, selnew_ref, knew_ref, vnew_ref, bias_ref, *refs, n_steps):
    k_refs = refs[:PAGES_PER_STEP]
    v_refs = refs[PAGES_PER_STEP:2 * PAGES_PER_STEP]
    o_ref, m_ref, l_ref, acc_ref = refs[2 * PAGES_PER_STEP:]
    s_id = pl.program_id(1)
    rows = KV_HEADS * GROUP * Q_PAD
    per_kv = GROUP * Q_PAD
    q = (q_ref[...] * HEAD_DIM ** -0.5).astype(BF16)

    @pl.when(s_id == 0)
    def _():
        m_ref[...] = jnp.full(m_ref.shape, NEG_BIG, F32)
        l_ref[...] = jnp.zeros(l_ref.shape, F32)
        acc_ref[...] = jnp.zeros(acc_ref.shape, F32)

    def attend(k_pages, v_pages, sel_pages, bias_pages):
        s_blocks, sel_blocks = [], []
        for kp, sp, bp in zip(k_pages, sel_pages, bias_pages):
            kb = kp.astype(BF16)
            s = jnp.concatenate(
                [_dot(q[n * per_kv:(n + 1) * per_kv], kb[n * HEAD_DIM:(n + 1) * HEAD_DIM]) for n in range(KV_HEADS)],
                axis=0) + bp
            s_blocks.append(s)
            sel_blocks.append(jnp.concatenate([sp] * (KV_HEADS * GROUP), axis=0) > 0.5)
        s = jnp.concatenate(s_blocks, axis=1)
        sel = jnp.concatenate(sel_blocks, axis=1)
        s = jnp.where(sel, s, NEG_BIG)
        m_old = m_ref[...]
        m_new = jnp.maximum(m_old, jnp.max(s, axis=1, keepdims=True))
        alpha = jnp.exp(m_old - m_new)
        p = jnp.where(sel, jnp.exp(s - m_new), 0.0)
        l_ref[...] = alpha * l_ref[...] + jnp.sum(p, axis=1, keepdims=True)
        m_ref[...] = m_new
        pb = p.astype(BF16)
        pv = None
        for u, vp in enumerate(v_pages):
            vb = vp.astype(BF16)
            pu = pb[:, u * PAGE:(u + 1) * PAGE]
            part = jnp.concatenate(
                [_dot_nt(pu[n * per_kv:(n + 1) * per_kv], vb[n * HEAD_DIM:(n + 1) * HEAD_DIM]) for n in range(KV_HEADS)], axis=0)
            pv = part if pv is None else pv + part
        acc_ref[...] = alpha * acc_ref[...] + pv

    @pl.when(s_id < n_steps)
    def _():
        far, near = bias_ref[0], bias_ref[1]
        biases = [far] * PAGES_PER_STEP
        last = s_id == n_steps - 1
        biases[-1] = jnp.where(last, near, far)
        attend([r[...] for r in k_refs], [r[...] for r in v_refs], [sel_ref[u] for u in range(PAGES_PER_STEP)], biases)

    @pl.when(s_id == n_steps)
    def _():
        attend([knew_ref[...]], [vnew_ref[...]], [selnew_ref[...]], [bias_ref[2]])
        o_ref[...] = acc_ref[...] / l_ref[...]


def _sample_attn(page_table, q8, sel, k_new, v_new, bias_tiles, cache_k, cache_v):
    n_batch, n_pages = page_table.shape
    n_steps = n_pages // PAGES_PER_STEP
    rows = KV_HEADS * GROUP * Q_PAD
    kern = functools.partial(_sample_attn_kernel, n_steps=n_steps)

    def page_spec(u):
        return pl.BlockSpec((None, KV_WIDTH, PAGE),
                            lambda b, s, pt: (pt[b, jnp.minimum(s, n_steps - 1) * PAGES_PER_STEP + u], 0, 0))

    per_batch = lambda a: pl.BlockSpec((None,) + a.shape[1:], lambda b, s, pt: (b,) + (0,) * (a.ndim - 1))
    return pl.pallas_call(
        kern,
        grid_spec=pltpu.PrefetchScalarGridSpec(
            num_scalar_prefetch=1,
            grid=(n_batch, n_steps + 1),
            in_specs=[
                per_batch(q8),
                pl.BlockSpec((None, PAGES_PER_STEP, Q_PAD, PAGE), lambda b, s, pt: (b, jnp.minimum(s, n_steps - 1), 0, 0)),
                pl.BlockSpec((None, None, Q_PAD, PAGE), lambda b, s, pt: (b, n_pages, 0, 0)),
                per_batch(k_new), per_batch(v_new),
                pl.BlockSpec(bias_tiles.shape, lambda b, s, pt: (0, 0, 0)),
            ] + [page_spec(u) for u in range(PAGES_PER_STEP)] * 2,
            out_specs=pl.BlockSpec((None, rows, HEAD_DIM), lambda b, s, pt: (b, 0, 0)),
            scratch_shapes=[pltpu.VMEM((rows, 1), F32), pltpu.VMEM((rows, 1), F32), pltpu.VMEM((rows, HEAD_DIM), F32)],
        ),
        out_shape=jax.ShapeDtypeStruct((n_batch, rows, HEAD_DIM), F32),
        compiler_params=_params("parallel", "arbitrary"),
    )(page_table, q8, sel, sel, k_new, v_new, bias_tiles, *([cache_k] * PAGES_PER_STEP), *([cache_v] * PAGES_PER_STEP))


def _mix_kernel(ysp_ref, yst_ref, bonusp_ref, bonust_ref, gp_ref, gt_ref, yap_ref, yat_ref,
                xn_ref, gng_ref, gnb_ref, ones_ref, wor_ref, woa_ref, lg_ref, lb_ref,
                x1_ref, x1b_ref, x1t_ref, *, n_prompt_tiles):
    is_prompt = pl.program_id(0) < n_prompt_tiles
    pick = lambda p_ref, t_ref: jnp.where(is_prompt, p_ref[...], t_ref[...])
    ones_bd = ones_ref[...]
    ys = pick(ysp_ref, yst_ref)
    inv = 1.0 / HEAD_DIM
    yc = ys - _segsum(ys, ones_bd) * inv
    var = _segsum(yc * yc, ones_bd) * inv
    yr = (yc * lax.rsqrt(var + GN_EPS) * gng_ref[...] + gnb_ref[...] + pick(bonusp_ref, bonust_ref)) * pick(gp_ref, gt_ref)
    mix = _dot(yr.astype(BF16), wor_ref[...]) + _dot(pick(yap_ref, yat_ref).astype(BF16), woa_ref[...])
    x1 = _ln(DN_ALPHA * xn_ref[...] + mix, lg_ref[...], lb_ref[...])
    x1_ref[...] = x1
    x1b_ref[...] = x1.astype(BF16)
    x1t_ref[...] = x1.T.astype(BF16)


def _mix(ys, bonus, g, ya, xn, gn_g, gn_b, ones_bd, wo_r, wo_a, ln_g, ln_b, tm):
    rows = xn.shape[0]
    n_prompt_tiles = ys[0].shape[0] // tm
    half_p = pl.BlockSpec((tm, RWKV_WIDTH), lambda i: (jnp.minimum(i, n_prompt_tiles - 1), 0))
    half_t = pl.BlockSpec((tm, RWKV_WIDTH), lambda i: (jnp.maximum(i - n_prompt_tiles, 0), 0))
    row = pl.BlockSpec((tm, D_MODEL), lambda i: (i, 0))
    vec = lambda n: pl.BlockSpec((1, n), lambda i: (0, 0))
    full = lambda a: pl.BlockSpec(a.shape, lambda i: (0, 0))
    return pl.pallas_call(
        functools.partial(_mix_kernel, n_prompt_tiles=n_prompt_tiles),
        grid=(rows // tm,),
        in_specs=[half_p, half_t] * 4 + [row, vec(RWKV_WIDTH), vec(RWKV_WIDTH), full(ones_bd), full(wo_r), full(wo_a),
                                         vec(D_MODEL), vec(D_MODEL)],
        out_specs=[row, row, pl.BlockSpec((D_MODEL, tm), lambda i: (0, i))],
        out_shape=[jax.ShapeDtypeStruct((rows, D_MODEL), F32), jax.ShapeDtypeStruct((rows, D_MODEL), BF16),
                   jax.ShapeDtypeStruct((D_MODEL, rows), BF16)],
        compiler_params=_params("parallel"),
    )(*ys, *bonus, *g, *ya, xn, gn_g, gn_b, ones_bd, wo_r, wo_a, ln_g, ln_b)


CAND_PAIRS = [(c, d) for c in range(PEER_TOPK) for d in range(PEER_TOPK) if (c + 1) * (d + 1) <= PEER_TOPK]


def _top_rows(x, n):
    rows = []
    for _ in range(n):
        m = jnp.max(x, axis=0, keepdims=True)
        rows.append(m)
        x = jnp.where(x == m, -jnp.inf, x)
    return rows


def _peer_route_kernel(x_ref, wq_ref, sub_ref, s1_ref, s2_ref, e2_ref, thr_ref, m1_ref, zinv_ref):
    q = _dot(x_ref[...], wq_ref[...])
    for h in range(PEER_HEADS):
        base = h * 2 * PEER_HALF
        s1 = _dot_nt_hp(sub_ref[h, 0], q[:, base:base + PEER_HALF])
        s2 = _dot_nt_hp(sub_ref[h, 1], q[:, base + PEER_HALF:base + 2 * PEER_HALF])
        top1 = _top_rows(s1, PEER_TOPK)
        top2 = _top_rows(s2, PEER_TOPK)
        cand = jnp.concatenate([top1[c] + top2[d] for c, d in CAND_PAIRS]
                               + [jnp.full_like(top1[0], -jnp.inf)] * (-len(CAND_PAIRS) % 8), axis=0)
        best = _top_rows(cand, PEER_TOPK)
        thr = best[-1]
        m = top1[0] + top2[0]
        z = jnp.sum(jnp.where(cand >= thr, jnp.exp(cand - m), 0.0), axis=0, keepdims=True)
        s1_ref[h] = s1
        s2_ref[h] = s2
        e2_ref[h] = jnp.exp(s2 - top2[0])
        thr_ref[h:h + 1, :] = thr
        m1_ref[h:h + 1, :] = top1[0]
        zinv_ref[h:h + 1, :] = 1.0 / z


def _peer_route(x1b, wq, subkeys):
    rows = x1b.shape[0]
    tm = LANE
    stat = pl.BlockSpec((PEER_HEADS, tm), lambda i: (0, i))
    big = pl.BlockSpec((PEER_HEADS, PEER_NKEYS, tm), lambda i: (0, 0, i))
    stat_shape = jax.ShapeDtypeStruct((PEER_HEADS, rows), F32)
    big_shape = jax.ShapeDtypeStruct((PEER_HEADS, PEER_NKEYS, rows), F32)
    return pl.pallas_call(
        _peer_route_kernel,
        grid=(rows // tm,),
        in_specs=[pl.BlockSpec((tm, D_MODEL), lambda i: (i, 0)),
                  pl.BlockSpec(wq.shape, lambda i: (0, 0)),
                  pl.BlockSpec(subkeys.shape, lambda i: (0, 0, 0, 0))],
        out_specs=[big, big, big, stat, stat, stat],
        out_shape=[big_shape, big_shape, big_shape, stat_shape, stat_shape, stat_shape],
        compiler_params=_params("parallel"),
    )(x1b, wq, subkeys)


EXPERT_ROWS = 8
MXU_DEPTH = 256


def _peer_dense_kernel(xt_ref, u_ref, vt_ref, s1_ref, s2_ref, e2_ref, thr_ref, m1_ref, zinv_ref, o_ref, *scratch):
    n_sub = EXPERT_ROWS * PEER_NKEYS // MXU_DEPTH
    ht_refs, g_refs, a_refs = scratch[:n_sub], scratch[n_sub:2 * n_sub], scratch[2 * n_sub:]
    j = pl.program_id(1)
    tm = xt_ref.shape[1]

    @pl.when(j == 0)
    def _():
        o_ref[...] = jnp.zeros(o_ref.shape, F32)

    per = MXU_DEPTH // PEER_NKEYS
    half = PEER_NKEYS // 2

    def gate_rows(p):
        out = []
        for e in range(p * per, (p + 1) * per):
            s1_i = [s1_ref[h, e:e + 1, :] for h in range(PEER_HEADS)]
            out.append((s1_i, [jnp.exp(s1_i[h] - m1_ref[h:h + 1, :]) * zinv_ref[h:h + 1, :] for h in range(PEER_HEADS)]))
        return out

    def gate_block(p, rows_p, c, jh):
        cols = slice(c, c + LANE)
        jr = slice(jh * half, (jh + 1) * half)
        acc = [None] * per
        for h in range(PEER_HEADS):
            s2, e2, thr = s2_ref[h, jr, cols], e2_ref[h, jr, cols], thr_ref[h:h + 1, cols]
            for e in range(per):
                s1_i, f_i = rows_p[e]
                g = jnp.where(s1_i[h][:, cols] + s2 >= thr, e2 * f_i[h][:, cols], 0.0)
                acc[e] = g if acc[e] is None else acc[e] + g
        for e in range(per):
            g_refs[p][e * PEER_NKEYS + jh * half:e * PEER_NKEYS + (jh + 1) * half, cols] = acc[e]

    def gate_blocks(p):
        rows_p = gate_rows(p)
        return [functools.partial(gate_block, p, rows_p, c, jh) for c in range(0, tm, LANE) for jh in range(2)]

    def up_piece(p, k):
        r = p * MXU_DEPTH + k * PEER_NKEYS
        ht_refs[p][k * PEER_NKEYS:(k + 1) * PEER_NKEYS, :] = _dot(u_ref[r:r + PEER_NKEYS, :], xt_ref[...])

    def act(p):
        he = ht_refs[p][...]
        a_refs[p][...] = (0.5 * he * (1.0 + lax.erf(he * (2.0 ** -0.5))) * g_refs[p][...]).astype(BF16)

    down_rows = 512

    def down_piece(p, m):
        r = p * MXU_DEPTH
        rows = slice(m * down_rows, (m + 1) * down_rows)
        o_ref[rows, :] += _dot(vt_ref[rows, r:r + MXU_DEPTH], a_refs[p][...])

    def run(pieces):
        for piece in pieces:
            piece()

    ups = lambda p: [functools.partial(up_piece, p, k) for k in range(per)]
    downs = lambda p: [functools.partial(down_piece, p, m) for m in range(D_MODEL // down_rows)]
    run(gate_blocks(0) + ups(0))
    for p in range(n_sub):
        if p + 1 < n_sub:
            run(gate_blocks(p + 1) + ups(p + 1))
        act(p)
        run(downs(p))


def _peer_dense(x1t, u, vt, s1, s2, e2, thr, m1, zinv, tm):
    rows = x1t.shape[1]
    eb = EXPERT_ROWS * PEER_NKEYS
    stat = pl.BlockSpec((PEER_HEADS, tm), lambda i, j: (0, i))
    big = pl.BlockSpec((PEER_HEADS, PEER_NKEYS, tm), lambda i, j: (0, 0, i))
    return pl.pallas_call(
        _peer_dense_kernel,
        grid=(rows // tm, PEER_EXPERTS // eb),
        in_specs=[pl.BlockSpec((D_MODEL, tm), lambda i, j: (0, i)),
                  pl.BlockSpec((eb, D_MODEL), lambda i, j: (j, 0)),
                  pl.BlockSpec((D_MODEL, eb), lambda i, j: (0, j)),
                  pl.BlockSpec((PEER_HEADS, EXPERT_ROWS, tm), lambda i, j: (0, j, i)),
                  big, big, stat, stat, stat],
        out_specs=pl.BlockSpec((D_MODEL, tm), lambda i, j: (0, i)),
        out_shape=jax.ShapeDtypeStruct((D_MODEL, rows), F32),
        scratch_shapes=([pltpu.VMEM((MXU_DEPTH, tm), F32)] * (2 * eb // MXU_DEPTH)
                        + [pltpu.VMEM((MXU_DEPTH, tm), BF16)] * (eb // MXU_DEPTH)),
        compiler_params=_params("parallel", "arbitrary"),
    )(x1t, u, vt, s1, s2, e2, thr, m1, zinv)


def _rel_buckets(dist):
    max_exact = REL_BUCKETS // 2
    d = np.maximum(dist, 0)
    ratio = np.log(np.maximum(d, 1).astype(np.float32) / np.float32(max_exact)) / np.float32(math.log(REL_MAX_DIST / max_exact))
    log_b = max_exact + (ratio * np.float32(REL_BUCKETS - max_exact)).astype(np.int32)
    return np.where(d < max_exact, d, np.minimum(log_b, REL_BUCKETS - 1)).astype(np.int32)


def _bias_lookup(rel_bias, dist):
    onehot = np.eye(REL_BUCKETS, dtype=np.float32)[_rel_buckets(dist)]
    return jnp.dot(jnp.asarray(onehot), rel_bias.astype(F32), precision=lax.Precision.HIGHEST)


def _prompt_bias_tiles(rel_bias):
    kk = np.arange(LANE)[:, None]
    qq = np.arange(LANE)[None, :]
    tiles = []
    for delta in range(3):
        b = _bias_lookup(rel_bias, delta * LANE + qq - kk)
        b = b.reshape(LANE, LANE, KV_HEADS, GROUP).transpose(2, 0, 3, 1).reshape(KV_HEADS, LANE, GROUP * LANE)
        tiles.append(b)
    return jnp.stack(tiles, axis=1).astype(F32)


def _sample_bias_tiles(rel_bias, past_len, dec_seq):
    q = np.arange(Q_PAD)[:, None]
    off = np.arange(PAGE)[None, :]
    qpos = past_len + np.minimum(q, dec_seq - 1)
    dists = [qpos - 0 * off - (past_len - 2 * PAGE), qpos - (past_len - PAGE + off), qpos - (past_len + np.minimum(off, dec_seq - 1))]
    tiles = []
    for d in dists:
        b = _bias_lookup(rel_bias, d + 0 * off)
        tiles.append(b.transpose(2, 0, 1).reshape(ATTN_HEADS * Q_PAD, PAGE))
    return jnp.stack(tiles).astype(F32)


def kernel(x_prompt, x_sample, cache_k, cache_v, cache_kidx, state_wkv, state_shift, page_table, meta_tokens, ln_in_g, ln_in_b, rel_bias, w_in, mu_shift, w0, w_up, a0, a_up, g_up, k_k, k_a, r_k, gn_g, gn_b, w_o, ln1_g, ln1_b, peer_wq, peer_subkeys, peer_u, peer_v, ln2_g, ln2_b):
    n_batch, seq, _ = x_prompt.shape
    dec_batch, dec_seq, _ = x_sample.shape
    n_pages = page_table.shape[1]
    past_len = n_pages * PAGE
    t_len = seq + N_META
    n_blocks = -(-t_len // LANE)
    t_pad = n_blocks * LANE
    rows_p = n_batch * t_pad
    rows_s = dec_batch * dec_seq
    assert rows_s == LANE and Q_PAD >= dec_seq and n_pages % PAGES_PER_STEP == 0
    assert past_len >= 2 * PAGE + REL_MAX_DIST
    rows = -(-(rows_p + rows_s) // ROW_ALIGN) * ROW_ALIGN
    layer = 0

    meta = jnp.broadcast_to(meta_tokens[None], (n_batch, N_META, D_MODEL))
    xp = jnp.pad(jnp.concatenate([meta, x_prompt], axis=1), ((0, 0), (0, t_pad - t_len), (0, 0)))
    x_all = jnp.concatenate([xp.reshape(rows_p, D_MODEL), x_sample.reshape(rows_s, D_MODEL),
                             jnp.zeros((rows - rows_p - rows_s, D_MODEL), F32)], axis=0)
    xn, xb = _ln_in(x_all, ln_in_g, ln_in_b, 256)

    w = w_in[layer]
    c0 = RWKV_COLS
    w_rwkv = jnp.pad(w[:, :c0], ((0, 0), (0, RWKV_PAD - RWKV_COLS))).astype(BF16)
    w_qq = jnp.concatenate([w[:, c0:c0 + ATTN_WIDTH], w[:, c0 + ATTN_WIDTH + 2 * KV_WIDTH:c0 + 2 * ATTN_WIDTH + 2 * KV_WIDTH]], axis=1).astype(BF16)
    c_ki = c0 + 2 * ATTN_WIDTH + 2 * KV_WIDTH
    w_kvi = jnp.concatenate([w[:, c0 + ATTN_WIDTH:c0 + ATTN_WIDTH + 2 * KV_WIDTH], w[:, c_ki:c_ki + IDX_DIM],
                             jnp.pad(w[:, c_ki + IDX_DIM:], ((0, 0), (0, LANE - IDX_HEADS)))], axis=1).astype(BF16)
    feat = _matmul(xb, w_rwkv, 640, RWKV_PAD // 3)
    qq = _matmul(xb, w_qq, 640, 1024)
    kvi = _matmul(xb, w_kvi, 640, w_kvi.shape[1])

    def prompt_rows(a):
        return a[:rows_p].reshape(n_batch, t_pad, -1)[:, :t_len]

    def sample_rows(a):
        return a[rows_p:rows_p + rows_s].reshape(dec_batch, dec_seq, -1)

    ones_bd = jnp.asarray(np.kron(np.eye(RWKV_HEADS), np.ones((HEAD_DIM, HEAD_DIM))), BF16)
    pad_cols = lambda a: jnp.pad(a, ((0, 0), (0, RWKV_PAD - RWKV_COLS)))
    init = jnp.zeros((dec_batch, dec_seq, RWKV_PAD), F32).at[:, 0].set(pad_cols(state_shift[layer]))
    init = jnp.concatenate([jnp.zeros((LANE, RWKV_PAD), F32), init.reshape(rows_s, RWKV_PAD)], axis=0)
    wup = jnp.pad(w_up[layer], ((0, ICLR_LORA), (0, 0)))
    aup = jnp.pad(a_up[layer], ((DECAY_LORA, 0), (0, 0)))
    gup = jnp.pad(g_up[layer], ((0, GATE_PAD - GATE_LORA), (0, 0)))
    vec = lambda a: a.reshape(1, -1)
    pre = _rwkv_pre(feat, init, vec(pad_cols(mu_shift[layer][None])), vec(w0[layer]), vec(a0[layer]), vec(k_k[layer]),
                    vec(k_a[layer]), vec(r_k[layer]), wup, aup, gup, ones_bd, rows_p // LANE, n_blocks, dec_seq)
    pre_p, pre_t = pre

    def to_scan(a, nb, steps):
        return a.reshape(nb, steps, RWKV_HEADS, HEAD_DIM).transpose(1, 3, 0, 2).reshape(steps, HEAD_DIM, nb * RWKV_HEADS)

    take_p = lambda a: a.reshape(n_batch, t_pad, -1)
    take_s = lambda a: a[:rows_s].reshape(dec_batch, dec_seq, -1)

    def from_scan(y, nb, steps):
        return y.reshape(steps, HEAD_DIM, nb, RWKV_HEADS).transpose(2, 0, 3, 1).reshape(nb, steps, RWKV_WIDTH)

    def state_in(s):
        nb = s.shape[0]
        return s.transpose(2, 3, 0, 1).reshape(HEAD_DIM, HEAD_DIM, nb * RWKV_HEADS)

    def state_out(s, nb):
        return s.reshape(HEAD_DIM, HEAD_DIM, nb, RWKV_HEADS).transpose(2, 3, 0, 1)

    tc = max(d for d in range(1, 49) if math.gcd(t_len, t_pad) % d == 0)
    y_p, wkv_p = _rwkv_scan(*[to_scan(take_p(a), n_batch, t_pad) for a in pre_p[:6]],
                            jnp.zeros((HEAD_DIM, HEAD_DIM, n_batch * RWKV_HEADS), F32), tc, t_len)
    y_s, wkv_s = _rwkv_scan(*[to_scan(take_s(a), dec_batch, dec_seq) for a in pre_t[:6]],
                            state_in(state_wkv[layer]), dec_seq, dec_seq)
    rows_t = rows - rows_p
    pad_tail = lambda a: jnp.pad(a, ((0, rows_t - rows_s), (0, 0)))
    ys = (from_scan(y_p, n_batch, t_pad).reshape(rows_p, RWKV_WIDTH),
          pad_tail(from_scan(y_s, dec_batch, dec_seq).reshape(rows_s, RWKV_WIDTH)))

    ya_p = _prompt_attn(qq, kvi, _prompt_bias_tiles(rel_bias), n_batch, n_blocks, min(IDX_TOPK, t_len // 4))

    qq_s, kvi_s = sample_rows(qq), sample_rows(kvi)

    def pad_q(a):
        a = jnp.pad(a.transpose(0, 2, 1, 3), ((0, 0), (0, 0), (0, Q_PAD - dec_seq), (0, 0)))
        return a.reshape(dec_batch, -1, a.shape[-1])

    qi8 = pad_q(qq_s[..., ATTN_WIDTH:].reshape(dec_batch, dec_seq, IDX_HEADS, IDX_DIM))
    wi_s = kvi_s[..., 2 * KV_WIDTH + IDX_DIM:2 * KV_WIDTH + IDX_DIM + IDX_HEADS]
    wb = jnp.broadcast_to(pad_q(wi_s[..., None]), (dec_batch, IDX_HEADS * Q_PAD, IDX_DIM))
    q8 = pad_q(qq_s[..., :ATTN_WIDTH].reshape(dec_batch, dec_seq, ATTN_HEADS, HEAD_DIM))
    pad_keys = lambda a: jnp.pad(a, ((0, 0), (0, PAGE - dec_seq), (0, 0)))
    ki_new = pad_keys(kvi_s[..., 2 * KV_WIDTH:2 * KV_WIDTH + IDX_DIM])
    k_new = pad_keys(kvi_s[..., :KV_WIDTH])
    v_new = pad_keys(kvi_s[..., KV_WIDTH:2 * KV_WIDTH])
    n_pool = cache_k.shape[1]
    pages_t = lambda c: c[layer].transpose(0, 2, 3, 1).reshape(n_pool, KV_WIDTH, PAGE)
    sc = _sample_scores(page_table, qi8, wb, cache_kidx[layer])
    sel = _sample_select(sc, qi8, wb, ki_new, dec_seq, min(IDX_TOPK, (past_len + dec_seq) // 4))
    o_s = _sample_attn(page_table, q8, sel, k_new.transpose(0, 2, 1), v_new.transpose(0, 2, 1),
                       _sample_bias_tiles(rel_bias, past_len, dec_seq), pages_t(cache_k), pages_t(cache_v))
    ya_s = o_s.reshape(dec_batch, ATTN_HEADS, Q_PAD, HEAD_DIM)[:, :, :dec_seq].transpose(0, 2, 1, 3).reshape(rows_s, ATTN_WIDTH)

    wo = w_o[layer].astype(BF16)
    x1, x1b, x1t = _mix(ys, (pre_p[7], pre_t[7]), (pre_p[6], pre_t[6]), (ya_p, pad_tail(ya_s)), xn,
                        vec(gn_g[layer]), vec(gn_b[layer]), ones_bd,
                        wo[:RWKV_WIDTH], wo[RWKV_WIDTH:], vec(ln1_g[layer]), vec(ln1_b[layer]), 256)
    routing = _peer_route(x1b, peer_wq[layer].astype(BF16), peer_subkeys[layer])
    peer_t = _peer_dense(x1t, peer_u[layer].astype(BF16), peer_v[layer].T.astype(BF16), *routing, 512)
    y_p, y_t = _ln_out(x1, peer_t, ln2_g[layer], ln2_b[layer], 256, rows_p)

    last_p = feat[jnp.arange(n_batch) * t_pad + (t_len - 1)]
    last_s = feat[rows_p + jnp.arange(dec_batch) * dec_seq + (dec_seq - 1)]
    kvi_p = prompt_rows(kvi)
    kv4 = lambda a, nb, steps: a.reshape(nb, steps, KV_HEADS, HEAD_DIM)[None]
    return (
        y_p.reshape(n_batch, t_pad, D_MODEL)[:, N_META:t_len], y_t[:rows_s].reshape(dec_batch, dec_seq, D_MODEL),
        kv4(kvi_p[..., :KV_WIDTH], n_batch, t_len), kv4(kvi_p[..., KV_WIDTH:2 * KV_WIDTH], n_batch, t_len),
        kvi_p[..., 2 * KV_WIDTH:2 * KV_WIDTH + IDX_DIM][None],
        state_out(wkv_p, n_batch)[None], last_p[:, :RWKV_COLS][None],
        kv4(kvi_s[..., :KV_WIDTH], dec_batch, dec_seq), kv4(kvi_s[..., KV_WIDTH:2 * KV_WIDTH], dec_batch, dec_seq),
        kvi_s[..., 2 * KV_WIDTH:2 * KV_WIDTH + IDX_DIM][None],
        state_out(wkv_s, dec_batch)[None], last_s[:, :RWKV_COLS][None],
    )
```

```python
import functools
import math

import numpy as np
import jax
import jax.numpy as jnp
from jax import lax
from jax.experimental import pallas as pl
from jax.experimental.pallas import tpu as pltpu

F32, BF16, I32 = jnp.float32, jnp.bfloat16, jnp.int32

D_MODEL = 2048
N_META = 16
HEAD_DIM = 64
RWKV_WIDTH = 1024
ATTN_WIDTH = 1024
RWKV_HEADS = 16
ATTN_HEADS = 16
KV_HEADS = 4
GROUP = 4
KV_WIDTH = 256
DECAY_LORA = 64
ICLR_LORA = 64
GATE_LORA = 160
RWKV_COLS = 3 * RWKV_WIDTH + DECAY_LORA + ICLR_LORA + GATE_LORA
RWKV_PAD = 3456
LORA_WA = DECAY_LORA + ICLR_LORA
GATE_PAD = RWKV_PAD - 3 * RWKV_WIDTH - LORA_WA
GN_EPS = 64e-5
IDX_HEADS = 8
IDX_DIM = 128
IDX_TOPK = 256
REL_BUCKETS = 32
REL_MAX_DIST = 128
PEER_HEADS = 8
PEER_NKEYS = 128
PEER_HALF = 128
PEER_TOPK = 16
PEER_EXPERTS = PEER_NKEYS * PEER_NKEYS
DN_ALPHA = 2.0 ** 0.25
LN_EPS = 1e-5
PAGE = 128
LANE = 128
ROW_ALIGN = 2560
VMEM_LIMIT = 56 * 1024 * 1024
NEG_BIG = -1e30


def _params(*sem):
    return pltpu.CompilerParams(dimension_semantics=sem, vmem_limit_bytes=VMEM_LIMIT)


def _dot(a, b):
    return jnp.dot(a, b, preferred_element_type=F32)


def _dot_nt(a, b):
    return lax.dot_general(a, b, (((1,), (1,)), ((), ())), preferred_element_type=F32)


def _split2(x):
    hi = x.astype(BF16)
    lo = (x - hi.astype(F32)).astype(BF16)
    return hi, lo


def _split3(x):
    hi = x.astype(BF16)
    r1 = x - hi.astype(F32)
    mid = r1.astype(BF16)
    lo = (r1 - mid.astype(F32)).astype(BF16)
    return hi, mid, lo


def _dot_hp(a, b):
    ah, al = _split2(a)
    bh, bl = _split2(b)
    return _dot(ah, bh) + (_dot(ah, bl) + _dot(al, bh))


def _dot_nt_hp(a, b):
    ah, al = _split2(a)
    bh, bl = _split2(b)
    return _dot_nt(ah, bh) + (_dot_nt(ah, bl) + _dot_nt(al, bh))


def _segsum(x, ones_bd):
    hi, mid, lo = _split3(x)
    return _dot(hi, ones_bd) + (_dot(mid, ones_bd) + _dot(lo, ones_bd))


BISECT_STEPS = 40


def _kth_largest_floor(count_ge, lo, hi, n_sel):
    def body(_, bracket):
        lo, hi = bracket
        mid = 0.5 * lo + 0.5 * hi
        enough = count_ge(mid) >= n_sel
        return jnp.where(enough, mid, lo), jnp.where(enough, hi, mid)

    return lax.fori_loop(0, BISECT_STEPS, body, (lo, hi))[0]


def _tie_cutoff(count_eq_below, need, nbits, shape):
    def body(it, cut):
        cand = cut | lax.shift_left(jnp.int32(1), nbits - 1 - it)
        return jnp.where(count_eq_below(cand) <= need, cand, cut)

    return lax.fori_loop(0, nbits, body, jnp.zeros(shape, I32))


def _ln(x, g, b):
    mu = jnp.mean(x, axis=-1, keepdims=True)
    xc = x - mu
    var = jnp.mean(xc * xc, axis=-1, keepdims=True)
    return xc * lax.rsqrt(var + LN_EPS) * g + b


def _ln_in_kernel(x_ref, g_ref, b_ref, xn_ref, xb_ref):
    y = _ln(x_ref[...], g_ref[...], b_ref[...])
    xn_ref[...] = y
    xb_ref[...] = y.astype(BF16)


def _ln_in(x, g, b, tm):
    rows = x.shape[0]
    row = pl.BlockSpec((tm, D_MODEL), lambda i: (i, 0))
    vec = pl.BlockSpec((1, D_MODEL), lambda i: (0, 0))
    return pl.pallas_call(
        _ln_in_kernel,
        grid=(rows // tm,),
        in_specs=[row, vec, vec],
        out_specs=[row, row],
        out_shape=[jax.ShapeDtypeStruct((rows, D_MODEL), F32), jax.ShapeDtypeStruct((rows, D_MODEL), BF16)],
        compiler_params=_params("parallel"),
    )(x, g.reshape(1, -1), b.reshape(1, -1))


def _ln_out_kernel(x_ref, pt_ref, g_ref, b_ref, op_ref, ot_ref, *, n_prompt_tiles):
    y = _ln(DN_ALPHA * x_ref[...] + pt_ref[...].T, g_ref[...], b_ref[...])

    @pl.when(pl.program_id(0) < n_prompt_tiles)
    def _():
        op_ref[...] = y

    @pl.when(pl.program_id(0) >= n_prompt_tiles)
    def _():
        ot_ref[...] = y


def _ln_out(x, pt, g, b, tm, rows_p):
    rows = x.shape[0]
    n_prompt_tiles = rows_p // tm
    row = pl.BlockSpec((tm, D_MODEL), lambda i: (i, 0))
    vec = pl.BlockSpec((1, D_MODEL), lambda i: (0, 0))
    return pl.pallas_call(
        functools.partial(_ln_out_kernel, n_prompt_tiles=n_prompt_tiles),
        grid=(rows // tm,),
        in_specs=[row, pl.BlockSpec((D_MODEL, tm), lambda i: (0, i)), vec, vec],
        out_specs=[pl.BlockSpec((tm, D_MODEL), lambda i: (jnp.minimum(i, n_prompt_tiles - 1), 0)),
                   pl.BlockSpec((tm, D_MODEL), lambda i: (jnp.maximum(i - n_prompt_tiles, 0), 0))],
        out_shape=[jax.ShapeDtypeStruct((rows_p, D_MODEL), F32), jax.ShapeDtypeStruct((rows - rows_p, D_MODEL), F32)],
        compiler_params=_params("arbitrary"),
    )(x, pt, g.reshape(1, -1), b.reshape(1, -1))


def _mm_kernel(x_ref, w_ref, o_ref):
    o_ref[...] = _dot(x_ref[...], w_ref[...])


def _matmul(xb, w, tm, tn):
    m, k = xb.shape
    n = w.shape[1]
    return pl.pallas_call(
        _mm_kernel,
        grid=(m // tm, n // tn),
        in_specs=[pl.BlockSpec((tm, k), lambda i, j: (i, 0)), pl.BlockSpec((k, tn), lambda i, j: (0, j))],
        out_specs=pl.BlockSpec((tm, tn), lambda i, j: (i, j)),
        out_shape=jax.ShapeDtypeStruct((m, n), F32),
        compiler_params=_params("parallel", "arbitrary"),
    )(xb, w)


def _rwkv_pre_kernel(cur_ref, prev8_ref, init_ref, mu_ref, w0_ref, a0_ref, kk_ref, ka_ref, rk_ref,
                     wup_ref, aup_ref, gup_ref, ones_ref, *out_refs, n_prompt_tiles, tiles_per_batch, dec_seq):
    i = pl.program_id(0)
    cur = cur_ref[...]
    row = lax.broadcasted_iota(I32, cur.shape, 0)
    prev = jnp.where(row == 0, jnp.broadcast_to(prev8_ref[7:8, :], cur.shape), pltpu.roll(cur, 1, axis=0))
    batch_start = ((i % tiles_per_batch) == 0).astype(I32)
    first_prompt = jnp.where(row == 0, batch_start, 0)
    first_sample = jnp.where(row % dec_seq == 0, 1, 0)
    first = jnp.where(i < n_prompt_tiles, first_prompt, first_sample)
    prev = jnp.where(first > 0, init_ref[...], prev)

    xm = cur + (prev - cur) * mu_ref[...]
    r = xm[:, 0:RWKV_WIDTH]
    k = xm[:, RWKV_WIDTH:2 * RWKV_WIDTH]
    v = xm[:, 2 * RWKV_WIDTH:3 * RWKV_WIDTH]
    wa = xm[:, 3 * RWKV_WIDTH:3 * RWKV_WIDTH + LORA_WA]
    gl = xm[:, 3 * RWKV_WIDTH + LORA_WA:]
    ones_bd = ones_ref[...]

    nz = -(w0_ref[...] + _dot_hp(jnp.tanh(wa), wup_ref[...]))
    softplus = jnp.maximum(nz, 0.0) + jnp.log1p(jnp.exp(-jnp.abs(nz)))
    decay = jnp.exp(-jnp.exp(-softplus - 0.5))
    a = jax.nn.sigmoid(a0_ref[...] + _dot_hp(wa, aup_ref[...]))
    g = _dot_hp(jax.nn.sigmoid(gl), gup_ref[...])
    kn = k * kk_ref[...]
    kn = kn / jnp.maximum(jnp.sqrt(_segsum(kn * kn, ones_bd)), 1e-12)
    k_h = k * (1.0 + (a - 1.0) * ka_ref[...])
    vals = (r, decay, k_h, v, kn, kn * a, g, _segsum(r * k_h * rk_ref[...], ones_bd) * v)
    n_out = len(vals)

    @pl.when(i < n_prompt_tiles)
    def _():
        for o_ref, val in zip(out_refs[:n_out], vals):
            o_ref[...] = val

    @pl.when(i >= n_prompt_tiles)
    def _():
        for o_ref, val in zip(out_refs[n_out:], vals):
            o_ref[...] = val


def _rwkv_pre(feat, init, mu, w0, a0, k_k, k_a, r_k, wup, aup, gup, ones_bd, n_prompt_tiles, tiles_per_batch, dec_seq):
    tm = LANE
    n_tiles = feat.shape[0] // tm
    vec = lambda n: pl.BlockSpec((1, n), lambda i: (0, 0))
    full = lambda a: pl.BlockSpec(a.shape, lambda i: (0, 0))
    out_p = pl.BlockSpec((tm, RWKV_WIDTH), lambda i: (jnp.minimum(i, n_prompt_tiles - 1), 0))
    out_t = pl.BlockSpec((tm, RWKV_WIDTH), lambda i: (jnp.maximum(i - n_prompt_tiles, 0), 0))
    shape_p = jax.ShapeDtypeStruct((n_prompt_tiles * tm, RWKV_WIDTH), F32)
    shape_t = jax.ShapeDtypeStruct(((n_tiles - n_prompt_tiles) * tm, RWKV_WIDTH), F32)
    kern = functools.partial(_rwkv_pre_kernel, n_prompt_tiles=n_prompt_tiles, tiles_per_batch=tiles_per_batch, dec_seq=dec_seq)
    outs = pl.pallas_call(
        kern,
        grid=(n_tiles,),
        in_specs=[
            pl.BlockSpec((tm, RWKV_PAD), lambda i: (i, 0)),
            pl.BlockSpec((8, RWKV_PAD), lambda i: (jnp.maximum(i * (tm // 8) - 1, 0), 0)),
            pl.BlockSpec((tm, RWKV_PAD), lambda i: (jnp.where(i == n_prompt_tiles, 1, 0), 0)),
            vec(RWKV_PAD), vec(RWKV_WIDTH), vec(RWKV_WIDTH), vec(RWKV_WIDTH), vec(RWKV_WIDTH), vec(RWKV_WIDTH),
            full(wup), full(aup), full(gup), full(ones_bd),
        ],
        out_specs=[out_p] * 8 + [out_t] * 8,
        out_shape=[shape_p] * 8 + [shape_t] * 8,
        compiler_params=_params("arbitrary"),
    )(feat, feat, init, mu, w0, a0, k_k, k_a, r_k, wup, aup, gup, ones_bd)
    return outs[:8], outs[8:]


def _to_scan_kernel(*refs):
    n = len(refs) // 2
    for x_ref, o_ref in zip(refs[:n], refs[n:]):
        for t in range(o_ref.shape[0]):
            x_t = x_ref[:, t, :]
            rows = jnp.concatenate([x_t[:, h * HEAD_DIM:(h + 1) * HEAD_DIM] for h in range(RWKV_HEADS)], axis=0)
            o_ref[t] = rows.T


def _to_scan_rows(arrays, tc):
    n_batch, steps, width = arrays[0].shape
    spec_in = pl.BlockSpec((n_batch, tc, width), lambda c: (0, c, 0))
    spec_out = pl.BlockSpec((tc, HEAD_DIM, LANE), lambda c: (c, 0, 0))
    return pl.pallas_call(
        _to_scan_kernel,
        grid=(steps // tc,),
        in_specs=[spec_in] * len(arrays),
        out_specs=[spec_out] * len(arrays),
        out_shape=[jax.ShapeDtypeStruct((steps, HEAD_DIM, LANE), F32)] * len(arrays),
        compiler_params=_params("parallel"),
    )(*arrays)


def _rwkv_scan_kernel(r_ref, w_ref, k_ref, v_ref, kn_ref, b_ref, s0_ref, y_ref, s_ref, *, n_chunks):
    c_id = pl.program_id(1)

    @pl.when(c_id == 0)
    def _():
        s_ref[...] = s0_ref[...]

    @pl.when(c_id >= n_chunks)
    def _():
        y_ref[...] = jnp.zeros(y_ref.shape, F32)

    def step(t, carry):
        kn_t = kn_ref[t]
        w_t = w_ref[t]
        b_t = b_ref[t]
        k_t = k_ref[t]
        r_t = r_ref[t]

        def value_row(vi, c):
            s_v = s_ref[vi]
            s_kn = jnp.sum(s_v * kn_t, axis=0, keepdims=True)
            s_new = s_v * w_t - s_kn * b_t + v_ref[t, pl.ds(vi, 1), :] * k_t
            s_ref[vi] = s_new
            y_ref[t, pl.ds(vi, 1), :] = jnp.sum(s_new * r_t, axis=0, keepdims=True)
            return c

        return lax.fori_loop(0, HEAD_DIM, value_row, carry, unroll=16)

    @pl.when(c_id < n_chunks)
    def _():
        lax.fori_loop(0, r_ref.shape[0], step, 0)


def _rwkv_scan(r, w, k, v, kn, b, s0, tc, n_steps):
    steps, _, pairs = r.shape
    seq = pl.BlockSpec((tc, HEAD_DIM, LANE), lambda p, c: (c, 0, p))
    state = pl.BlockSpec((HEAD_DIM, HEAD_DIM, LANE), lambda p, c: (0, 0, p))
    return pl.pallas_call(
        functools.partial(_rwkv_scan_kernel, n_chunks=n_steps // tc),
        grid=(pairs // LANE, steps // tc),
        in_specs=[seq] * 6 + [state],
        out_specs=[seq, state],
        out_shape=[jax.ShapeDtypeStruct(r.shape, F32), jax.ShapeDtypeStruct(s0.shape, F32)],
        compiler_params=_params("parallel", "arbitrary"),
    )(r, w, k, v, kn, b, s0)


def _prompt_attn_kernel(qq_ref, kvi_ref, wi_ref, bias_ref, y_ref,
                        vt_ref, kb_ref, kib_ref, qn_ref, key_ref, sel_ref, cut_ref, acc_ref, *, n_blocks, n_sel):
    i = pl.program_id(1)
    n_kb = i + 1
    gq = GROUP * LANE

    @pl.when(i == 0)
    def _():
        t_pad = n_blocks * LANE
        for j in range(n_blocks):
            vt_ref[j] = kvi_ref[j * LANE:(j + 1) * LANE, KV_WIDTH:2 * KV_WIDTH].T.astype(BF16)
        for n in range(KV_HEADS):
            kb_ref[n, 0:t_pad, :] = kvi_ref[:, n * HEAD_DIM:(n + 1) * HEAD_DIM].astype(BF16)
        kib_ref[...] = kvi_ref[:, 2 * KV_WIDTH:2 * KV_WIDTH + IDX_DIM].astype(BF16)
        for j in range(n_blocks, vt_ref.shape[0]):
            vt_ref[j] = jnp.zeros(vt_ref.shape[1:], BF16)
            kb_ref[:, j * LANE:(j + 1) * LANE, :] = jnp.zeros((KV_HEADS, LANE, HEAD_DIM), BF16)

    kpos0 = lax.broadcasted_iota(I32, (LANE, LANE), 0)
    qpos = i * LANE + lax.broadcasted_iota(I32, (LANE, LANE), 1)

    qi_all = jnp.concatenate(
        [qq_ref[:, ATTN_WIDTH + h * IDX_DIM:ATTN_WIDTH + (h + 1) * IDX_DIM] for h in range(IDX_HEADS)], axis=0).astype(BF16)
    w_t = wi_ref[...].T
    w_flat = jnp.concatenate([w_t[h:h + 1, :] for h in range(IDX_HEADS)], axis=1)

    def score_block(j, bounds):
        lo, hi = bounds
        r0 = pl.multiple_of(j * LANE, LANE)
        ki = kib_ref[pl.ds(r0, LANE), :]
        s = jnp.maximum(_dot_nt(ki, qi_all), 0.0) * w_flat
        acc = s[:, 0:LANE]
        for h in range(1, IDX_HEADS):
            acc = acc + s[:, h * LANE:(h + 1) * LANE]
        causal = kpos0 + r0 <= qpos
        key_ref[pl.ds(r0, LANE), :] = jnp.where(causal, acc, -jnp.inf)
        return jnp.minimum(lo, jnp.where(causal, acc, jnp.inf)), jnp.maximum(hi, jnp.where(causal, acc, -jnp.inf))

    lo, hi = lax.fori_loop(0, n_kb, score_block, (jnp.full((LANE, LANE), jnp.inf, F32), jnp.full((LANE, LANE), -jnp.inf, F32)))
    lo = jnp.min(lo, axis=0, keepdims=True)
    hi = jnp.max(hi, axis=0, keepdims=True)

    row1 = (1, LANE)

    def fold(flag, init, combine):
        def body(j, acc):
            r0 = pl.multiple_of(j * LANE, LANE)
            return combine(acc, flag(key_ref[pl.ds(r0, LANE), :], r0))

        return lax.fori_loop(0, n_kb, body, jnp.full((LANE, LANE), init, F32))

    def count(flag):
        return jnp.sum(fold(flag, 0.0, jnp.add), axis=0, keepdims=True)

    floor = _kth_largest_floor(lambda t: count(lambda k, r0: jnp.where(k >= t, 1.0, 0.0)), lo, hi, n_sel)
    thr = jnp.min(fold(lambda k, r0: jnp.where(k >= floor, k, jnp.inf), jnp.inf, jnp.minimum), axis=0, keepdims=True)
    need = n_sel - count(lambda k, r0: jnp.where(k > thr, 1.0, 0.0))
    n_tied = count(lambda k, r0: jnp.where(k == thr, 1.0, 0.0))
    nbits = (n_blocks * LANE).bit_length()
    cut_ref[...] = jnp.full(cut_ref.shape, 2 ** nbits, I32)

    @pl.when(jnp.max(n_tied - need) > 0.0)
    def _():
        cut = _tie_cutoff(
            lambda c: count(lambda k, r0: jnp.where(k == thr, jnp.where(kpos0 + r0 < c, 1.0, 0.0), 0.0)), need, nbits, row1)
        cut_ref[...] = jnp.broadcast_to(cut, cut_ref.shape)

    cut = cut_ref[0:1, :]

    def select_block(j, c):
        r0 = pl.multiple_of(j * LANE, LANE)
        k = key_ref[pl.ds(r0, LANE), :]
        kpos = kpos0 + r0
        chosen = jnp.where(k > thr, 1.0, jnp.where(k == thr, jnp.where(kpos < cut, 1.0, 0.0), 0.0))
        sel_ref[pl.ds(r0, LANE), :] = jnp.where(kpos <= qpos, chosen, 0.0)
        return c

    lax.fori_loop(0, n_kb, select_block, 0)

    @pl.when(n_kb % 2 == 1)
    def _():
        sel_ref[pl.ds(pl.multiple_of(n_kb * LANE, LANE), LANE), :] = jnp.zeros((LANE, LANE), F32)

    for n in range(KV_HEADS):
        q_n = jnp.concatenate(
            [qq_ref[:, (GROUP * n + g) * HEAD_DIM:(GROUP * n + g + 1) * HEAD_DIM] for g in range(GROUP)], axis=0)
        qn_ref[n] = (q_n * HEAD_DIM ** -0.5).astype(BF16)
    acc_ref[...] = jnp.zeros(acc_ref.shape, F32)
    pair = 2 * LANE

    def key_blocks(jj, carry):
        ms, ls = carry
        r0 = pl.multiple_of(jj * pair, pair)
        mask = sel_ref[pl.ds(r0, pair), :] > 0.5
        near = (jnp.clip(i - 2 * jj, 0, 2), jnp.clip(i - 2 * jj - 1, 0, 2))
        new_ms, new_ls = [], []
        for n in range(KV_HEADS):
            bias = jnp.concatenate([bias_ref[n, near[0]], bias_ref[n, near[1]]], axis=0)
            s = _dot_nt(kb_ref[n, pl.ds(r0, pair), :], qn_ref[n]) + bias
            s = jnp.concatenate([jnp.where(mask, s[:, g * LANE:(g + 1) * LANE], NEG_BIG) for g in range(GROUP)], axis=1)
            m_new = jnp.maximum(ms[n], jnp.max(s, axis=0, keepdims=True))
            alpha = jnp.exp(ms[n] - m_new)
            p = jnp.exp(s - m_new)
            new_ls.append(alpha * ls[n] + jnp.sum(p, axis=0, keepdims=True))
            new_ms.append(m_new)
            rows = slice(n * HEAD_DIM, (n + 1) * HEAD_DIM)
            vt = jnp.concatenate([vt_ref[2 * jj, rows, :], vt_ref[2 * jj + 1, rows, :]], axis=1)
            acc_ref[n] = alpha * acc_ref[n] + _dot(vt, p.astype(BF16))
        return tuple(new_ms), tuple(new_ls)

    init = (tuple(jnp.full((1, gq), NEG_BIG, F32) for _ in range(KV_HEADS)),
            tuple(jnp.zeros((1, gq), F32) for _ in range(KV_HEADS)))
    _, ls = lax.fori_loop(0, (n_kb + 1) // 2, key_blocks, init)
    outs = []
    for n in range(KV_HEADS):
        o = acc_ref[n] / ls[n]
        outs += [o[:, g * LANE:(g + 1) * LANE].T for g in range(GROUP)]
    y_ref[...] = jnp.concatenate(outs, axis=1)


def _prompt_attn(qq, kvi, bias_tiles, n_batch, n_blocks, n_sel):
    t_pad = n_blocks * LANE
    n_even = n_blocks + n_blocks % 2
    kern = functools.partial(_prompt_attn_kernel, n_blocks=n_blocks, n_sel=n_sel)
    return pl.pallas_call(
        kern,
        grid=(n_batch, n_blocks),
        in_specs=[
            pl.BlockSpec((LANE, 2 * ATTN_WIDTH), lambda b, i: (b * n_blocks + i, 0)),
            pl.BlockSpec((t_pad, kvi.shape[1]), lambda b, i: (b, 0)),
            pl.BlockSpec((LANE, LANE), lambda b, i: (b * n_blocks + i, (2 * KV_WIDTH + IDX_DIM) // LANE)),
            pl.BlockSpec(bias_tiles.shape, lambda b, i: (0, 0, 0, 0)),
        ],
        out_specs=pl.BlockSpec((LANE, ATTN_WIDTH), lambda b, i: (b * n_blocks + i, 0)),
        out_shape=jax.ShapeDtypeStruct((n_batch * t_pad, ATTN_WIDTH), F32),
        scratch_shapes=[
            pltpu.VMEM((n_even, KV_WIDTH, LANE), BF16),
            pltpu.VMEM((KV_HEADS, n_even * LANE, HEAD_DIM), BF16),
            pltpu.VMEM((t_pad, IDX_DIM), BF16),
            pltpu.VMEM((KV_HEADS, GROUP * LANE, HEAD_DIM), BF16),
            pltpu.VMEM((t_pad, LANE), F32),
            pltpu.VMEM((n_even * LANE, LANE), F32),
            pltpu.VMEM((8, LANE), I32),
            pltpu.VMEM((KV_HEADS, HEAD_DIM, GROUP * LANE), F32),
        ],
        compiler_params=_params("parallel", "arbitrary"),
    )(qq, kvi, kvi, bias_tiles)


PAGES_PER_STEP = 16
Q_PAD = 8


def _sample_score_kernel(pt_ref, qi_ref, wb_ref, *refs):
    page_refs, out_ref = refs[:PAGES_PER_STEP], refs[PAGES_PER_STEP]
    qi = qi_ref[...].astype(BF16)
    wb = wb_ref[...]
    for u in range(PAGES_PER_STEP):
        s = jnp.maximum(_dot_nt(qi, page_refs[u][...].astype(BF16)), 0.0) * wb
        acc = s[0:Q_PAD]
        for h in range(1, IDX_HEADS):
            acc = acc + s[h * Q_PAD:(h + 1) * Q_PAD]
        out_ref[u] = acc


def _sample_scores(page_table, qi8, wb, cache_kidx):
    n_batch, n_pages = page_table.shape
    page_spec = lambda u: pl.BlockSpec((None, PAGE, IDX_DIM), lambda b, s, pt: (pt[b, s * PAGES_PER_STEP + u], 0, 0))
    per_batch = pl.BlockSpec((None, IDX_HEADS * Q_PAD, IDX_DIM), lambda b, s, pt: (b, 0, 0))
    return pl.pallas_call(
        _sample_score_kernel,
        grid_spec=pltpu.PrefetchScalarGridSpec(
            num_scalar_prefetch=1,
            grid=(n_batch, n_pages // PAGES_PER_STEP),
            in_specs=[per_batch, per_batch] + [page_spec(u) for u in range(PAGES_PER_STEP)],
            out_specs=pl.BlockSpec((None, PAGES_PER_STEP, Q_PAD, PAGE), lambda b, s, pt: (b, s, 0, 0)),
        ),
        out_shape=jax.ShapeDtypeStruct((n_batch, n_pages, Q_PAD, PAGE), F32),
        compiler_params=_params("parallel", "arbitrary"),
    )(page_table, qi8, wb, *([cache_kidx] * PAGES_PER_STEP))


def _sample_select_kernel(sc_ref, qi_ref, wb_ref, kin_ref, sel_ref, key_ref, *, n_pages, dec_seq, n_sel):
    qrow = lax.broadcasted_iota(I32, (Q_PAD, PAGE), 0)
    lane = lax.broadcasted_iota(I32, (Q_PAD, PAGE), 1)
    s = jnp.maximum(_dot_nt(qi_ref[...].astype(BF16), kin_ref[...].astype(BF16)), 0.0) * wb_ref[...]
    acc = s[0:Q_PAD]
    for h in range(1, IDX_HEADS):
        acc = acc + s[h * Q_PAD:(h + 1) * Q_PAD]
    new_valid = jnp.where(lane < dec_seq, jnp.where(lane <= qrow, 1, 0), 0) > 0
    key_ref[0:n_pages] = sc_ref[...]
    key_ref[n_pages] = jnp.where(new_valid, acc, -jnp.inf)

    def lane_fold(x, combine, reduce_lanes):
        n_main = x.shape[0] // 8 * 8
        part = x[0:n_main].reshape(8, n_main // 8, Q_PAD, PAGE)
        for axis in (1, 0):
            acc_p = part[:, 0] if axis == 1 else part[0]
            for u in range(1, part.shape[axis]):
                acc_p = combine(acc_p, part[:, u] if axis == 1 else part[u])
            part = acc_p
        for extra in range(n_main, x.shape[0]):
            part = combine(part, x[extra])
        return reduce_lanes(part, axis=1, keepdims=True)

    lane_count = lambda x: lane_fold(x, jnp.add, jnp.sum)
    col1 = (Q_PAD, 1)
    keys = key_ref[...]
    lo = lane_fold(jnp.where(keys == -jnp.inf, jnp.inf, keys), jnp.minimum, jnp.min)
    hi = lane_fold(keys, jnp.maximum, jnp.max)
    floor = _kth_largest_floor(lambda t: lane_count(jnp.where(key_ref[...] >= t, 1.0, 0.0)), lo, hi, n_sel)
    thr = lane_fold(jnp.where(keys >= floor, keys, jnp.inf), jnp.minimum, jnp.min)
    need = n_sel - lane_count(jnp.where(keys > thr, 1.0, 0.0))
    shape3 = (n_pages + 1, Q_PAD, PAGE)
    kidx = lax.broadcasted_iota(I32, shape3, 0) * PAGE + lax.broadcasted_iota(I32, shape3, 2)
    cut = _tie_cutoff(
        lambda c: lane_count(jnp.where(key_ref[...] == thr, jnp.where(kidx < c, 1.0, 0.0), 0.0)),
        need, ((n_pages + 1) * PAGE).bit_length(), col1)
    chosen = jnp.where(keys > thr, 1.0, jnp.where(keys == thr, jnp.where(kidx < cut, 1.0, 0.0), 0.0))
    sel_ref[0:n_pages] = chosen[0:n_pages]
    sel_ref[n_pages] = jnp.where(new_valid, chosen[n_pages], 0.0)


def _sample_select(sc, qi8, wb, ki_new, dec_seq, n_sel):
    n_batch, n_pages = sc.shape[:2]
    kern = functools.partial(_sample_select_kernel, n_pages=n_pages, dec_seq=dec_seq, n_sel=n_sel)
    per_batch = lambda a: pl.BlockSpec((None,) + a.shape[1:], lambda b: (b,) + (0,) * (a.ndim - 1))
    return pl.pallas_call(
        kern,
        grid=(n_batch,),
        in_specs=[per_batch(sc), per_batch(qi8), per_batch(wb), per_batch(ki_new)],
        out_specs=pl.BlockSpec((None, n_pages + 1, Q_PAD, PAGE), lambda b: (b, 0, 0, 0)),
        out_shape=jax.ShapeDtypeStruct((n_batch, n_pages + 1, Q_PAD, PAGE), F32),
        scratch_shapes=[pltpu.VMEM((n_pages + 1, Q_PAD, PAGE), F32)],
        compiler_params=_params("parallel"),
    )(sc, qi8, wb, ki_new)


def _sample_attn_kernel(pt_ref, q_ref, sel_ref, selnew_ref, knew_ref, vnew_ref, bias_ref, *refs, n_steps):
    k_refs = refs[:PAGES_PER_STEP]
    v_refs = refs[PAGES_PER_STEP:2 * PAGES_PER_STEP]
    o_ref, m_ref, l_ref, acc_ref = refs[2 * PAGES_PER_STEP:]
    s_id = pl.program_id(1)
    rows = KV_HEADS * GROUP * Q_PAD
    per_kv = GROUP * Q_PAD
    q = (q_ref[...] * HEAD_DIM ** -0.5).astype(BF16)

    @pl.when(s_id == 0)
    def _():
        m_ref[...] = jnp.full(m_ref.shape, NEG_BIG, F32)
        l_ref[...] = jnp.zeros(l_ref.shape, F32)
        acc_ref[...] = jnp.zeros(acc_ref.shape, F32)

    def attend(k_pages, v_pages, sel_pages, bias_pages):
        s_blocks, sel_blocks = [], []
        for kp, sp, bp in zip(k_pages, sel_pages, bias_pages):
            kb = kp.astype(BF16)
            s = jnp.concatenate(
                [_dot(q[n * per_kv:(n + 1) * per_kv], kb[n * HEAD_DIM:(n + 1) * HEAD_DIM]) for n in range(KV_HEADS)],
                axis=0) + bp
            s_blocks.append(s)
            sel_blocks.append(jnp.concatenate([sp] * (KV_HEADS * GROUP), axis=0) > 0.5)
        s = jnp.concatenate(s_blocks, axis=1)
        sel = jnp.concatenate(sel_blocks, axis=1)
        s = jnp.where(sel, s, NEG_BIG)
        m_old = m_ref[...]
        m_new = jnp.maximum(m_old, jnp.max(s, axis=1, keepdims=True))
        alpha = jnp.exp(m_old - m_new)
        p = jnp.where(sel, jnp.exp(s - m_new), 0.0)
        l_ref[...] = alpha * l_ref[...] + jnp.sum(p, axis=1, keepdims=True)
        m_ref[...] = m_new
        pb = p.astype(BF16)
        pv = None
        for u, vp in enumerate(v_pages):
            vb = vp.astype(BF16)
            pu = pb[:, u * PAGE:(u + 1) * PAGE]
            part = jnp.concatenate(
                [_dot_nt(pu[n * per_kv:(n + 1) * per_kv], vb[n * HEAD_DIM:(n + 1) * HEAD_DIM]) for n in range(KV_HEADS)], axis=0)
            pv = part if pv is None else pv + part
        acc_ref[...] = alpha * acc_ref[...] + pv

    @pl.when(s_id < n_steps)
    def _():
        far, near = bias_ref[0], bias_ref[1]
        biases = [far] * PAGES_PER_STEP
        last = s_id == n_steps - 1
        biases[-1] = jnp.where(last, near, far)
        attend([r[...] for r in k_refs], [r[...] for r in v_refs], [sel_ref[u] for u in range(PAGES_PER_STEP)], biases)

    @pl.when(s_id == n_steps)
    def _():
        attend([knew_ref[...]], [vnew_ref[...]], [selnew_ref[...]], [bias_ref[2]])
        o_ref[...] = acc_ref[...] / l_ref[...]


def _sample_attn(page_table, q8, sel, k_new, v_new, bias_tiles, cache_k, cache_v):
    n_batch, n_pages = page_table.shape
    n_steps = n_pages // PAGES_PER_STEP
    rows = KV_HEADS * GROUP * Q_PAD
    kern = functools.partial(_sample_attn_kernel, n_steps=n_steps)

    def page_spec(u):
        return pl.BlockSpec((None, KV_WIDTH, PAGE),
                            lambda b, s, pt: (pt[b, jnp.minimum(s, n_steps - 1) * PAGES_PER_STEP + u], 0, 0))

    per_batch = lambda a: pl.BlockSpec((None,) + a.shape[1:], lambda b, s, pt: (b,) + (0,) * (a.ndim - 1))
    return pl.pallas_call(
        kern,
        grid_spec=pltpu.PrefetchScalarGridSpec(
            num_scalar_prefetch=1,
            grid=(n_batch, n_steps + 1),
            in_specs=[
                per_batch(q8),
                pl.BlockSpec((None, PAGES_PER_STEP, Q_PAD, PAGE), lambda b, s, pt: (b, jnp.minimum(s, n_steps - 1), 0, 0)),
                pl.BlockSpec((None, None, Q_PAD, PAGE), lambda b, s, pt: (b, n_pages, 0, 0)),
                per_batch(k_new), per_batch(v_new),
                pl.BlockSpec(bias_tiles.shape, lambda b, s, pt: (0, 0, 0)),
            ] + [page_spec(u) for u in range(PAGES_PER_STEP)] * 2,
            out_specs=pl.BlockSpec((None, rows, HEAD_DIM), lambda b, s, pt: (b, 0, 0)),
            scratch_shapes=[pltpu.VMEM((rows, 1), F32), pltpu.VMEM((rows, 1), F32), pltpu.VMEM((rows, HEAD_DIM), F32)],
        ),
        out_shape=jax.ShapeDtypeStruct((n_batch, rows, HEAD_DIM), F32),
        compiler_params=_params("parallel", "arbitrary"),
    )(page_table, q8, sel, sel, k_new, v_new, bias_tiles, *([cache_k] * PAGES_PER_STEP), *([cache_v] * PAGES_PER_STEP))


def _mix_kernel(ysp_ref, yst_ref, bonusp_ref, bonust_ref, gp_ref, gt_ref, yap_ref, yat_ref,
                xn_ref, gng_ref, gnb_ref, ones_ref, wor_ref, woa_ref, lg_ref, lb_ref,
                x1_ref, x1b_ref, x1t_ref, *, n_prompt_tiles):
    is_prompt = pl.program_id(0) < n_prompt_tiles
    pick = lambda p_ref, t_ref: jnp.where(is_prompt, p_ref[...], t_ref[...])
    ones_bd = ones_ref[...]
    ys = pick(ysp_ref, yst_ref)
    inv = 1.0 / HEAD_DIM
    yc = ys - _segsum(ys, ones_bd) * inv
    var = _segsum(yc * yc, ones_bd) * inv
    yr = (yc * lax.rsqrt(var + GN_EPS) * gng_ref[...] + gnb_ref[...] + pick(bonusp_ref, bonust_ref)) * pick(gp_ref, gt_ref)
    mix = _dot(yr.astype(BF16), wor_ref[...]) + _dot(pick(yap_ref, yat_ref).astype(BF16), woa_ref[...])
    x1 = _ln(DN_ALPHA * xn_ref[...] + mix, lg_ref[...], lb_ref[...])
    x1_ref[...] = x1
    x1b_ref[...] = x1.astype(BF16)
    x1t_ref[...] = x1.T.astype(BF16)


def _mix(ys, bonus, g, ya, xn, gn_g, gn_b, ones_bd, wo_r, wo_a, ln_g, ln_b, tm):
    rows = xn.shape[0]
    n_prompt_tiles = ys[0].shape[0] // tm
    half_p = pl.BlockSpec((tm, RWKV_WIDTH), lambda i: (jnp.minimum(i, n_prompt_tiles - 1), 0))
    half_t = pl.BlockSpec((tm, RWKV_WIDTH), lambda i: (jnp.maximum(i - n_prompt_tiles, 0), 0))
    row = pl.BlockSpec((tm, D_MODEL), lambda i: (i, 0))
    vec = lambda n: pl.BlockSpec((1, n), lambda i: (0, 0))
    full = lambda a: pl.BlockSpec(a.shape, lambda i: (0, 0))
    return pl.pallas_call(
        functools.partial(_mix_kernel, n_prompt_tiles=n_prompt_tiles),
        grid=(rows // tm,),
        in_specs=[half_p, half_t] * 4 + [row, vec(RWKV_WIDTH), vec(RWKV_WIDTH), full(ones_bd), full(wo_r), full(wo_a),
                                         vec(D_MODEL), vec(D_MODEL)],
        out_specs=[row, row, pl.BlockSpec((D_MODEL, tm), lambda i: (0, i))],
        out_shape=[jax.ShapeDtypeStruct((rows, D_MODEL), F32), jax.ShapeDtypeStruct((rows, D_MODEL), BF16),
                   jax.ShapeDtypeStruct((D_MODEL, rows), BF16)],
        compiler_params=_params("parallel"),
    )(*ys, *bonus, *g, *ya, xn, gn_g, gn_b, ones_bd, wo_r, wo_a, ln_g, ln_b)


CAND_PAIRS = [(c, d) for c in range(PEER_TOPK) for d in range(PEER_TOPK) if (c + 1) * (d + 1) <= PEER_TOPK]


def _top_rows(x, n):
    rows = []
    for _ in range(n):
        m = jnp.max(x, axis=0, keepdims=True)
        rows.append(m)
        x = jnp.where(x == m, -jnp.inf, x)
    return rows


def _peer_route_kernel(x_ref, wq_ref, sub_ref, s1_ref, s2_ref, e2_ref, thr_ref, m1_ref, zinv_ref):
    q = _dot(x_ref[...], wq_ref[...])
    for h in range(PEER_HEADS):
        base = h * 2 * PEER_HALF
        s1 = _dot_nt_hp(sub_ref[h, 0], q[:, base:base + PEER_HALF])
        s2 = _dot_nt_hp(sub_ref[h, 1], q[:, base + PEER_HALF:base + 2 * PEER_HALF])
        top1 = _top_rows(s1, PEER_TOPK)
        top2 = _top_rows(s2, PEER_TOPK)
        cand = jnp.concatenate([top1[c] + top2[d] for c, d in CAND_PAIRS]
                               + [jnp.full_like(top1[0], -jnp.inf)] * (-len(CAND_PAIRS) % 8), axis=0)
        best = _top_rows(cand, PEER_TOPK)
        thr = best[-1]
        m = top1[0] + top2[0]
        z = jnp.sum(jnp.where(cand >= thr, jnp.exp(cand - m), 0.0), axis=0, keepdims=True)
        s1_ref[h] = s1
        s2_ref[h] = s2
        e2_ref[h] = jnp.exp(s2 - top2[0])
        thr_ref[h:h + 1, :] = thr
        m1_ref[h:h + 1, :] = top1[0]
        zinv_ref[h:h + 1, :] = 1.0 / z


def _peer_route(x1b, wq, subkeys):
    rows = x1b.shape[0]
    tm = LANE
    stat = pl.BlockSpec((PEER_HEADS, tm), lambda i: (0, i))
    big = pl.BlockSpec((PEER_HEADS, PEER_NKEYS, tm), lambda i: (0, 0, i))
    stat_shape = jax.ShapeDtypeStruct((PEER_HEADS, rows), F32)
    big_shape = jax.ShapeDtypeStruct((PEER_HEADS, PEER_NKEYS, rows), F32)
    return pl.pallas_call(
        _peer_route_kernel,
        grid=(rows // tm,),
        in_specs=[pl.BlockSpec((tm, D_MODEL), lambda i: (i, 0)),
                  pl.BlockSpec(wq.shape, lambda i: (0, 0)),
                  pl.BlockSpec(subkeys.shape, lambda i: (0, 0, 0, 0))],
        out_specs=[big, big, big, stat, stat, stat],
        out_shape=[big_shape, big_shape, big_shape, stat_shape, stat_shape, stat_shape],
        compiler_params=_params("parallel"),
    )(x1b, wq, subkeys)


EXPERT_ROWS = 8
MXU_DEPTH = 256


def _peer_dense_kernel(xt_ref, u_ref, vt_ref, s1_ref, s2_ref, e2_ref, thr_ref, m1_ref, zinv_ref, o_ref, *scratch):
    n_sub = EXPERT_ROWS * PEER_NKEYS // MXU_DEPTH
    ht_refs, g_refs, a_refs = scratch[:n_sub], scratch[n_sub:2 * n_sub], scratch[2 * n_sub:]
    j = pl.program_id(1)
    tm = xt_ref.shape[1]

    @pl.when(j == 0)
    def _():
        o_ref[...] = jnp.zeros(o_ref.shape, F32)

    per = MXU_DEPTH // PEER_NKEYS
    half = PEER_NKEYS // 2

    def gate_rows(p):
        out = []
        for e in range(p * per, (p + 1) * per):
            s1_i = [s1_ref[h, e:e + 1, :] for h in range(PEER_HEADS)]
            out.append((s1_i, [jnp.exp(s1_i[h] - m1_ref[h:h + 1, :]) * zinv_ref[h:h + 1, :] for h in range(PEER_HEADS)]))
        return out

    def gate_block(p, rows_p, c, jh):
        cols = slice(c, c + LANE)
        jr = slice(jh * half, (jh + 1) * half)
        acc = [None] * per
        for h in range(PEER_HEADS):
            s2, e2, thr = s2_ref[h, jr, cols], e2_ref[h, jr, cols], thr_ref[h:h + 1, cols]
            for e in range(per):
                s1_i, f_i = rows_p[e]
                g = jnp.where(s1_i[h][:, cols] + s2 >= thr, e2 * f_i[h][:, cols], 0.0)
                acc[e] = g if acc[e] is None else acc[e] + g
        for e in range(per):
            g_refs[p][e * PEER_NKEYS + jh * half:e * PEER_NKEYS + (jh + 1) * half, cols] = acc[e]

    def gate_blocks(p):
        rows_p = gate_rows(p)
        return [functools.partial(gate_block, p, rows_p, c, jh) for c in range(0, tm, LANE) for jh in range(2)]

    def up_piece(p, k):
        r = p * MXU_DEPTH + k * PEER_NKEYS
        ht_refs[p][k * PEER_NKEYS:(k + 1) * PEER_NKEYS, :] = _dot(u_ref[r:r + PEER_NKEYS, :], xt_ref[...])

    def act(p):
        he = ht_refs[p][...]
        a_refs[p][...] = (0.5 * he * (1.0 + lax.erf(he * (2.0 ** -0.5))) * g_refs[p][...]).astype(BF16)

    down_rows = 512

    def down_piece(p, m):
        r = p * MXU_DEPTH
        rows = slice(m * down_rows, (m + 1) * down_rows)
        o_ref[rows, :] += _dot(vt_ref[rows, r:r + MXU_DEPTH], a_refs[p][...])

    def run(pieces):
        for piece in pieces:
            piece()

    ups = lambda p: [functools.partial(up_piece, p, k) for k in range(per)]
    downs = lambda p: [functools.partial(down_piece, p, m) for m in range(D_MODEL // down_rows)]
    run(gate_blocks(0) + ups(0))
    for p in range(n_sub):
        if p + 1 < n_sub:
            run(gate_blocks(p + 1) + ups(p + 1))
        act(p)
        run(downs(p))


def _peer_dense(x1t, u, vt, s1, s2, e2, thr, m1, zinv, tm):
    rows = x1t.shape[1]
    eb = EXPERT_ROWS * PEER_NKEYS
    stat = pl.BlockSpec((PEER_HEADS, tm), lambda i, j: (0, i))
    big = pl.BlockSpec((PEER_HEADS, PEER_NKEYS, tm), lambda i, j: (0, 0, i))
    return pl.pallas_call(
        _peer_dense_kernel,
        grid=(rows // tm, PEER_EXPERTS // eb),
        in_specs=[pl.BlockSpec((D_MODEL, tm), lambda i, j: (0, i)),
                  pl.BlockSpec((eb, D_MODEL), lambda i, j: (j, 0)),
                  pl.BlockSpec((D_MODEL, eb), lambda i, j: (0, j)),
                  pl.BlockSpec((PEER_HEADS, EXPERT_ROWS, tm), lambda i, j: (0, j, i)),
                  big, big, stat, stat, stat],
        out_specs=pl.BlockSpec((D_MODEL, tm), lambda i, j: (0, i)),
        out_shape=jax.ShapeDtypeStruct((D_MODEL, rows), F32),
        scratch_shapes=([pltpu.VMEM((MXU_DEPTH, tm), F32)] * (2 * eb // MXU_DEPTH)
                        + [pltpu.VMEM((MXU_DEPTH, tm), BF16)] * (eb // MXU_DEPTH)),
        compiler_params=_params("parallel", "arbitrary"),
    )(x1t, u, vt, s1, s2, e2, thr, m1, zinv)


def _rel_buckets(dist):
    max_exact = REL_BUCKETS // 2
    d = np.maximum(dist, 0)
    ratio = np.log(np.maximum(d, 1).astype(np.float32) / np.float32(max_exact)) / np.float32(math.log(REL_MAX_DIST / max_exact))
    log_b = max_exact + (ratio * np.float32(REL_BUCKETS - max_exact)).astype(np.int32)
    return np.where(d < max_exact, d, np.minimum(log_b, REL_BUCKETS - 1)).astype(np.int32)


def _bias_lookup(rel_bias, dist):
    onehot = np.eye(REL_BUCKETS, dtype=np.float32)[_rel_buckets(dist)]
    return jnp.dot(jnp.asarray(onehot), rel_bias.astype(F32), precision=lax.Precision.HIGHEST)


def _prompt_bias_tiles(rel_bias):
    kk = np.arange(LANE)[:, None]
    qq = np.arange(LANE)[None, :]
    tiles = []
    for delta in range(3):
        b = _bias_lookup(rel_bias, delta * LANE + qq - kk)
        b = b.reshape(LANE, LANE, KV_HEADS, GROUP).transpose(2, 0, 3, 1).reshape(KV_HEADS, LANE, GROUP * LANE)
        tiles.append(b)
    return jnp.stack(tiles, axis=1).astype(F32)


def _sample_bias_tiles(rel_bias, past_len, dec_seq):
    q = np.arange(Q_PAD)[:, None]
    off = np.arange(PAGE)[None, :]
    qpos = past_len + np.minimum(q, dec_seq - 1)
    dists = [qpos - 0 * off - (past_len - 2 * PAGE), qpos - (past_len - PAGE + off), qpos - (past_len + np.minimum(off, dec_seq - 1))]
    tiles = []
    for d in dists:
        b = _bias_lookup(rel_bias, d + 0 * off)
        tiles.append(b.transpose(2, 0, 1).reshape(ATTN_HEADS * Q_PAD, PAGE))
    return jnp.stack(tiles).astype(F32)


def kernel(x_prompt, x_sample, cache_k, cache_v, cache_kidx, state_wkv, state_shift, page_table, meta_tokens, ln_in_g, ln_in_b, rel_bias, w_in, mu_shift, w0, w_up, a0, a_up, g_up, k_k, k_a, r_k, gn_g, gn_b, w_o, ln1_g, ln1_b, peer_wq, peer_subkeys, peer_u, peer_v, ln2_g, ln2_b):
    n_batch, seq, _ = x_prompt.shape
    dec_batch, dec_seq, _ = x_sample.shape
    n_pages = page_table.shape[1]
    past_len = n_pages * PAGE
    t_len = seq + N_META
    n_blocks = -(-t_len // LANE)
    t_pad = n_blocks * LANE
    rows_p = n_batch * t_pad
    rows_s = dec_batch * dec_seq
    assert rows_s == LANE and Q_PAD >= dec_seq and n_pages % PAGES_PER_STEP == 0
    assert past_len >= 2 * PAGE + REL_MAX_DIST
    rows = -(-(rows_p + rows_s) // ROW_ALIGN) * ROW_ALIGN
    layer = 0

    meta = jnp.broadcast_to(meta_tokens[None], (n_batch, N_META, D_MODEL))
    xp = jnp.pad(jnp.concatenate([meta, x_prompt], axis=1), ((0, 0), (0, t_pad - t_len), (0, 0)))
    x_all = jnp.concatenate([xp.reshape(rows_p, D_MODEL), x_sample.reshape(rows_s, D_MODEL),
                             jnp.zeros((rows - rows_p - rows_s, D_MODEL), F32)], axis=0)
    xn, xb = _ln_in(x_all, ln_in_g, ln_in_b, 256)

    w = w_in[layer]
    c0 = RWKV_COLS
    w_rwkv = jnp.pad(w[:, :c0], ((0, 0), (0, RWKV_PAD - RWKV_COLS))).astype(BF16)
    w_qq = jnp.concatenate([w[:, c0:c0 + ATTN_WIDTH], w[:, c0 + ATTN_WIDTH + 2 * KV_WIDTH:c0 + 2 * ATTN_WIDTH + 2 * KV_WIDTH]], axis=1).astype(BF16)
    c_ki = c0 + 2 * ATTN_WIDTH + 2 * KV_WIDTH
    w_kvi = jnp.concatenate([w[:, c0 + ATTN_WIDTH:c0 + ATTN_WIDTH + 2 * KV_WIDTH], w[:, c_ki:c_ki + IDX_DIM],
                             jnp.pad(w[:, c_ki + IDX_DIM:], ((0, 0), (0, LANE - IDX_HEADS)))], axis=1).astype(BF16)
    feat = _matmul(xb, w_rwkv, 640, RWKV_PAD // 3)
    qq = _matmul(xb, w_qq, 640, 1024)
    kvi = _matmul(xb, w_kvi, 640, w_kvi.shape[1])

    def prompt_rows(a):
        return a[:rows_p].reshape(n_batch, t_pad, -1)[:, :t_len]

    def sample_rows(a):
        return a[rows_p:rows_p + rows_s].reshape(dec_batch, dec_seq, -1)

    ones_bd = jnp.asarray(np.kron(np.eye(RWKV_HEADS), np.ones((HEAD_DIM, HEAD_DIM))), BF16)
    pad_cols = lambda a: jnp.pad(a, ((0, 0), (0, RWKV_PAD - RWKV_COLS)))
    init = jnp.zeros((dec_batch, dec_seq, RWKV_PAD), F32).at[:, 0].set(pad_cols(state_shift[layer]))
    init = jnp.concatenate([jnp.zeros((LANE, RWKV_PAD), F32), init.reshape(rows_s, RWKV_PAD)], axis=0)
    wup = jnp.pad(w_up[layer], ((0, ICLR_LORA), (0, 0)))
    aup = jnp.pad(a_up[layer], ((DECAY_LORA, 0), (0, 0)))
    gup = jnp.pad(g_up[layer], ((0, GATE_PAD - GATE_LORA), (0, 0)))
    vec = lambda a: a.reshape(1, -1)
    pre = _rwkv_pre(feat, init, vec(pad_cols(mu_shift[layer][None])), vec(w0[layer]), vec(a0[layer]), vec(k_k[layer]),
                    vec(k_a[layer]), vec(r_k[layer]), wup, aup, gup, ones_bd, rows_p // LANE, n_blocks, dec_seq)
    pre_p, pre_t = pre

    def to_scan(a, nb, steps):
        return a.reshape(nb, steps, RWKV_HEADS, HEAD_DIM).transpose(1, 3, 0, 2).reshape(steps, HEAD_DIM, nb * RWKV_HEADS)

    take_p = lambda a: a.reshape(n_batch, t_pad, -1)
    take_s = lambda a: a[:rows_s].reshape(dec_batch, dec_seq, -1)

    def from_scan(y, nb, steps):
        return y.reshape(steps, HEAD_DIM, nb, RWKV_HEADS).transpose(2, 0, 3, 1).reshape(nb, steps, RWKV_WIDTH)

    def state_in(s):
        nb = s.shape[0]
        return s.transpose(2, 3, 0, 1).reshape(HEAD_DIM, HEAD_DIM, nb * RWKV_HEADS)

    def state_out(s, nb):
        return s.reshape(HEAD_DIM, HEAD_DIM, nb, RWKV_HEADS).transpose(2, 3, 0, 1)

    tc = max(d for d in range(1, 49) if math.gcd(t_len, t_pad) % d == 0)
    y_p, wkv_p = _rwkv_scan(*_to_scan_rows([take_p(a) for a in pre_p[:6]], tc),
                            jnp.zeros((HEAD_DIM, HEAD_DIM, n_batch * RWKV_HEADS), F32), tc, t_len)
    y_p = y_p.reshape(t_pad, HEAD_DIM, RWKV_HEADS, n_batch).transpose(3, 0, 2, 1).reshape(rows_p, RWKV_WIDTH)
    wkv_p = wkv_p.reshape(HEAD_DIM, HEAD_DIM, RWKV_HEADS, n_batch).transpose(3, 2, 0, 1)
    y_s, wkv_s = _rwkv_scan(*[to_scan(take_s(a), dec_batch, dec_seq) for a in pre_t[:6]],
                            state_in(state_wkv[layer]), dec_seq, dec_seq)
    rows_t = rows - rows_p
    pad_tail = lambda a: jnp.pad(a, ((0, rows_t - rows_s), (0, 0)))
    ys = (y_p, pad_tail(from_scan(y_s, dec_batch, dec_seq).reshape(rows_s, RWKV_WIDTH)))

    ya_p = _prompt_attn(qq, kvi, _prompt_bias_tiles(rel_bias), n_batch, n_blocks, min(IDX_TOPK, t_len // 4))

    qq_s, kvi_s = sample_rows(qq), sample_rows(kvi)

    def pad_q(a):
        a = jnp.pad(a.transpose(0, 2, 1, 3), ((0, 0), (0, 0), (0, Q_PAD - dec_seq), (0, 0)))
        return a.reshape(dec_batch, -1, a.shape[-1])

    qi8 = pad_q(qq_s[..., ATTN_WIDTH:].reshape(dec_batch, dec_seq, IDX_HEADS, IDX_DIM))
    wi_s = kvi_s[..., 2 * KV_WIDTH + IDX_DIM:2 * KV_WIDTH + IDX_DIM + IDX_HEADS]
    wb = jnp.broadcast_to(pad_q(wi_s[..., None]), (dec_batch, IDX_HEADS * Q_PAD, IDX_DIM))
    q8 = pad_q(qq_s[..., :ATTN_WIDTH].reshape(dec_batch, dec_seq, ATTN_HEADS, HEAD_DIM))
    pad_keys = lambda a: jnp.pad(a, ((0, 0), (0, PAGE - dec_seq), (0, 0)))
    ki_new = pad_keys(kvi_s[..., 2 * KV_WIDTH:2 * KV_WIDTH + IDX_DIM])
    k_new = pad_keys(kvi_s[..., :KV_WIDTH])
    v_new = pad_keys(kvi_s[..., KV_WIDTH:2 * KV_WIDTH])
    n_pool = cache_k.shape[1]
    pages_t = lambda c: c[layer].transpose(0, 2, 3, 1).reshape(n_pool, KV_WIDTH, PAGE)
    sc = _sample_scores(page_table, qi8, wb, cache_kidx[layer])
    sel = _sample_select(sc, qi8, wb, ki_new, dec_seq, min(IDX_TOPK, (past_len + dec_seq) // 4))
    o_s = _sample_attn(page_table, q8, sel, k_new.transpose(0, 2, 1), v_new.transpose(0, 2, 1),
                       _sample_bias_tiles(rel_bias, past_len, dec_seq), pages_t(cache_k), pages_t(cache_v))
    ya_s = o_s.reshape(dec_batch, ATTN_HEADS, Q_PAD, HEAD_DIM)[:, :, :dec_seq].transpose(0, 2, 1, 3).reshape(rows_s, ATTN_WIDTH)

    wo = w_o[layer].astype(BF16)
    x1, x1b, x1t = _mix(ys, (pre_p[7], pre_t[7]), (pre_p[6], pre_t[6]), (ya_p, pad_tail(ya_s)), xn,
                        vec(gn_g[layer]), vec(gn_b[layer]), ones_bd,
                        wo[:RWKV_WIDTH], wo[RWKV_WIDTH:], vec(ln1_g[layer]), vec(ln1_b[layer]), 256)
    routing = _peer_route(x1b, peer_wq[layer].astype(BF16), peer_subkeys[layer])
    peer_t = _peer_dense(x1t, peer_u[layer].astype(BF16), peer_v[layer].T.astype(BF16), *routing, 512)
    y_p, y_t = _ln_out(x1, peer_t, ln2_g[layer], ln2_b[layer], 256, rows_p)

    last_p = feat[jnp.arange(n_batch) * t_pad + (t_len - 1)]
    last_s = feat[rows_p + jnp.arange(dec_batch) * dec_seq + (dec_seq - 1)]
    kvi_p = prompt_rows(kvi)
    kv4 = lambda a, nb, steps: a.reshape(nb, steps, KV_HEADS, HEAD_DIM)[None]
    return (
        y_p.reshape(n_batch, t_pad, D_MODEL)[:, N_META:t_len], y_t[:rows_s].reshape(dec_batch, dec_seq, D_MODEL),
        kv4(kvi_p[..., :KV_WIDTH], n_batch, t_len), kv4(kvi_p[..., KV_WIDTH:2 * KV_WIDTH], n_batch, t_len),
        kvi_p[..., 2 * KV_WIDTH:2 * KV_WIDTH + IDX_DIM][None],
        wkv_p[None], last_p[:, :RWKV_COLS][None],
        kv4(kvi_s[..., :KV_WIDTH], dec_batch, dec_seq), kv4(kvi_s[..., KV_WIDTH:2 * KV_WIDTH], dec_batch, dec_seq),
        kvi_s[..., 2 * KV_WIDTH:2 * KV_WIDTH + IDX_DIM][None],
        state_out(wkv_s, dec_batch)[None], last_s[:, :RWKV_COLS][None],
    )
```

```python
import functools
import math

import numpy as np
import jax
import jax.numpy as jnp
from jax import lax
from jax.experimental import pallas as pl
from jax.experimental.pallas import tpu as pltpu

F32, BF16, I32 = jnp.float32, jnp.bfloat16, jnp.int32

D_MODEL = 2048
N_META = 16
HEAD_DIM = 64
RWKV_WIDTH = 1024
ATTN_WIDTH = 1024
RWKV_HEADS = 16
ATTN_HEADS = 16
KV_HEADS = 4
GROUP = 4
KV_WIDTH = 256
DECAY_LORA = 64
ICLR_LORA = 64
GATE_LORA = 160
RWKV_COLS = 3 * RWKV_WIDTH + DECAY_LORA + ICLR_LORA + GATE_LORA
RWKV_PAD = 3456
LORA_WA = DECAY_LORA + ICLR_LORA
GATE_PAD = RWKV_PAD - 3 * RWKV_WIDTH - LORA_WA
GN_EPS = 64e-5
IDX_HEADS = 8
IDX_DIM = 128
IDX_TOPK = 256
REL_BUCKETS = 32
REL_MAX_DIST = 128
PEER_HEADS = 8
PEER_NKEYS = 128
PEER_HALF = 128
PEER_TOPK = 16
PEER_EXPERTS = PEER_NKEYS * PEER_NKEYS
DN_ALPHA = 2.0 ** 0.25
LN_EPS = 1e-5
PAGE = 128
LANE = 128
ROW_ALIGN = 2560
VMEM_LIMIT = 56 * 1024 * 1024
NEG_BIG = -1e30


def _params(*sem):
    return pltpu.CompilerParams(dimension_semantics=sem, vmem_limit_bytes=VMEM_LIMIT)


def _dot(a, b):
    return jnp.dot(a, b, preferred_element_type=F32)


def _dot_nt(a, b):
    return lax.dot_general(a, b, (((1,), (1,)), ((), ())), preferred_element_type=F32)


def _split2(x):
    hi = x.astype(BF16)
    lo = (x - hi.astype(F32)).astype(BF16)
    return hi, lo


def _dot_hp(a, b):
    ah, al = _split2(a)
    bh, bl = _split2(b)
    return _dot(ah, bh) + (_dot(ah, bl) + _dot(al, bh))


def _dot_nt_hp(a, b):
    ah, al = _split2(a)
    bh, bl = _split2(b)
    return _dot_nt(ah, bh) + (_dot_nt(ah, bl) + _dot_nt(al, bh))


def _segsum(x, ones_bd):
    hi, lo = _split2(x)
    return _dot(hi, ones_bd) + _dot(lo, ones_bd)


BISECT_STEPS = 40


def _kth_largest_floor(count_ge, lo, hi, n_sel):
    def body(_, bracket):
        lo, hi = bracket
        mid = 0.5 * lo + 0.5 * hi
        enough = count_ge(mid) >= n_sel
        return jnp.where(enough, mid, lo), jnp.where(enough, hi, mid)

    return lax.fori_loop(0, BISECT_STEPS, body, (lo, hi))[0]


def _tie_cutoff(count_eq_below, need, nbits, shape):
    def body(it, cut):
        cand = cut | lax.shift_left(jnp.int32(1), nbits - 1 - it)
        return jnp.where(count_eq_below(cand) <= need, cand, cut)

    return lax.fori_loop(0, nbits, body, jnp.zeros(shape, I32))


def _ln(x, g, b):
    mu = jnp.mean(x, axis=-1, keepdims=True)
    xc = x - mu
    var = jnp.mean(xc * xc, axis=-1, keepdims=True)
    return xc * lax.rsqrt(var + LN_EPS) * g + b


def _ln_in_kernel(x_ref, g_ref, b_ref, xn_ref, xb_ref):
    y = _ln(x_ref[...], g_ref[...], b_ref[...])
    xn_ref[...] = y
    xb_ref[...] = y.astype(BF16)


def _ln_in(x, g, b, tm):
    rows = x.shape[0]
    row = pl.BlockSpec((tm, D_MODEL), lambda i: (i, 0))
    vec = pl.BlockSpec((1, D_MODEL), lambda i: (0, 0))
    return pl.pallas_call(
        _ln_in_kernel,
        grid=(rows // tm,),
        in_specs=[row, vec, vec],
        out_specs=[row, row],
        out_shape=[jax.ShapeDtypeStruct((rows, D_MODEL), F32), jax.ShapeDtypeStruct((rows, D_MODEL), BF16)],
        compiler_params=_params("parallel"),
    )(x, g.reshape(1, -1), b.reshape(1, -1))


def _ln_out_kernel(x_ref, pt_ref, g_ref, b_ref, op_ref, ot_ref, *, n_prompt_tiles):
    y = _ln(DN_ALPHA * x_ref[...] + pt_ref[...].T, g_ref[...], b_ref[...])

    @pl.when(pl.program_id(0) < n_prompt_tiles)
    def _():
        op_ref[...] = y

    @pl.when(pl.program_id(0) >= n_prompt_tiles)
    def _():
        ot_ref[...] = y


def _ln_out(x, pt, g, b, tm, rows_p):
    rows = x.shape[0]
    n_prompt_tiles = rows_p // tm
    row = pl.BlockSpec((tm, D_MODEL), lambda i: (i, 0))
    vec = pl.BlockSpec((1, D_MODEL), lambda i: (0, 0))
    return pl.pallas_call(
        functools.partial(_ln_out_kernel, n_prompt_tiles=n_prompt_tiles),
        grid=(rows // tm,),
        in_specs=[row, pl.BlockSpec((D_MODEL, tm), lambda i: (0, i)), vec, vec],
        out_specs=[pl.BlockSpec((tm, D_MODEL), lambda i: (jnp.minimum(i, n_prompt_tiles - 1), 0)),
                   pl.BlockSpec((tm, D_MODEL), lambda i: (jnp.maximum(i - n_prompt_tiles, 0), 0))],
        out_shape=[jax.ShapeDtypeStruct((rows_p, D_MODEL), F32), jax.ShapeDtypeStruct((rows - rows_p, D_MODEL), F32)],
        compiler_params=_params("arbitrary"),
    )(x, pt, g.reshape(1, -1), b.reshape(1, -1))


def _mm_kernel(x_ref, w_ref, o_ref):
    o_ref[...] = _dot(x_ref[...], w_ref[...])


def _matmul(xb, w, tm, tn):
    m, k = xb.shape
    n = w.shape[1]
    return pl.pallas_call(
        _mm_kernel,
        grid=(m // tm, n // tn),
        in_specs=[pl.BlockSpec((tm, k), lambda i, j: (i, 0)), pl.BlockSpec((k, tn), lambda i, j: (0, j))],
        out_specs=pl.BlockSpec((tm, tn), lambda i, j: (i, j)),
        out_shape=jax.ShapeDtypeStruct((m, n), F32),
        compiler_params=_params("parallel", "arbitrary"),
    )(xb, w)


def _rwkv_pre_kernel(cur_ref, prev8_ref, init_ref, mu_ref, w0_ref, a0_ref, kk_ref, ka_ref, rk_ref,
                     wup_ref, aup_ref, gup_ref, ones_ref, *out_refs, n_prompt_tiles, tiles_per_batch, dec_seq):
    i = pl.program_id(0)
    cur = cur_ref[...]
    row = lax.broadcasted_iota(I32, cur.shape, 0)
    prev = jnp.where(row == 0, jnp.broadcast_to(prev8_ref[7:8, :], cur.shape), pltpu.roll(cur, 1, axis=0))
    batch_start = ((i % tiles_per_batch) == 0).astype(I32)
    first_prompt = jnp.where(row == 0, batch_start, 0)
    first_sample = jnp.where(row % dec_seq == 0, 1, 0)
    first = jnp.where(i < n_prompt_tiles, first_prompt, first_sample)
    prev = jnp.where(first > 0, init_ref[...], prev)

    xm = cur + (prev - cur) * mu_ref[...]
    r = xm[:, 0:RWKV_WIDTH]
    k = xm[:, RWKV_WIDTH:2 * RWKV_WIDTH]
    v = xm[:, 2 * RWKV_WIDTH:3 * RWKV_WIDTH]
    wa = xm[:, 3 * RWKV_WIDTH:3 * RWKV_WIDTH + LORA_WA]
    gl = xm[:, 3 * RWKV_WIDTH + LORA_WA:]
    ones_bd = ones_ref[...]

    nz = -(w0_ref[...] + _dot_hp(jnp.tanh(wa), wup_ref[...]))
    softplus = jnp.maximum(nz, 0.0) + jnp.log1p(jnp.exp(-jnp.abs(nz)))
    decay = jnp.exp(-jnp.exp(-softplus - 0.5))
    a = jax.nn.sigmoid(a0_ref[...] + _dot_hp(wa, aup_ref[...]))
    g = _dot_hp(jax.nn.sigmoid(gl), gup_ref[...])
    kn = k * kk_ref[...]
    kn = kn / jnp.maximum(jnp.sqrt(_segsum(kn * kn, ones_bd)), 1e-12)
    k_h = k * (1.0 + (a - 1.0) * ka_ref[...])
    vals = (r, decay, k_h, v, kn, kn * a, g, _segsum(r * k_h * rk_ref[...], ones_bd) * v)
    n_out = len(vals)

    @pl.when(i < n_prompt_tiles)
    def _():
        for o_ref, val in zip(out_refs[:n_out], vals):
            o_ref[...] = val

    @pl.when(i >= n_prompt_tiles)
    def _():
        for o_ref, val in zip(out_refs[n_out:], vals):
            o_ref[...] = val


def _rwkv_pre(feat, init, mu, w0, a0, k_k, k_a, r_k, wup, aup, gup, ones_bd, n_prompt_tiles, tiles_per_batch, dec_seq):
    tm = LANE
    n_tiles = feat.shape[0] // tm
    vec = lambda n: pl.BlockSpec((1, n), lambda i: (0, 0))
    full = lambda a: pl.BlockSpec(a.shape, lambda i: (0, 0))
    out_p = pl.BlockSpec((tm, RWKV_WIDTH), lambda i: (jnp.minimum(i, n_prompt_tiles - 1), 0))
    out_t = pl.BlockSpec((tm, RWKV_WIDTH), lambda i: (jnp.maximum(i - n_prompt_tiles, 0), 0))
    shape_p = jax.ShapeDtypeStruct((n_prompt_tiles * tm, RWKV_WIDTH), F32)
    shape_t = jax.ShapeDtypeStruct(((n_tiles - n_prompt_tiles) * tm, RWKV_WIDTH), F32)
    kern = functools.partial(_rwkv_pre_kernel, n_prompt_tiles=n_prompt_tiles, tiles_per_batch=tiles_per_batch, dec_seq=dec_seq)
    outs = pl.pallas_call(
        kern,
        grid=(n_tiles,),
        in_specs=[
            pl.BlockSpec((tm, RWKV_PAD), lambda i: (i, 0)),
            pl.BlockSpec((8, RWKV_PAD), lambda i: (jnp.maximum(i * (tm // 8) - 1, 0), 0)),
            pl.BlockSpec((tm, RWKV_PAD), lambda i: (jnp.where(i == n_prompt_tiles, 1, 0), 0)),
            vec(RWKV_PAD), vec(RWKV_WIDTH), vec(RWKV_WIDTH), vec(RWKV_WIDTH), vec(RWKV_WIDTH), vec(RWKV_WIDTH),
            full(wup), full(aup), full(gup), full(ones_bd),
        ],
        out_specs=[out_p] * 8 + [out_t] * 8,
        out_shape=[shape_p] * 8 + [shape_t] * 8,
        compiler_params=_params("arbitrary"),
    )(feat, feat, init, mu, w0, a0, k_k, k_a, r_k, wup, aup, gup, ones_bd)
    return outs[:8], outs[8:]


def _to_scan_kernel(*refs):
    n = len(refs) // 2
    for x_ref, o_ref in zip(refs[:n], refs[n:]):
        for t in range(o_ref.shape[0]):
            x_t = x_ref[:, t, :]
            rows = jnp.concatenate([x_t[:, h * HEAD_DIM:(h + 1) * HEAD_DIM] for h in range(RWKV_HEADS)], axis=0)
            o_ref[t] = rows.T


def _to_scan_rows(arrays, tc):
    n_batch, steps, width = arrays[0].shape
    spec_in = pl.BlockSpec((n_batch, tc, width), lambda c: (0, c, 0))
    spec_out = pl.BlockSpec((tc, HEAD_DIM, LANE), lambda c: (c, 0, 0))
    return pl.pallas_call(
        _to_scan_kernel,
        grid=(steps // tc,),
        in_specs=[spec_in] * len(arrays),
        out_specs=[spec_out] * len(arrays),
        out_shape=[jax.ShapeDtypeStruct((steps, HEAD_DIM, LANE), F32)] * len(arrays),
        compiler_params=_params("parallel"),
    )(*arrays)


def _rwkv_scan_kernel(r_ref, w_ref, k_ref, v_ref, kn_ref, b_ref, s0_ref, y_ref, s_ref, *, n_chunks):
    c_id = pl.program_id(1)

    @pl.when(c_id == 0)
    def _():
        s_ref[...] = s0_ref[...]

    @pl.when(c_id >= n_chunks)
    def _():
        y_ref[...] = jnp.zeros(y_ref.shape, F32)

    def step(t, carry):
        kn_t = kn_ref[t]
        w_t = w_ref[t]
        b_t = b_ref[t]
        k_t = k_ref[t]
        r_t = r_ref[t]

        def value_row(vi, c):
            s_v = s_ref[vi]
            s_kn = jnp.sum(s_v * kn_t, axis=0, keepdims=True)
            s_new = s_v * w_t - s_kn * b_t + v_ref[t, pl.ds(vi, 1), :] * k_t
            s_ref[vi] = s_new
            y_ref[t, pl.ds(vi, 1), :] = jnp.sum(s_new * r_t, axis=0, keepdims=True)
            return c

        return lax.fori_loop(0, HEAD_DIM, value_row, carry, unroll=16)

    @pl.when(c_id < n_chunks)
    def _():
        lax.fori_loop(0, r_ref.shape[0], step, 0)


def _rwkv_scan(r, w, k, v, kn, b, s0, tc, n_steps):
    steps, _, pairs = r.shape
    seq = pl.BlockSpec((tc, HEAD_DIM, LANE), lambda p, c: (c, 0, p))
    state = pl.BlockSpec((HEAD_DIM, HEAD_DIM, LANE), lambda p, c: (0, 0, p))
    return pl.pallas_call(
        functools.partial(_rwkv_scan_kernel, n_chunks=n_steps // tc),
        grid=(pairs // LANE, steps // tc),
        in_specs=[seq] * 6 + [state],
        out_specs=[seq, state],
        out_shape=[jax.ShapeDtypeStruct(r.shape, F32), jax.ShapeDtypeStruct(s0.shape, F32)],
        compiler_params=_params("parallel", "arbitrary"),
    )(r, w, k, v, kn, b, s0)


def _prompt_attn_kernel(qq_ref, kvi_ref, wi_ref, bias_ref, y_ref,
                        vt_ref, kb_ref, kib_ref, qn_ref, key_ref, sel_ref, cut_ref, acc_ref, *, n_blocks, n_sel):
    i = pl.program_id(1)
    n_kb = i + 1
    gq = GROUP * LANE

    @pl.when(i == 0)
    def _():
        t_pad = n_blocks * LANE
        for j in range(n_blocks):
            vt_ref[j] = kvi_ref[j * LANE:(j + 1) * LANE, KV_WIDTH:2 * KV_WIDTH].T.astype(BF16)
        for n in range(KV_HEADS):
            kb_ref[n, 0:t_pad, :] = kvi_ref[:, n * HEAD_DIM:(n + 1) * HEAD_DIM].astype(BF16)
        kib_ref[...] = kvi_ref[:, 2 * KV_WIDTH:2 * KV_WIDTH + IDX_DIM].astype(BF16)
        for j in range(n_blocks, vt_ref.shape[0]):
            vt_ref[j] = jnp.zeros(vt_ref.shape[1:], BF16)
            kb_ref[:, j * LANE:(j + 1) * LANE, :] = jnp.zeros((KV_HEADS, LANE, HEAD_DIM), BF16)

    kpos0 = lax.broadcasted_iota(I32, (LANE, LANE), 0)
    qpos = i * LANE + lax.broadcasted_iota(I32, (LANE, LANE), 1)

    qi_all = jnp.concatenate(
        [qq_ref[:, ATTN_WIDTH + h * IDX_DIM:ATTN_WIDTH + (h + 1) * IDX_DIM] for h in range(IDX_HEADS)], axis=0).astype(BF16)
    w_t = wi_ref[...].T
    w_flat = jnp.concatenate([w_t[h:h + 1, :] for h in range(IDX_HEADS)], axis=1)

    def score_block(j, bounds):
        lo, hi = bounds
        r0 = pl.multiple_of(j * LANE, LANE)
        ki = kib_ref[pl.ds(r0, LANE), :]
        s = jnp.maximum(_dot_nt(ki, qi_all), 0.0) * w_flat
        acc = s[:, 0:LANE]
        for h in range(1, IDX_HEADS):
            acc = acc + s[:, h * LANE:(h + 1) * LANE]
        causal = kpos0 + r0 <= qpos
        key_ref[pl.ds(r0, LANE), :] = jnp.where(causal, acc, -jnp.inf)
        return jnp.minimum(lo, jnp.where(causal, acc, jnp.inf)), jnp.maximum(hi, jnp.where(causal, acc, -jnp.inf))

    lo, hi = lax.fori_loop(0, n_kb, score_block, (jnp.full((LANE, LANE), jnp.inf, F32), jnp.full((LANE, LANE), -jnp.inf, F32)))
    lo = jnp.min(lo, axis=0, keepdims=True)
    hi = jnp.max(hi, axis=0, keepdims=True)

    row1 = (1, LANE)

    def fold(flag, init, combine):
        def body(j, acc):
            r0 = pl.multiple_of(j * LANE, LANE)
            return combine(acc, flag(key_ref[pl.ds(r0, LANE), :], r0))

        return lax.fori_loop(0, n_kb, body, jnp.full((LANE, LANE), init, F32))

    def count(flag):
        return jnp.sum(fold(flag, 0.0, jnp.add), axis=0, keepdims=True)

    floor = _kth_largest_floor(lambda t: count(lambda k, r0: jnp.where(k >= t, 1.0, 0.0)), lo, hi, n_sel)
    thr = jnp.min(fold(lambda k, r0: jnp.where(k >= floor, k, jnp.inf), jnp.inf, jnp.minimum), axis=0, keepdims=True)
    need = n_sel - count(lambda k, r0: jnp.where(k > thr, 1.0, 0.0))
    n_tied = count(lambda k, r0: jnp.where(k == thr, 1.0, 0.0))
    nbits = (n_blocks * LANE).bit_length()
    cut_ref[...] = jnp.full(cut_ref.shape, 2 ** nbits, I32)

    @pl.when(jnp.max(n_tied - need) > 0.0)
    def _():
        cut = _tie_cutoff(
            lambda c: count(lambda k, r0: jnp.where(k == thr, jnp.where(kpos0 + r0 < c, 1.0, 0.0), 0.0)), need, nbits, row1)
        cut_ref[...] = jnp.broadcast_to(cut, cut_ref.shape)

    cut = cut_ref[0:1, :]

    def select_block(j, c):
        r0 = pl.multiple_of(j * LANE, LANE)
        k = key_ref[pl.ds(r0, LANE), :]
        kpos = kpos0 + r0
        chosen = jnp.where(k > thr, 1.0, jnp.where(k == thr, jnp.where(kpos < cut, 1.0, 0.0), 0.0))
        sel_ref[pl.ds(r0, LANE), :] = jnp.where(kpos <= qpos, chosen, 0.0)
        return c

    lax.fori_loop(0, n_kb, select_block, 0)

    @pl.when(n_kb % 2 == 1)
    def _():
        sel_ref[pl.ds(pl.multiple_of(n_kb * LANE, LANE), LANE), :] = jnp.zeros((LANE, LANE), F32)

    for n in range(KV_HEADS):
        q_n = jnp.concatenate(
            [qq_ref[:, (GROUP * n + g) * HEAD_DIM:(GROUP * n + g + 1) * HEAD_DIM] for g in range(GROUP)], axis=0)
        qn_ref[n] = (q_n * HEAD_DIM ** -0.5).astype(BF16)
    acc_ref[...] = jnp.zeros(acc_ref.shape, F32)
    pair = 2 * LANE

    def key_blocks(jj, carry):
        ms, ls = carry
        r0 = pl.multiple_of(jj * pair, pair)
        mask = sel_ref[pl.ds(r0, pair), :] > 0.5
        near = (jnp.clip(i - 2 * jj, 0, 2), jnp.clip(i - 2 * jj - 1, 0, 2))
        new_ms, new_ls = [], []
        for n in range(KV_HEADS):
            bias = jnp.concatenate([bias_ref[n, near[0]], bias_ref[n, near[1]]], axis=0)
            s = _dot_nt(kb_ref[n, pl.ds(r0, pair), :], qn_ref[n]) + bias
            s = jnp.concatenate([jnp.where(mask, s[:, g * LANE:(g + 1) * LANE], NEG_BIG) for g in range(GROUP)], axis=1)
            m_new = jnp.maximum(ms[n], jnp.max(s, axis=0, keepdims=True))
            alpha = jnp.exp(ms[n] - m_new)
            p = jnp.exp(s - m_new)
            new_ls.append(alpha * ls[n] + jnp.sum(p, axis=0, keepdims=True))
            new_ms.append(m_new)
            rows = slice(n * HEAD_DIM, (n + 1) * HEAD_DIM)
            vt = jnp.concatenate([vt_ref[2 * jj, rows, :], vt_ref[2 * jj + 1, rows, :]], axis=1)
            acc_ref[n] = alpha * acc_ref[n] + _dot(vt, p.astype(BF16))
        return tuple(new_ms), tuple(new_ls)

    init = (tuple(jnp.full((1, gq), NEG_BIG, F32) for _ in range(KV_HEADS)),
            tuple(jnp.zeros((1, gq), F32) for _ in range(KV_HEADS)))
    _, ls = lax.fori_loop(0, (n_kb + 1) // 2, key_blocks, init)
    outs = []
    for n in range(KV_HEADS):
        o = acc_ref[n] / ls[n]
        outs += [o[:, g * LANE:(g + 1) * LANE].T for g in range(GROUP)]
    y_ref[...] = jnp.concatenate(outs, axis=1)


def _prompt_attn(qq, kvi, bias_tiles, n_batch, n_blocks, n_sel):
    t_pad = n_blocks * LANE
    n_even = n_blocks + n_blocks % 2
    kern = functools.partial(_prompt_attn_kernel, n_blocks=n_blocks, n_sel=n_sel)
    return pl.pallas_call(
        kern,
        grid=(n_batch, n_blocks),
        in_specs=[
            pl.BlockSpec((LANE, 2 * ATTN_WIDTH), lambda b, i: (b * n_blocks + i, 0)),
            pl.BlockSpec((t_pad, kvi.shape[1]), lambda b, i: (b, 0)),
            pl.BlockSpec((LANE, LANE), lambda b, i: (b * n_blocks + i, (2 * KV_WIDTH + IDX_DIM) // LANE)),
            pl.BlockSpec(bias_tiles.shape, lambda b, i: (0, 0, 0, 0)),
        ],
        out_specs=pl.BlockSpec((LANE, ATTN_WIDTH), lambda b, i: (b * n_blocks + i, 0)),
        out_shape=jax.ShapeDtypeStruct((n_batch * t_pad, ATTN_WIDTH), F32),
        scratch_shapes=[
            pltpu.VMEM((n_even, KV_WIDTH, LANE), BF16),
            pltpu.VMEM((KV_HEADS, n_even * LANE, HEAD_DIM), BF16),
            pltpu.VMEM((t_pad, IDX_DIM), BF16),
            pltpu.VMEM((KV_HEADS, GROUP * LANE, HEAD_DIM), BF16),
            pltpu.VMEM((t_pad, LANE), F32),
            pltpu.VMEM((n_even * LANE, LANE), F32),
            pltpu.VMEM((8, LANE), I32),
            pltpu.VMEM((KV_HEADS, HEAD_DIM, GROUP * LANE), F32),
        ],
        compiler_params=_params("parallel", "arbitrary"),
    )(qq, kvi, kvi, bias_tiles)


PAGES_PER_STEP = 16
Q_PAD = 8


def _sample_score_kernel(pt_ref, qi_ref, wb_ref, *refs):
    page_refs, out_ref = refs[:PAGES_PER_STEP], refs[PAGES_PER_STEP]
    qi = qi_ref[...].astype(BF16)
    wb = wb_ref[...]
    for u in range(PAGES_PER_STEP):
        s = jnp.maximum(_dot_nt(qi, page_refs[u][...].astype(BF16)), 0.0) * wb
        acc = s[0:Q_PAD]
        for h in range(1, IDX_HEADS):
            acc = acc + s[h * Q_PAD:(h + 1) * Q_PAD]
        out_ref[u] = acc


def _sample_scores(page_table, qi8, wb, cache_kidx):
    n_batch, n_pages = page_table.shape
    page_spec = lambda u: pl.BlockSpec((None, PAGE, IDX_DIM), lambda b, s, pt: (pt[b, s * PAGES_PER_STEP + u], 0, 0))
    per_batch = pl.BlockSpec((None, IDX_HEADS * Q_PAD, IDX_DIM), lambda b, s, pt: (b, 0, 0))
    return pl.pallas_call(
        _sample_score_kernel,
        grid_spec=pltpu.PrefetchScalarGridSpec(
            num_scalar_prefetch=1,
            grid=(n_batch, n_pages // PAGES_PER_STEP),
            in_specs=[per_batch, per_batch] + [page_spec(u) for u in range(PAGES_PER_STEP)],
            out_specs=pl.BlockSpec((None, PAGES_PER_STEP, Q_PAD, PAGE), lambda b, s, pt: (b, s, 0, 0)),
        ),
        out_shape=jax.ShapeDtypeStruct((n_batch, n_pages, Q_PAD, PAGE), F32),
        compiler_params=_params("parallel", "arbitrary"),
    )(page_table, qi8, wb, *([cache_kidx] * PAGES_PER_STEP))


def _sample_select_kernel(sc_ref, qi_ref, wb_ref, kin_ref, sel_ref, key_ref, *, n_pages, dec_seq, n_sel):
    qrow = lax.broadcasted_iota(I32, (Q_PAD, PAGE), 0)
    lane = lax.broadcasted_iota(I32, (Q_PAD, PAGE), 1)
    s = jnp.maximum(_dot_nt(qi_ref[...].astype(BF16), kin_ref[...].astype(BF16)), 0.0) * wb_ref[...]
    acc = s[0:Q_PAD]
    for h in range(1, IDX_HEADS):
        acc = acc + s[h * Q_PAD:(h + 1) * Q_PAD]
    new_valid = jnp.where(lane < dec_seq, jnp.where(lane <= qrow, 1, 0), 0) > 0
    key_ref[0:n_pages] = sc_ref[...]
    key_ref[n_pages] = jnp.where(new_valid, acc, -jnp.inf)

    def lane_fold(x, combine, reduce_lanes):
        n_main = x.shape[0] // 8 * 8
        part = x[0:n_main].reshape(8, n_main // 8, Q_PAD, PAGE)
        for axis in (1, 0):
            acc_p = part[:, 0] if axis == 1 else part[0]
            for u in range(1, part.shape[axis]):
                acc_p = combine(acc_p, part[:, u] if axis == 1 else part[u])
            part = acc_p
        for extra in range(n_main, x.shape[0]):
            part = combine(part, x[extra])
        return reduce_lanes(part, axis=1, keepdims=True)

    lane_count = lambda x: lane_fold(x, jnp.add, jnp.sum)
    col1 = (Q_PAD, 1)
    keys = key_ref[...]
    lo = lane_fold(jnp.where(keys == -jnp.inf, jnp.inf, keys), jnp.minimum, jnp.min)
    hi = lane_fold(keys, jnp.maximum, jnp.max)
    floor = _kth_largest_floor(lambda t: lane_count(jnp.where(key_ref[...] >= t, 1.0, 0.0)), lo, hi, n_sel)
    thr = lane_fold(jnp.where(keys >= floor, keys, jnp.inf), jnp.minimum, jnp.min)
    need = n_sel - lane_count(jnp.where(keys > thr, 1.0, 0.0))
    shape3 = (n_pages + 1, Q_PAD, PAGE)
    kidx = lax.broadcasted_iota(I32, shape3, 0) * PAGE + lax.broadcasted_iota(I32, shape3, 2)
    cut = _tie_cutoff(
        lambda c: lane_count(jnp.where(key_ref[...] == thr, jnp.where(kidx < c, 1.0, 0.0), 0.0)),
        need, ((n_pages + 1) * PAGE).bit_length(), col1)
    chosen = jnp.where(keys > thr, 1.0, jnp.where(keys == thr, jnp.where(kidx < cut, 1.0, 0.0), 0.0))
    sel_ref[0:n_pages] = chosen[0:n_pages]
    sel_ref[n_pages] = jnp.where(new_valid, chosen[n_pages], 0.0)


def _sample_select(sc, qi8, wb, ki_new, dec_seq, n_sel):
    n_batch, n_pages = sc.shape[:2]
    kern = functools.partial(_sample_select_kernel, n_pages=n_pages, dec_seq=dec_seq, n_sel=n_sel)
    per_batch = lambda a: pl.BlockSpec((None,) + a.shape[1:], lambda b: (b,) + (0,) * (a.ndim - 1))
    return pl.pallas_call(
        kern,
        grid=(n_batch,),
        in_specs=[per_batch(sc), per_batch(qi8), per_batch(wb), per_batch(ki_new)],
        out_specs=pl.BlockSpec((None, n_pages + 1, Q_PAD, PAGE), lambda b: (b, 0, 0, 0)),
        out_shape=jax.ShapeDtypeStruct((n_batch, n_pages + 1, Q_PAD, PAGE), F32),
        scratch_shapes=[pltpu.VMEM((n_pages + 1, Q_PAD, PAGE), F32)],
        compiler_params=_params("parallel"),
    )(sc, qi8, wb, ki_new)


def _sample_attn_kernel(pt_ref, q_ref, sel_ref, selnew_ref, knew_ref, vnew_ref, bias_ref, *refs, n_steps):
    k_refs = refs[:PAGES_PER_STEP]
    v_refs = refs[PAGES_PER_STEP:2 * PAGES_PER_STEP]
    o_ref, m_ref, l_ref, acc_ref = refs[2 * PAGES_PER_STEP:]
    s_id = pl.program_id(1)
    rows = KV_HEADS * GROUP * Q_PAD
    per_kv = GROUP * Q_PAD
    q = (q_ref[...] * HEAD_DIM ** -0.5).astype(BF16)

    @pl.when(s_id == 0)
    def _():
        m_ref[...] = jnp.full(m_ref.shape, NEG_BIG, F32)
        l_ref[...] = jnp.zeros(l_ref.shape, F32)
        acc_ref[...] = jnp.zeros(acc_ref.shape, F32)

    def attend(k_pages, v_pages, sel_pages, bias_pages):
        s_blocks, sel_blocks = [], []
        for kp, sp, bp in zip(k_pages, sel_pages, bias_pages):
            kb = kp.astype(BF16)
            s = jnp.concatenate(
                [_dot(q[n * per_kv:(n + 1) * per_kv], kb[n * HEAD_DIM:(n + 1) * HEAD_DIM]) for n in range(KV_HEADS)],
                axis=0) + bp
            s_blocks.append(s)
            sel_blocks.append(jnp.concatenate([sp] * (KV_HEADS * GROUP), axis=0) > 0.5)
        s = jnp.concatenate(s_blocks, axis=1)
        sel = jnp.concatenate(sel_blocks, axis=1)
        s = jnp.where(sel, s, NEG_BIG)
        m_old = m_ref[...]
        m_new = jnp.maximum(m_old, jnp.max(s, axis=1, keepdims=True))
        alpha = jnp.exp(m_old - m_new)
        p = jnp.where(sel, jnp.exp(s - m_new), 0.0)
        l_ref[...] = alpha * l_ref[...] + jnp.sum(p, axis=1, keepdims=True)
        m_ref[...] = m_new
        pb = p.astype(BF16)
        pv = None
        for u, vp in enumerate(v_pages):
            vb = vp.astype(BF16)
            pu = pb[:, u * PAGE:(u + 1) * PAGE]
            part = jnp.concatenate(
                [_dot_nt(pu[n * per_kv:(n + 1) * per_kv], vb[n * HEAD_DIM:(n + 1) * HEAD_DIM]) for n in range(KV_HEADS)], axis=0)
            pv = part if pv is None else pv + part
        acc_ref[...] = alpha * acc_ref[...] + pv

    @pl.when(s_id < n_steps)
    def _():
        far, near = bias_ref[0], bias_ref[1]
        biases = [far] * PAGES_PER_STEP
        last = s_id == n_steps - 1
        biases[-1] = jnp.where(last, near, far)
        attend([r[...] for r in k_refs], [r[...] for r in v_refs], [sel_ref[u] for u in range(PAGES_PER_STEP)], biases)

    @pl.when(s_id == n_steps)
    def _():
        attend([knew_ref[...]], [vnew_ref[...]], [selnew_ref[...]], [bias_ref[2]])
        o_ref[...] = acc_ref[...] / l_ref[...]


def _sample_attn(page_table, q8, sel, k_new, v_new, bias_tiles, cache_k, cache_v):
    n_batch, n_pages = page_table.shape
    n_steps = n_pages // PAGES_PER_STEP
    rows = KV_HEADS * GROUP * Q_PAD
    kern = functools.partial(_sample_attn_kernel, n_steps=n_steps)

    def page_spec(u):
        return pl.BlockSpec((None, KV_WIDTH, PAGE),
                            lambda b, s, pt: (pt[b, jnp.minimum(s, n_steps - 1) * PAGES_PER_STEP + u], 0, 0))

    per_batch = lambda a: pl.BlockSpec((None,) + a.shape[1:], lambda b, s, pt: (b,) + (0,) * (a.ndim - 1))
    return pl.pallas_call(
        kern,
        grid_spec=pltpu.PrefetchScalarGridSpec(
            num_scalar_prefetch=1,
            grid=(n_batch, n_steps + 1),
            in_specs=[
                per_batch(q8),
                pl.BlockSpec((None, PAGES_PER_STEP, Q_PAD, PAGE), lambda b, s, pt: (b, jnp.minimum(s, n_steps - 1), 0, 0)),
                pl.BlockSpec((None, None, Q_PAD, PAGE), lambda b, s, pt: (b, n_pages, 0, 0)),
                per_batch(k_new), per_batch(v_new),
                pl.BlockSpec(bias_tiles.shape, lambda b, s, pt: (0, 0, 0)),
            ] + [page_spec(u) for u in range(PAGES_PER_STEP)] * 2,
            out_specs=pl.BlockSpec((None, rows, HEAD_DIM), lambda b, s, pt: (b, 0, 0)),
            scratch_shapes=[pltpu.VMEM((rows, 1), F32), pltpu.VMEM((rows, 1), F32), pltpu.VMEM((rows, HEAD_DIM), F32)],
        ),
        out_shape=jax.ShapeDtypeStruct((n_batch, rows, HEAD_DIM), F32),
        compiler_params=_params("parallel", "arbitrary"),
    )(page_table, q8, sel, sel, k_new, v_new, bias_tiles, *([cache_k] * PAGES_PER_STEP), *([cache_v] * PAGES_PER_STEP))


def _mix_kernel(ysp_ref, yst_ref, bonusp_ref, bonust_ref, gp_ref, gt_ref, yap_ref, yat_ref,
                xn_ref, gng_ref, gnb_ref, ones_ref, wor_ref, woa_ref, lg_ref, lb_ref,
                x1_ref, x1b_ref, x1t_ref, *, n_prompt_tiles):
    is_prompt = pl.program_id(0) < n_prompt_tiles
    pick = lambda p_ref, t_ref: jnp.where(is_prompt, p_ref[...], t_ref[...])
    ones_bd = ones_ref[...]
    ys = pick(ysp_ref, yst_ref)
    inv = 1.0 / HEAD_DIM
    yc = ys - _segsum(ys, ones_bd) * inv
    var = _segsum(yc * yc, ones_bd) * inv
    yr = (yc * lax.rsqrt(var + GN_EPS) * gng_ref[...] + gnb_ref[...] + pick(bonusp_ref, bonust_ref)) * pick(gp_ref, gt_ref)
    mix = _dot(yr.astype(BF16), wor_ref[...]) + _dot(pick(yap_ref, yat_ref).astype(BF16), woa_ref[...])
    x1 = _ln(DN_ALPHA * xn_ref[...] + mix, lg_ref[...], lb_ref[...])
    x1_ref[...] = x1
    x1b_ref[...] = x1.astype(BF16)
    x1t_ref[...] = x1.T.astype(BF16)


def _mix(ys, bonus, g, ya, xn, gn_g, gn_b, ones_bd, wo_r, wo_a, ln_g, ln_b, tm):
    rows = xn.shape[0]
    n_prompt_tiles = ys[0].shape[0] // tm
    half_p = pl.BlockSpec((tm, RWKV_WIDTH), lambda i: (jnp.minimum(i, n_prompt_tiles - 1), 0))
    half_t = pl.BlockSpec((tm, RWKV_WIDTH), lambda i: (jnp.maximum(i - n_prompt_tiles, 0), 0))
    row = pl.BlockSpec((tm, D_MODEL), lambda i: (i, 0))
    vec = lambda n: pl.BlockSpec((1, n), lambda i: (0, 0))
    full = lambda a: pl.BlockSpec(a.shape, lambda i: (0, 0))
    return pl.pallas_call(
        functools.partial(_mix_kernel, n_prompt_tiles=n_prompt_tiles),
        grid=(rows // tm,),
        in_specs=[half_p, half_t] * 4 + [row, vec(RWKV_WIDTH), vec(RWKV_WIDTH), full(ones_bd), full(wo_r), full(wo_a),
                                         vec(D_MODEL), vec(D_MODEL)],
        out_specs=[row, row, pl.BlockSpec((D_MODEL, tm), lambda i: (0, i))],
        out_shape=[jax.ShapeDtypeStruct((rows, D_MODEL), F32), jax.ShapeDtypeStruct((rows, D_MODEL), BF16),
                   jax.ShapeDtypeStruct((D_MODEL, rows), BF16)],
        compiler_params=_params("parallel"),
    )(*ys, *bonus, *g, *ya, xn, gn_g, gn_b, ones_bd, wo_r, wo_a, ln_g, ln_b)


CAND_PAIRS = [(c, d) for c in range(PEER_TOPK) for d in range(PEER_TOPK) if (c + 1) * (d + 1) <= PEER_TOPK]


def _top_rows(x, n):
    rows = []
    for _ in range(n):
        m = jnp.max(x, axis=0, keepdims=True)
        rows.append(m)
        x = jnp.where(x == m, -jnp.inf, x)
    return rows


def _peer_route_kernel(x_ref, wq_ref, sub_ref, tau_ref, f_ref, s2_ref, e2_ref):
    q = _dot(x_ref[...], wq_ref[...])
    for h in range(PEER_HEADS):
        base = h * 2 * PEER_HALF
        s1 = _dot_nt_hp(sub_ref[h, 0], q[:, base:base + PEER_HALF])
        s2 = _dot_nt_hp(sub_ref[h, 1], q[:, base + PEER_HALF:base + 2 * PEER_HALF])
        top1 = _top_rows(s1, PEER_TOPK)
        top2 = _top_rows(s2, PEER_TOPK)
        cand = jnp.concatenate([top1[c] + top2[d] for c, d in CAND_PAIRS]
                               + [jnp.full_like(top1[0], -jnp.inf)] * (-len(CAND_PAIRS) % 8), axis=0)
        best = _top_rows(cand, PEER_TOPK)
        thr = best[-1]
        m = top1[0] + top2[0]
        z = jnp.sum(jnp.where(cand >= thr, jnp.exp(cand - m), 0.0), axis=0, keepdims=True)
        tau = jnp.full(s1.shape, jnp.inf, F32)
        for d in range(PEER_TOPK):
            tau = jnp.where(s1 + top2[d] >= thr, top2[d], tau)
        tau_ref[h] = tau
        f_ref[h] = jnp.exp(s1 - top1[0]) * (1.0 / z)
        s2_ref[h] = s2
        e2_ref[h] = jnp.exp(s2 - top2[0])


def _peer_route(x1b, wq, subkeys):
    rows = x1b.shape[0]
    tm = LANE
    big = pl.BlockSpec((PEER_HEADS, PEER_NKEYS, tm), lambda i: (0, 0, i))
    big_shape = jax.ShapeDtypeStruct((PEER_HEADS, PEER_NKEYS, rows), F32)
    return pl.pallas_call(
        _peer_route_kernel,
        grid=(rows // tm,),
        in_specs=[pl.BlockSpec((tm, D_MODEL), lambda i: (i, 0)),
                  pl.BlockSpec(wq.shape, lambda i: (0, 0)),
                  pl.BlockSpec(subkeys.shape, lambda i: (0, 0, 0, 0))],
        out_specs=[big] * 4,
        out_shape=[big_shape] * 4,
        compiler_params=_params("parallel"),
    )(x1b, wq, subkeys)


EXPERT_ROWS = 8
MXU_DEPTH = 256


def _peer_dense_kernel(xt_ref, u_ref, vt_ref, tau_ref, f_ref, s2_ref, e2_ref, o_ref, *scratch):
    n_sub = EXPERT_ROWS * PEER_NKEYS // MXU_DEPTH
    ht_refs, g_refs, a_refs = scratch[:n_sub], scratch[n_sub:2 * n_sub], scratch[2 * n_sub:]
    j = pl.program_id(1)
    tm = xt_ref.shape[1]

    @pl.when(j == 0)
    def _():
        o_ref[...] = jnp.zeros(o_ref.shape, F32)

    per = MXU_DEPTH // PEER_NKEYS
    half = PEER_NKEYS // 2

    def gate_rows(p):
        return [([tau_ref[h, e:e + 1, :] for h in range(PEER_HEADS)], [f_ref[h, e:e + 1, :] for h in range(PEER_HEADS)])
                for e in range(p * per, (p + 1) * per)]

    def gate_block(p, rows_p, c, jh):
        cols = slice(c, c + LANE)
        jr = slice(jh * half, (jh + 1) * half)
        acc = [None] * per
        for h in range(PEER_HEADS):
            s2, e2 = s2_ref[h, jr, cols], e2_ref[h, jr, cols]
            for e in range(per):
                tau_i, f_i = rows_p[e]
                g = jnp.where(s2 >= tau_i[h][:, cols], e2 * f_i[h][:, cols], 0.0)
                acc[e] = g if acc[e] is None else acc[e] + g
        for e in range(per):
            g_refs[p][e * PEER_NKEYS + jh * half:e * PEER_NKEYS + (jh + 1) * half, cols] = acc[e]

    def gate_blocks(p):
        rows_p = gate_rows(p)
        return [functools.partial(gate_block, p, rows_p, c, jh) for c in range(0, tm, LANE) for jh in range(2)]

    def up_piece(p, k):
        r = p * MXU_DEPTH + k * PEER_NKEYS
        ht_refs[p][k * PEER_NKEYS:(k + 1) * PEER_NKEYS, :] = _dot(u_ref[r:r + PEER_NKEYS, :], xt_ref[...])

    def act(p):
        he = ht_refs[p][...]
        a_refs[p][...] = (0.5 * he * (1.0 + lax.erf(he * (2.0 ** -0.5))) * g_refs[p][...]).astype(BF16)

    down_rows = 512

    def down_piece(p, m):
        r = p * MXU_DEPTH
        rows = slice(m * down_rows, (m + 1) * down_rows)
        o_ref[rows, :] += _dot(vt_ref[rows, r:r + MXU_DEPTH], a_refs[p][...])

    def run(pieces):
        for piece in pieces:
            piece()

    ups = lambda p: [functools.partial(up_piece, p, k) for k in range(per)]
    downs = lambda p: [functools.partial(down_piece, p, m) for m in range(D_MODEL // down_rows)]
    run(gate_blocks(0) + ups(0))
    for p in range(n_sub):
        if p + 1 < n_sub:
            run(gate_blocks(p + 1) + ups(p + 1))
        act(p)
        run(downs(p))


def _peer_dense(x1t, u, vt, tau, f, s2, e2, tm):
    rows = x1t.shape[1]
    eb = EXPERT_ROWS * PEER_NKEYS
    step_rows = pl.BlockSpec((PEER_HEADS, EXPERT_ROWS, tm), lambda i, j: (0, j, i))
    big = pl.BlockSpec((PEER_HEADS, PEER_NKEYS, tm), lambda i, j: (0, 0, i))
    return pl.pallas_call(
        _peer_dense_kernel,
        grid=(rows // tm, PEER_EXPERTS // eb),
        in_specs=[pl.BlockSpec((D_MODEL, tm), lambda i, j: (0, i)),
                  pl.BlockSpec((eb, D_MODEL), lambda i, j: (j, 0)),
                  pl.BlockSpec((D_MODEL, eb), lambda i, j: (0, j)),
                  step_rows, step_rows, big, big],
        out_specs=pl.BlockSpec((D_MODEL, tm), lambda i, j: (0, i)),
        out_shape=jax.ShapeDtypeStruct((D_MODEL, rows), F32),
        scratch_shapes=([pltpu.VMEM((MXU_DEPTH, tm), F32)] * (2 * eb // MXU_DEPTH)
                        + [pltpu.VMEM((MXU_DEPTH, tm), BF16)] * (eb // MXU_DEPTH)),
        compiler_params=_params("parallel", "arbitrary"),
    )(x1t, u, vt, tau, f, s2, e2)


def _rel_buckets(dist):
    max_exact = REL_BUCKETS // 2
    d = np.maximum(dist, 0)
    ratio = np.log(np.maximum(d, 1).astype(np.float32) / np.float32(max_exact)) / np.float32(math.log(REL_MAX_DIST / max_exact))
    log_b = max_exact + (ratio * np.float32(REL_BUCKETS - max_exact)).astype(np.int32)
    return np.where(d < max_exact, d, np.minimum(log_b, REL_BUCKETS - 1)).astype(np.int32)


def _bias_lookup(rel_bias, dist):
    onehot = np.eye(REL_BUCKETS, dtype=np.float32)[_rel_buckets(dist)]
    return jnp.dot(jnp.asarray(onehot), rel_bias.astype(F32), precision=lax.Precision.HIGHEST)


def _prompt_bias_tiles(rel_bias):
    kk = np.arange(LANE)[:, None]
    qq = np.arange(LANE)[None, :]
    tiles = []
    for delta in range(3):
        b = _bias_lookup(rel_bias, delta * LANE + qq - kk)
        b = b.reshape(LANE, LANE, KV_HEADS, GROUP).transpose(2, 0, 3, 1).reshape(KV_HEADS, LANE, GROUP * LANE)
        tiles.append(b)
    return jnp.stack(tiles, axis=1).astype(F32)


def _sample_bias_tiles(rel_bias, past_len, dec_seq):
    q = np.arange(Q_PAD)[:, None]
    off = np.arange(PAGE)[None, :]
    qpos = past_len + np.minimum(q, dec_seq - 1)
    dists = [qpos - 0 * off - (past_len - 2 * PAGE), qpos - (past_len - PAGE + off), qpos - (past_len + np.minimum(off, dec_seq - 1))]
    tiles = []
    for d in dists:
        b = _bias_lookup(rel_bias, d + 0 * off)
        tiles.append(b.transpose(2, 0, 1).reshape(ATTN_HEADS * Q_PAD, PAGE))
    return jnp.stack(tiles).astype(F32)


def kernel(x_prompt, x_sample, cache_k, cache_v, cache_kidx, state_wkv, state_shift, page_table, meta_tokens, ln_in_g, ln_in_b, rel_bias, w_in, mu_shift, w0, w_up, a0, a_up, g_up, k_k, k_a, r_k, gn_g, gn_b, w_o, ln1_g, ln1_b, peer_wq, peer_subkeys, peer_u, peer_v, ln2_g, ln2_b):
    n_batch, seq, _ = x_prompt.shape
    dec_batch, dec_seq, _ = x_sample.shape
    n_pages = page_table.shape[1]
    past_len = n_pages * PAGE
    t_len = seq + N_META
    n_blocks = -(-t_len // LANE)
    t_pad = n_blocks * LANE
    rows_p = n_batch * t_pad
    rows_s = dec_batch * dec_seq
    assert rows_s == LANE and Q_PAD >= dec_seq and n_pages % PAGES_PER_STEP == 0
    assert past_len >= 2 * PAGE + REL_MAX_DIST
    rows = -(-(rows_p + rows_s) // ROW_ALIGN) * ROW_ALIGN
    layer = 0

    meta = jnp.broadcast_to(meta_tokens[None], (n_batch, N_META, D_MODEL))
    xp = jnp.pad(jnp.concatenate([meta, x_prompt], axis=1), ((0, 0), (0, t_pad - t_len), (0, 0)))
    x_all = jnp.concatenate([xp.reshape(rows_p, D_MODEL), x_sample.reshape(rows_s, D_MODEL),
                             jnp.zeros((rows - rows_p - rows_s, D_MODEL), F32)], axis=0)
    xn, xb = _ln_in(x_all, ln_in_g, ln_in_b, 256)

    w = w_in[layer]
    c0 = RWKV_COLS
    w_rwkv = jnp.pad(w[:, :c0], ((0, 0), (0, RWKV_PAD - RWKV_COLS))).astype(BF16)
    w_qq = jnp.concatenate([w[:, c0:c0 + ATTN_WIDTH], w[:, c0 + ATTN_WIDTH + 2 * KV_WIDTH:c0 + 2 * ATTN_WIDTH + 2 * KV_WIDTH]], axis=1).astype(BF16)
    c_ki = c0 + 2 * ATTN_WIDTH + 2 * KV_WIDTH
    w_kvi = jnp.concatenate([w[:, c0 + ATTN_WIDTH:c0 + ATTN_WIDTH + 2 * KV_WIDTH], w[:, c_ki:c_ki + IDX_DIM],
                             jnp.pad(w[:, c_ki + IDX_DIM:], ((0, 0), (0, LANE - IDX_HEADS)))], axis=1).astype(BF16)
    feat = _matmul(xb, w_rwkv, 640, RWKV_PAD // 3)
    qq = _matmul(xb, w_qq, 640, 1024)
    kvi = _matmul(xb, w_kvi, 640, w_kvi.shape[1])

    def prompt_rows(a):
        return a[:rows_p].reshape(n_batch, t_pad, -1)[:, :t_len]

    def sample_rows(a):
        return a[rows_p:rows_p + rows_s].reshape(dec_batch, dec_seq, -1)

    ones_bd = jnp.asarray(np.kron(np.eye(RWKV_HEADS), np.ones((HEAD_DIM, HEAD_DIM))), BF16)
    pad_cols = lambda a: jnp.pad(a, ((0, 0), (0, RWKV_PAD - RWKV_COLS)))
    init = jnp.zeros((dec_batch, dec_seq, RWKV_PAD), F32).at[:, 0].set(pad_cols(state_shift[layer]))
    init = jnp.concatenate([jnp.zeros((LANE, RWKV_PAD), F32), init.reshape(rows_s, RWKV_PAD)], axis=0)
    wup = jnp.pad(w_up[layer], ((0, ICLR_LORA), (0, 0)))
    aup = jnp.pad(a_up[layer], ((DECAY_LORA, 0), (0, 0)))
    gup = jnp.pad(g_up[layer], ((0, GATE_PAD - GATE_LORA), (0, 0)))
    vec = lambda a: a.reshape(1, -1)
    pre = _rwkv_pre(feat, init, vec(pad_cols(mu_shift[layer][None])), vec(w0[layer]), vec(a0[layer]), vec(k_k[layer]),
                    vec(k_a[layer]), vec(r_k[layer]), wup, aup, gup, ones_bd, rows_p // LANE, n_blocks, dec_seq)
    pre_p, pre_t = pre

    def to_scan(a, nb, steps):
        return a.reshape(nb, steps, RWKV_HEADS, HEAD_DIM).transpose(1, 3, 0, 2).reshape(steps, HEAD_DIM, nb * RWKV_HEADS)

    take_p = lambda a: a.reshape(n_batch, t_pad, -1)
    take_s = lambda a: a[:rows_s].reshape(dec_batch, dec_seq, -1)

    def from_scan(y, nb, steps):
        return y.reshape(steps, HEAD_DIM, nb, RWKV_HEADS).transpose(2, 0, 3, 1).reshape(nb, steps, RWKV_WIDTH)

    def state_in(s):
        nb = s.shape[0]
        return s.transpose(2, 3, 0, 1).reshape(HEAD_DIM, HEAD_DIM, nb * RWKV_HEADS)

    def state_out(s, nb):
        return s.reshape(HEAD_DIM, HEAD_DIM, nb, RWKV_HEADS).transpose(2, 3, 0, 1)

    tc = max(d for d in range(1, 49) if math.gcd(t_len, t_pad) % d == 0)
    y_p, wkv_p = _rwkv_scan(*_to_scan_rows([take_p(a) for a in pre_p[:6]], tc),
                            jnp.zeros((HEAD_DIM, HEAD_DIM, n_batch * RWKV_HEADS), F32), tc, t_len)
    y_p = y_p.reshape(t_pad, HEAD_DIM, RWKV_HEADS, n_batch).transpose(3, 0, 2, 1).reshape(rows_p, RWKV_WIDTH)
    wkv_p = wkv_p.reshape(HEAD_DIM, HEAD_DIM, RWKV_HEADS, n_batch).transpose(3, 2, 0, 1)
    y_s, wkv_s = _rwkv_scan(*[to_scan(take_s(a), dec_batch, dec_seq) for a in pre_t[:6]],
                            state_in(state_wkv[layer]), dec_seq, dec_seq)
    rows_t = rows - rows_p
    pad_tail = lambda a: jnp.pad(a, ((0, rows_t - rows_s), (0, 0)))
    ys = (y_p, pad_tail(from_scan(y_s, dec_batch, dec_seq).reshape(rows_s, RWKV_WIDTH)))

    ya_p = _prompt_attn(qq, kvi, _prompt_bias_tiles(rel_bias), n_batch, n_blocks, min(IDX_TOPK, t_len // 4))

    qq_s, kvi_s = sample_rows(qq), sample_rows(kvi)

    def pad_q(a):
        a = jnp.pad(a.transpose(0, 2, 1, 3), ((0, 0), (0, 0), (0, Q_PAD - dec_seq), (0, 0)))
        return a.reshape(dec_batch, -1, a.shape[-1])

    qi8 = pad_q(qq_s[..., ATTN_WIDTH:].reshape(dec_batch, dec_seq, IDX_HEADS, IDX_DIM))
    wi_s = kvi_s[..., 2 * KV_WIDTH + IDX_DIM:2 * KV_WIDTH + IDX_DIM + IDX_HEADS]
    wb = jnp.broadcast_to(pad_q(wi_s[..., None]), (dec_batch, IDX_HEADS * Q_PAD, IDX_DIM))
    q8 = pad_q(qq_s[..., :ATTN_WIDTH].reshape(dec_batch, dec_seq, ATTN_HEADS, HEAD_DIM))
    pad_keys = lambda a: jnp.pad(a, ((0, 0), (0, PAGE - dec_seq), (0, 0)))
    ki_new = pad_keys(kvi_s[..., 2 * KV_WIDTH:2 * KV_WIDTH + IDX_DIM])
    k_new = pad_keys(kvi_s[..., :KV_WIDTH])
    v_new = pad_keys(kvi_s[..., KV_WIDTH:2 * KV_WIDTH])
    n_pool = cache_k.shape[1]
    pages_t = lambda c: c[layer].transpose(0, 2, 3, 1).reshape(n_pool, KV_WIDTH, PAGE)
    sc = _sample_scores(page_table, qi8, wb, cache_kidx[layer])
    sel = _sample_select(sc, qi8, wb, ki_new, dec_seq, min(IDX_TOPK, (past_len + dec_seq) // 4))
    o_s = _sample_attn(page_table, q8, sel, k_new.transpose(0, 2, 1), v_new.transpose(0, 2, 1),
                       _sample_bias_tiles(rel_bias, past_len, dec_seq), pages_t(cache_k), pages_t(cache_v))
    ya_s = o_s.reshape(dec_batch, ATTN_HEADS, Q_PAD, HEAD_DIM)[:, :, :dec_seq].transpose(0, 2, 1, 3).reshape(rows_s, ATTN_WIDTH)

    wo = w_o[layer].astype(BF16)
    x1, x1b, x1t = _mix(ys, (pre_p[7], pre_t[7]), (pre_p[6], pre_t[6]), (ya_p, pad_tail(ya_s)), xn,
                        vec(gn_g[layer]), vec(gn_b[layer]), ones_bd,
                        wo[:RWKV_WIDTH], wo[RWKV_WIDTH:], vec(ln1_g[layer]), vec(ln1_b[layer]), 256)
    routing = _peer_route(x1b, peer_wq[layer].astype(BF16), peer_subkeys[layer])
    peer_t = _peer_dense(x1t, peer_u[layer].astype(BF16), peer_v[layer].T.astype(BF16), *routing, 512)
    y_p, y_t = _ln_out(x1, peer_t, ln2_g[layer], ln2_b[layer], 256, rows_p)

    last_p = feat[jnp.arange(n_batch) * t_pad + (t_len - 1)]
    last_s = feat[rows_p + jnp.arange(dec_batch) * dec_seq + (dec_seq - 1)]
    kvi_p = prompt_rows(kvi)
    kv4 = lambda a, nb, steps: a.reshape(nb, steps, KV_HEADS, HEAD_DIM)[None]
    return (
        y_p.reshape(n_batch, t_pad, D_MODEL)[:, N_META:t_len], y_t[:rows_s].reshape(dec_batch, dec_seq, D_MODEL),
        kv4(kvi_p[..., :KV_WIDTH], n_batch, t_len), kv4(kvi_p[..., KV_WIDTH:2 * KV_WIDTH], n_batch, t_len),
        kvi_p[..., 2 * KV_WIDTH:2 * KV_WIDTH + IDX_DIM][None],
        wkv_p[None], last_p[:, :RWKV_COLS][None],
        kv4(kvi_s[..., :KV_WIDTH], dec_batch, dec_seq), kv4(kvi_s[..., KV_WIDTH:2 * KV_WIDTH], dec_batch, dec_seq),
        kvi_s[..., 2 * KV_WIDTH:2 * KV_WIDTH + IDX_DIM][None],
        state_out(wkv_s, dec_batch)[None], last_s[:, :RWKV_COLS][None],
    )
```

```python
import functools
import math

import numpy as np
import jax
import jax.numpy as jnp
from jax import lax
from jax.experimental import pallas as pl
from jax.experimental.pallas import tpu as pltpu

F32, BF16, I32 = jnp.float32, jnp.bfloat16, jnp.int32

D_MODEL = 2048
N_META = 16
HEAD_DIM = 64
RWKV_WIDTH = 1024
ATTN_WIDTH = 1024
RWKV_HEADS = 16
ATTN_HEADS = 16
KV_HEADS = 4
GROUP = 4
KV_WIDTH = 256
DECAY_LORA = 64
ICLR_LORA = 64
GATE_LORA = 160
RWKV_COLS = 3 * RWKV_WIDTH + DECAY_LORA + ICLR_LORA + GATE_LORA
RWKV_PAD = 3456
LORA_WA = DECAY_LORA + ICLR_LORA
GATE_PAD = RWKV_PAD - 3 * RWKV_WIDTH - LORA_WA
GN_EPS = 64e-5
IDX_HEADS = 8
IDX_DIM = 128
IDX_TOPK = 256
REL_BUCKETS = 32
REL_MAX_DIST = 128
PEER_HEADS = 8
PEER_NKEYS = 128
PEER_HALF = 128
PEER_TOPK = 16
PEER_EXPERTS = PEER_NKEYS * PEER_NKEYS
DN_ALPHA = 2.0 ** 0.25
LN_EPS = 1e-5
PAGE = 128
LANE = 128
ROW_ALIGN = 2560
VMEM_LIMIT = 56 * 1024 * 1024
NEG_BIG = -1e30


def _params(*sem):
    return pltpu.CompilerParams(dimension_semantics=sem, vmem_limit_bytes=VMEM_LIMIT)


def _dot(a, b):
    return jnp.dot(a, b, preferred_element_type=F32)


def _dot_nt(a, b):
    return lax.dot_general(a, b, (((1,), (1,)), ((), ())), preferred_element_type=F32)


def _split2(x):
    hi = x.astype(BF16)
    lo = (x - hi.astype(F32)).astype(BF16)
    return hi, lo


def _dot_hp(a, b):
    ah, al = _split2(a)
    bh, bl = _split2(b)
    return _dot(ah, bh) + (_dot(ah, bl) + _dot(al, bh))


def _dot_nt_hp(a, b):
    ah, al = _split2(a)
    bh, bl = _split2(b)
    return _dot_nt(ah, bh) + (_dot_nt(ah, bl) + _dot_nt(al, bh))


def _segsum(x, ones_bd):
    hi, lo = _split2(x)
    return _dot(hi, ones_bd) + _dot(lo, ones_bd)


BISECT_STEPS = 40


def _kth_largest_floor(count_ge, lo, hi, n_sel):
    def body(_, bracket):
        lo, hi = bracket
        mid = 0.5 * lo + 0.5 * hi
        enough = count_ge(mid) >= n_sel
        return jnp.where(enough, mid, lo), jnp.where(enough, hi, mid)

    return lax.fori_loop(0, BISECT_STEPS, body, (lo, hi))[0]


def _tie_cutoff(count_eq_below, need, nbits, shape):
    def body(it, cut):
        cand = cut | lax.shift_left(jnp.int32(1), nbits - 1 - it)
        return jnp.where(count_eq_below(cand) <= need, cand, cut)

    return lax.fori_loop(0, nbits, body, jnp.zeros(shape, I32))


def _ln(x, g, b):
    mu = jnp.mean(x, axis=-1, keepdims=True)
    xc = x - mu
    var = jnp.mean(xc * xc, axis=-1, keepdims=True)
    return xc * lax.rsqrt(var + LN_EPS) * g + b


def _ln_in_kernel(x_ref, g_ref, b_ref, xn_ref, xb_ref):
    y = _ln(x_ref[...], g_ref[...], b_ref[...])
    xn_ref[...] = y
    xb_ref[...] = y.astype(BF16)


def _ln_in(x, g, b, tm):
    rows = x.shape[0]
    row = pl.BlockSpec((tm, D_MODEL), lambda i: (i, 0))
    vec = pl.BlockSpec((1, D_MODEL), lambda i: (0, 0))
    return pl.pallas_call(
        _ln_in_kernel,
        grid=(rows // tm,),
        in_specs=[row, vec, vec],
        out_specs=[row, row],
        out_shape=[jax.ShapeDtypeStruct((rows, D_MODEL), F32), jax.ShapeDtypeStruct((rows, D_MODEL), BF16)],
        compiler_params=_params("parallel"),
    )(x, g.reshape(1, -1), b.reshape(1, -1))


def _ln_out_kernel(x_ref, pt_ref, g_ref, b_ref, op_ref, ot_ref, *, n_prompt_tiles):
    y = _ln(DN_ALPHA * x_ref[...] + pt_ref[...].T, g_ref[...], b_ref[...])

    @pl.when(pl.program_id(0) < n_prompt_tiles)
    def _():
        op_ref[...] = y

    @pl.when(pl.program_id(0) >= n_prompt_tiles)
    def _():
        ot_ref[...] = y


def _ln_out(x, pt, g, b, tm, rows_p):
    rows = x.shape[0]
    n_prompt_tiles = rows_p // tm
    row = pl.BlockSpec((tm, D_MODEL), lambda i: (i, 0))
    vec = pl.BlockSpec((1, D_MODEL), lambda i: (0, 0))
    return pl.pallas_call(
        functools.partial(_ln_out_kernel, n_prompt_tiles=n_prompt_tiles),
        grid=(rows // tm,),
        in_specs=[row, pl.BlockSpec((D_MODEL, tm), lambda i: (0, i)), vec, vec],
        out_specs=[pl.BlockSpec((tm, D_MODEL), lambda i: (jnp.minimum(i, n_prompt_tiles - 1), 0)),
                   pl.BlockSpec((tm, D_MODEL), lambda i: (jnp.maximum(i - n_prompt_tiles, 0), 0))],
        out_shape=[jax.ShapeDtypeStruct((rows_p, D_MODEL), F32), jax.ShapeDtypeStruct((rows - rows_p, D_MODEL), F32)],
        compiler_params=_params("arbitrary"),
    )(x, pt, g.reshape(1, -1), b.reshape(1, -1))


def _mm_kernel(x_ref, w_ref, o_ref):
    o_ref[...] = _dot(x_ref[...], w_ref[...])


def _matmul(xb, w, tm, tn):
    m, k = xb.shape
    n = w.shape[1]
    return pl.pallas_call(
        _mm_kernel,
        grid=(m // tm, n // tn),
        in_specs=[pl.BlockSpec((tm, k), lambda i, j: (i, 0)), pl.BlockSpec((k, tn), lambda i, j: (0, j))],
        out_specs=pl.BlockSpec((tm, tn), lambda i, j: (i, j)),
        out_shape=jax.ShapeDtypeStruct((m, n), F32),
        compiler_params=_params("parallel", "arbitrary"),
    )(xb, w)


def _rwkv_pre_kernel(cur_ref, prev8_ref, init_ref, mu_ref, w0_ref, a0_ref, kk_ref, ka_ref, rk_ref,
                     wup_ref, aup_ref, gup_ref, ones_ref, *out_refs, n_prompt_tiles, tiles_per_batch, dec_seq):
    i = pl.program_id(0)
    cur = cur_ref[...]
    row = lax.broadcasted_iota(I32, cur.shape, 0)
    prev = jnp.where(row == 0, jnp.broadcast_to(prev8_ref[7:8, :], cur.shape), pltpu.roll(cur, 1, axis=0))
    batch_start = ((i % tiles_per_batch) == 0).astype(I32)
    first_prompt = jnp.where(row == 0, batch_start, 0)
    first_sample = jnp.where(row % dec_seq == 0, 1, 0)
    first = jnp.where(i < n_prompt_tiles, first_prompt, first_sample)
    prev = jnp.where(first > 0, init_ref[...], prev)

    xm = cur + (prev - cur) * mu_ref[...]
    r = xm[:, 0:RWKV_WIDTH]
    k = xm[:, RWKV_WIDTH:2 * RWKV_WIDTH]
    v = xm[:, 2 * RWKV_WIDTH:3 * RWKV_WIDTH]
    wa = xm[:, 3 * RWKV_WIDTH:3 * RWKV_WIDTH + LORA_WA]
    gl = xm[:, 3 * RWKV_WIDTH + LORA_WA:]
    ones_bd = ones_ref[...]

    nz = -(w0_ref[...] + _dot_hp(jnp.tanh(wa), wup_ref[...]))
    softplus = jnp.maximum(nz, 0.0) + jnp.log1p(jnp.exp(-jnp.abs(nz)))
    decay = jnp.exp(-jnp.exp(-softplus - 0.5))
    a = jax.nn.sigmoid(a0_ref[...] + _dot_hp(wa, aup_ref[...]))
    g = _dot_hp(jax.nn.sigmoid(gl), gup_ref[...])
    kn = k * kk_ref[...]
    kn = kn / jnp.maximum(jnp.sqrt(_segsum(kn * kn, ones_bd)), 1e-12)
    k_h = k * (1.0 + (a - 1.0) * ka_ref[...])
    vals = (r, decay, k_h, v, kn, kn * a, g, _segsum(r * k_h * rk_ref[...], ones_bd) * v)
    n_out = len(vals)

    @pl.when(i < n_prompt_tiles)
    def _():
        for o_ref, val in zip(out_refs[:n_out], vals):
            o_ref[...] = val

    @pl.when(i >= n_prompt_tiles)
    def _():
        for o_ref, val in zip(out_refs[n_out:], vals):
            o_ref[...] = val


def _rwkv_pre(feat, init, mu, w0, a0, k_k, k_a, r_k, wup, aup, gup, ones_bd, n_prompt_tiles, tiles_per_batch, dec_seq):
    tm = LANE
    n_tiles = feat.shape[0] // tm
    vec = lambda n: pl.BlockSpec((1, n), lambda i: (0, 0))
    full = lambda a: pl.BlockSpec(a.shape, lambda i: (0, 0))
    out_p = pl.BlockSpec((tm, RWKV_WIDTH), lambda i: (jnp.minimum(i, n_prompt_tiles - 1), 0))
    out_t = pl.BlockSpec((tm, RWKV_WIDTH), lambda i: (jnp.maximum(i - n_prompt_tiles, 0), 0))
    shape_p = jax.ShapeDtypeStruct((n_prompt_tiles * tm, RWKV_WIDTH), F32)
    shape_t = jax.ShapeDtypeStruct(((n_tiles - n_prompt_tiles) * tm, RWKV_WIDTH), F32)
    kern = functools.partial(_rwkv_pre_kernel, n_prompt_tiles=n_prompt_tiles, tiles_per_batch=tiles_per_batch, dec_seq=dec_seq)
    outs = pl.pallas_call(
        kern,
        grid=(n_tiles,),
        in_specs=[
            pl.BlockSpec((tm, RWKV_PAD), lambda i: (i, 0)),
            pl.BlockSpec((8, RWKV_PAD), lambda i: (jnp.maximum(i * (tm // 8) - 1, 0), 0)),
            pl.BlockSpec((tm, RWKV_PAD), lambda i: (jnp.where(i == n_prompt_tiles, 1, 0), 0)),
            vec(RWKV_PAD), vec(RWKV_WIDTH), vec(RWKV_WIDTH), vec(RWKV_WIDTH), vec(RWKV_WIDTH), vec(RWKV_WIDTH),
            full(wup), full(aup), full(gup), full(ones_bd),
        ],
        out_specs=[out_p] * 8 + [out_t] * 8,
        out_shape=[shape_p] * 8 + [shape_t] * 8,
        compiler_params=_params("arbitrary"),
    )(feat, feat, init, mu, w0, a0, k_k, k_a, r_k, wup, aup, gup, ones_bd)
    return outs[:8], outs[8:]


def _to_scan_kernel(*refs):
    n = len(refs) // 2
    for x_ref, o_ref in zip(refs[:n], refs[n:]):
        for t in range(o_ref.shape[0]):
            x_t = x_ref[:, t, :]
            rows = jnp.concatenate([x_t[:, h * HEAD_DIM:(h + 1) * HEAD_DIM] for h in range(RWKV_HEADS)], axis=0)
            o_ref[t] = rows.T


def _to_scan_rows(arrays, tc):
    n_batch, steps, width = arrays[0].shape
    spec_in = pl.BlockSpec((n_batch, tc, width), lambda c: (0, c, 0))
    spec_out = pl.BlockSpec((tc, HEAD_DIM, LANE), lambda c: (c, 0, 0))
    return pl.pallas_call(
        _to_scan_kernel,
        grid=(steps // tc,),
        in_specs=[spec_in] * len(arrays),
        out_specs=[spec_out] * len(arrays),
        out_shape=[jax.ShapeDtypeStruct((steps, HEAD_DIM, LANE), F32)] * len(arrays),
        compiler_params=_params("parallel"),
    )(*arrays)


def _rwkv_scan_kernel(r_ref, w_ref, k_ref, v_ref, kn_ref, b_ref, s0_ref, y_ref, s_ref, *, n_chunks):
    c_id = pl.program_id(1)

    @pl.when(c_id == 0)
    def _():
        s_ref[...] = s0_ref[...]

    @pl.when(c_id >= n_chunks)
    def _():
        y_ref[...] = jnp.zeros(y_ref.shape, F32)

    def step(t, carry):
        kn_t = kn_ref[t]
        w_t = w_ref[t]
        b_t = b_ref[t]
        k_t = k_ref[t]
        r_t = r_ref[t]

        def value_row(vi, c):
            s_v = s_ref[vi]
            s_kn = jnp.sum(s_v * kn_t, axis=0, keepdims=True)
            s_new = s_v * w_t - s_kn * b_t + v_ref[t, pl.ds(vi, 1), :] * k_t
            s_ref[vi] = s_new
            y_ref[t, pl.ds(vi, 1), :] = jnp.sum(s_new * r_t, axis=0, keepdims=True)
            return c

        return lax.fori_loop(0, HEAD_DIM, value_row, carry, unroll=16)

    @pl.when(c_id < n_chunks)
    def _():
        lax.fori_loop(0, r_ref.shape[0], step, 0)


def _rwkv_scan(r, w, k, v, kn, b, s0, tc, n_steps):
    steps, _, pairs = r.shape
    seq = pl.BlockSpec((tc, HEAD_DIM, LANE), lambda p, c: (c, 0, p))
    state = pl.BlockSpec((HEAD_DIM, HEAD_DIM, LANE), lambda p, c: (0, 0, p))
    return pl.pallas_call(
        functools.partial(_rwkv_scan_kernel, n_chunks=n_steps // tc),
        grid=(pairs // LANE, steps // tc),
        in_specs=[seq] * 6 + [state],
        out_specs=[seq, state],
        out_shape=[jax.ShapeDtypeStruct(r.shape, F32), jax.ShapeDtypeStruct(s0.shape, F32)],
        compiler_params=_params("parallel", "arbitrary"),
    )(r, w, k, v, kn, b, s0)


KEY_BLOCKS_PER_PASS = 4


def _prompt_attn_kernel(qq_ref, kvi_ref, wi_ref, bias_ref, y_ref,
                        vt_ref, kb_ref, kib_ref, qn_ref, key_ref, sel_ref, cut_ref, acc_ref, *, n_blocks, n_sel):
    i = pl.program_id(1)
    n_kb = i + 1
    gq = GROUP * LANE

    @pl.when(i == 0)
    def _():
        t_pad = n_blocks * LANE
        for j in range(n_blocks):
            vt_ref[j] = kvi_ref[j * LANE:(j + 1) * LANE, KV_WIDTH:2 * KV_WIDTH].T.astype(BF16)
        for n in range(KV_HEADS):
            kb_ref[n, 0:t_pad, :] = kvi_ref[:, n * HEAD_DIM:(n + 1) * HEAD_DIM].astype(BF16)
        kib_ref[...] = kvi_ref[:, 2 * KV_WIDTH:2 * KV_WIDTH + IDX_DIM].astype(BF16)
        for j in range(n_blocks, vt_ref.shape[0]):
            vt_ref[j] = jnp.zeros(vt_ref.shape[1:], BF16)
            kb_ref[:, j * LANE:(j + 1) * LANE, :] = jnp.zeros((KV_HEADS, LANE, HEAD_DIM), BF16)

    kpos0 = lax.broadcasted_iota(I32, (LANE, LANE), 0)
    qpos = i * LANE + lax.broadcasted_iota(I32, (LANE, LANE), 1)

    qi_all = jnp.concatenate(
        [qq_ref[:, ATTN_WIDTH + h * IDX_DIM:ATTN_WIDTH + (h + 1) * IDX_DIM] for h in range(IDX_HEADS)], axis=0).astype(BF16)
    w_t = wi_ref[...].T
    w_flat = jnp.concatenate([w_t[h:h + 1, :] for h in range(IDX_HEADS)], axis=1)

    def score_block(j, bounds):
        lo, hi = bounds
        r0 = pl.multiple_of(j * LANE, LANE)
        ki = kib_ref[pl.ds(r0, LANE), :]
        s = jnp.maximum(_dot_nt(ki, qi_all), 0.0) * w_flat
        acc = s[:, 0:LANE]
        for h in range(1, IDX_HEADS):
            acc = acc + s[:, h * LANE:(h + 1) * LANE]
        causal = kpos0 + r0 <= qpos
        key_ref[pl.ds(r0, LANE), :] = jnp.where(causal, acc, -jnp.inf)
        return jnp.minimum(lo, jnp.where(causal, acc, jnp.inf)), jnp.maximum(hi, jnp.where(causal, acc, -jnp.inf))

    lo, hi = lax.fori_loop(0, n_kb, score_block, (jnp.full((LANE, LANE), jnp.inf, F32), jnp.full((LANE, LANE), -jnp.inf, F32)))
    lo = jnp.min(lo, axis=0, keepdims=True)
    hi = jnp.max(hi, axis=0, keepdims=True)

    row1 = (1, LANE)

    def fold(flag, init, combine):
        def body(j, acc):
            r0 = pl.multiple_of(j * LANE, LANE)
            return combine(acc, flag(key_ref[pl.ds(r0, LANE), :], r0))

        return lax.fori_loop(0, n_kb, body, jnp.full((LANE, LANE), init, F32))

    def count(flag):
        return jnp.sum(fold(flag, 0.0, jnp.add), axis=0, keepdims=True)

    floor = _kth_largest_floor(lambda t: count(lambda k, r0: jnp.where(k >= t, 1.0, 0.0)), lo, hi, n_sel)
    thr = jnp.min(fold(lambda k, r0: jnp.where(k >= floor, k, jnp.inf), jnp.inf, jnp.minimum), axis=0, keepdims=True)
    need = n_sel - count(lambda k, r0: jnp.where(k > thr, 1.0, 0.0))
    n_tied = count(lambda k, r0: jnp.where(k == thr, 1.0, 0.0))
    nbits = (n_blocks * LANE).bit_length()
    cut_ref[...] = jnp.full(cut_ref.shape, 2 ** nbits, I32)

    @pl.when(jnp.max(n_tied - need) > 0.0)
    def _():
        cut = _tie_cutoff(
            lambda c: count(lambda k, r0: jnp.where(k == thr, jnp.where(kpos0 + r0 < c, 1.0, 0.0), 0.0)), need, nbits, row1)
        cut_ref[...] = jnp.broadcast_to(cut, cut_ref.shape)

    cut = cut_ref[0:1, :]

    def select_block(j, c):
        r0 = pl.multiple_of(j * LANE, LANE)
        k = key_ref[pl.ds(r0, LANE), :]
        kpos = kpos0 + r0
        chosen = jnp.where(k > thr, 1.0, jnp.where(k == thr, jnp.where(kpos < cut, 1.0, 0.0), 0.0))
        sel_ref[pl.ds(r0, LANE), :] = jnp.where(kpos <= qpos, chosen, 0.0)
        return c

    lax.fori_loop(0, n_kb, select_block, 0)

    per_pass = KEY_BLOCKS_PER_PASS
    n_pass = (n_kb + per_pass - 1) // per_pass
    for extra in range(per_pass - 1):
        @pl.when(n_kb + extra < n_pass * per_pass)
        def _(extra=extra):
            sel_ref[pl.ds(pl.multiple_of((n_kb + extra) * LANE, LANE), LANE), :] = jnp.zeros((LANE, LANE), F32)

    for n in range(KV_HEADS):
        q_n = jnp.concatenate(
            [qq_ref[:, (GROUP * n + g) * HEAD_DIM:(GROUP * n + g + 1) * HEAD_DIM] for g in range(GROUP)], axis=0)
        qn_ref[n] = (q_n * HEAD_DIM ** -0.5).astype(BF16)
    acc_ref[...] = jnp.zeros(acc_ref.shape, F32)
    pair = per_pass * LANE

    def key_blocks(jj, carry):
        ms, ls = carry
        r0 = pl.multiple_of(jj * pair, pair)
        mask = sel_ref[pl.ds(r0, pair), :] > 0.5
        near = [jnp.clip(i - per_pass * jj - u, 0, 2) for u in range(per_pass)]
        new_ms, new_ls = [], []
        for n in range(KV_HEADS):
            bias = jnp.concatenate([bias_ref[n, d] for d in near], axis=0)
            s = _dot_nt(kb_ref[n, pl.ds(r0, pair), :], qn_ref[n]) + bias
            s = jnp.concatenate([jnp.where(mask, s[:, g * LANE:(g + 1) * LANE], NEG_BIG) for g in range(GROUP)], axis=1)
            m_new = jnp.maximum(ms[n], jnp.max(s, axis=0, keepdims=True))
            alpha = jnp.exp(ms[n] - m_new)
            p = jnp.exp(s - m_new)
            new_ls.append(alpha * ls[n] + jnp.sum(p, axis=0, keepdims=True))
            new_ms.append(m_new)
            rows = slice(n * HEAD_DIM, (n + 1) * HEAD_DIM)
            vt = jnp.concatenate([vt_ref[per_pass * jj + u, rows, :] for u in range(per_pass)], axis=1)
            acc_ref[n] = alpha * acc_ref[n] + _dot(vt, p.astype(BF16))
        return tuple(new_ms), tuple(new_ls)

    init = (tuple(jnp.full((1, gq), NEG_BIG, F32) for _ in range(KV_HEADS)),
            tuple(jnp.zeros((1, gq), F32) for _ in range(KV_HEADS)))
    _, ls = lax.fori_loop(0, n_pass, key_blocks, init)
    outs = []
    for n in range(KV_HEADS):
        o = acc_ref[n] / ls[n]
        outs += [o[:, g * LANE:(g + 1) * LANE].T for g in range(GROUP)]
    y_ref[...] = jnp.concatenate(outs, axis=1)


def _prompt_attn(qq, kvi, bias_tiles, n_batch, n_blocks, n_sel):
    t_pad = n_blocks * LANE
    n_even = -(-n_blocks // KEY_BLOCKS_PER_PASS) * KEY_BLOCKS_PER_PASS
    kern = functools.partial(_prompt_attn_kernel, n_blocks=n_blocks, n_sel=n_sel)
    return pl.pallas_call(
        kern,
        grid=(n_batch, n_blocks),
        in_specs=[
            pl.BlockSpec((LANE, 2 * ATTN_WIDTH), lambda b, i: (b * n_blocks + i, 0)),
            pl.BlockSpec((t_pad, kvi.shape[1]), lambda b, i: (b, 0)),
            pl.BlockSpec((LANE, LANE), lambda b, i: (b * n_blocks + i, (2 * KV_WIDTH + IDX_DIM) // LANE)),
            pl.BlockSpec(bias_tiles.shape, lambda b, i: (0, 0, 0, 0)),
        ],
        out_specs=pl.BlockSpec((LANE, ATTN_WIDTH), lambda b, i: (b * n_blocks + i, 0)),
        out_shape=jax.ShapeDtypeStruct((n_batch * t_pad, ATTN_WIDTH), F32),
        scratch_shapes=[
            pltpu.VMEM((n_even, KV_WIDTH, LANE), BF16),
            pltpu.VMEM((KV_HEADS, n_even * LANE, HEAD_DIM), BF16),
            pltpu.VMEM((t_pad, IDX_DIM), BF16),
            pltpu.VMEM((KV_HEADS, GROUP * LANE, HEAD_DIM), BF16),
            pltpu.VMEM((t_pad, LANE), F32),
            pltpu.VMEM((n_even * LANE, LANE), F32),
            pltpu.VMEM((8, LANE), I32),
            pltpu.VMEM((KV_HEADS, HEAD_DIM, GROUP * LANE), F32),
        ],
        compiler_params=_params("parallel", "arbitrary"),
    )(qq, kvi, kvi, bias_tiles)


PAGES_PER_STEP = 16
Q_PAD = 8


def _sample_score_kernel(pt_ref, qi_ref, wb_ref, *refs):
    page_refs, out_ref = refs[:PAGES_PER_STEP], refs[PAGES_PER_STEP]
    qi = qi_ref[...].astype(BF16)
    wb = wb_ref[...]
    for u in range(PAGES_PER_STEP):
        s = jnp.maximum(_dot_nt(qi, page_refs[u][...].astype(BF16)), 0.0) * wb
        acc = s[0:Q_PAD]
        for h in range(1, IDX_HEADS):
            acc = acc + s[h * Q_PAD:(h + 1) * Q_PAD]
        out_ref[u] = acc


def _sample_scores(page_table, qi8, wb, cache_kidx):
    n_batch, n_pages = page_table.shape
    page_spec = lambda u: pl.BlockSpec((None, PAGE, IDX_DIM), lambda b, s, pt: (pt[b, s * PAGES_PER_STEP + u], 0, 0))
    per_batch = pl.BlockSpec((None, IDX_HEADS * Q_PAD, IDX_DIM), lambda b, s, pt: (b, 0, 0))
    return pl.pallas_call(
        _sample_score_kernel,
        grid_spec=pltpu.PrefetchScalarGridSpec(
            num_scalar_prefetch=1,
            grid=(n_batch, n_pages // PAGES_PER_STEP),
            in_specs=[per_batch, per_batch] + [page_spec(u) for u in range(PAGES_PER_STEP)],
            out_specs=pl.BlockSpec((None, PAGES_PER_STEP, Q_PAD, PAGE), lambda b, s, pt: (b, s, 0, 0)),
        ),
        out_shape=jax.ShapeDtypeStruct((n_batch, n_pages, Q_PAD, PAGE), F32),
        compiler_params=_params("parallel", "arbitrary"),
    )(page_table, qi8, wb, *([cache_kidx] * PAGES_PER_STEP))


def _sample_select_kernel(sc_ref, qi_ref, wb_ref, kin_ref, sel_ref, key_ref, *, n_pages, dec_seq, n_sel):
    qrow = lax.broadcasted_iota(I32, (Q_PAD, PAGE), 0)
    lane = lax.broadcasted_iota(I32, (Q_PAD, PAGE), 1)
    s = jnp.maximum(_dot_nt(qi_ref[...].astype(BF16), kin_ref[...].astype(BF16)), 0.0) * wb_ref[...]
    acc = s[0:Q_PAD]
    for h in range(1, IDX_HEADS):
        acc = acc + s[h * Q_PAD:(h + 1) * Q_PAD]
    new_valid = jnp.where(lane < dec_seq, jnp.where(lane <= qrow, 1, 0), 0) > 0
    key_ref[0:n_pages] = sc_ref[...]
    key_ref[n_pages] = jnp.where(new_valid, acc, -jnp.inf)

    def lane_fold(x, combine, reduce_lanes):
        n_main = x.shape[0] // 8 * 8
        part = x[0:n_main].reshape(8, n_main // 8, Q_PAD, PAGE)
        for axis in (1, 0):
            acc_p = part[:, 0] if axis == 1 else part[0]
            for u in range(1, part.shape[axis]):
                acc_p = combine(acc_p, part[:, u] if axis == 1 else part[u])
            part = acc_p
        for extra in range(n_main, x.shape[0]):
            part = combine(part, x[extra])
        return reduce_lanes(part, axis=1, keepdims=True)

    lane_count = lambda x: lane_fold(x, jnp.add, jnp.sum)
    col1 = (Q_PAD, 1)
    keys = key_ref[...]
    lo = lane_fold(jnp.where(keys == -jnp.inf, jnp.inf, keys), jnp.minimum, jnp.min)
    hi = lane_fold(keys, jnp.maximum, jnp.max)
    floor = _kth_largest_floor(lambda t: lane_count(jnp.where(key_ref[...] >= t, 1.0, 0.0)), lo, hi, n_sel)
    thr = lane_fold(jnp.where(keys >= floor, keys, jnp.inf), jnp.minimum, jnp.min)
    need = n_sel - lane_count(jnp.where(keys > thr, 1.0, 0.0))
    shape3 = (n_pages + 1, Q_PAD, PAGE)
    kidx = lax.broadcasted_iota(I32, shape3, 0) * PAGE + lax.broadcasted_iota(I32, shape3, 2)
    cut = _tie_cutoff(
        lambda c: lane_count(jnp.where(key_ref[...] == thr, jnp.where(kidx < c, 1.0, 0.0), 0.0)),
        need, ((n_pages + 1) * PAGE).bit_length(), col1)
    chosen = jnp.where(keys > thr, 1.0, jnp.where(keys == thr, jnp.where(kidx < cut, 1.0, 0.0), 0.0))
    sel_ref[0:n_pages] = chosen[0:n_pages]
    sel_ref[n_pages] = jnp.where(new_valid, chosen[n_pages], 0.0)


def _sample_select(sc, qi8, wb, ki_new, dec_seq, n_sel):
    n_batch, n_pages = sc.shape[:2]
    kern = functools.partial(_sample_select_kernel, n_pages=n_pages, dec_seq=dec_seq, n_sel=n_sel)
    per_batch = lambda a: pl.BlockSpec((None,) + a.shape[1:], lambda b: (b,) + (0,) * (a.ndim - 1))
    return pl.pallas_call(
        kern,
        grid=(n_batch,),
        in_specs=[per_batch(sc), per_batch(qi8), per_batch(wb), per_batch(ki_new)],
        out_specs=pl.BlockSpec((None, n_pages + 1, Q_PAD, PAGE), lambda b: (b, 0, 0, 0)),
        out_shape=jax.ShapeDtypeStruct((n_batch, n_pages + 1, Q_PAD, PAGE), F32),
        scratch_shapes=[pltpu.VMEM((n_pages + 1, Q_PAD, PAGE), F32)],
        compiler_params=_params("parallel"),
    )(sc, qi8, wb, ki_new)


def _sample_attn_kernel(pt_ref, q_ref, sel_ref, selnew_ref, knew_ref, vnew_ref, bias_ref, *refs, n_steps):
    k_refs = refs[:PAGES_PER_STEP]
    v_refs = refs[PAGES_PER_STEP:2 * PAGES_PER_STEP]
    o_ref, m_ref, l_ref, acc_ref = refs[2 * PAGES_PER_STEP:]
    s_id = pl.program_id(1)
    rows = KV_HEADS * GROUP * Q_PAD
    per_kv = GROUP * Q_PAD
    q = (q_ref[...] * HEAD_DIM ** -0.5).astype(BF16)

    @pl.when(s_id == 0)
    def _():
        m_ref[...] = jnp.full(m_ref.shape, NEG_BIG, F32)
        l_ref[...] = jnp.zeros(l_ref.shape, F32)
        acc_ref[...] = jnp.zeros(acc_ref.shape, F32)

    def attend(k_pages, v_pages, sel_pages, bias_pages):
        s_blocks, sel_blocks = [], []
        for kp, sp, bp in zip(k_pages, sel_pages, bias_pages):
            kb = kp.astype(BF16)
            s = jnp.concatenate(
                [_dot(q[n * per_kv:(n + 1) * per_kv], kb[n * HEAD_DIM:(n + 1) * HEAD_DIM]) for n in range(KV_HEADS)],
                axis=0) + bp
            s_blocks.append(s)
            sel_blocks.append(jnp.concatenate([sp] * (KV_HEADS * GROUP), axis=0) > 0.5)
        s = jnp.concatenate(s_blocks, axis=1)
        sel = jnp.concatenate(sel_blocks, axis=1)
        s = jnp.where(sel, s, NEG_BIG)
        m_old = m_ref[...]
        m_new = jnp.maximum(m_old, jnp.max(s, axis=1, keepdims=True))
        alpha = jnp.exp(m_old - m_new)
        p = jnp.where(sel, jnp.exp(s - m_new), 0.0)
        l_ref[...] = alpha * l_ref[...] + jnp.sum(p, axis=1, keepdims=True)
        m_ref[...] = m_new
        pb = p.astype(BF16)
        pv = None
        for u, vp in enumerate(v_pages):
            vb = vp.astype(BF16)
            pu = pb[:, u * PAGE:(u + 1) * PAGE]
            part = jnp.concatenate(
                [_dot_nt(pu[n * per_kv:(n + 1) * per_kv], vb[n * HEAD_DIM:(n + 1) * HEAD_DIM]) for n in range(KV_HEADS)], axis=0)
            pv = part if pv is None else pv + part
        acc_ref[...] = alpha * acc_ref[...] + pv

    @pl.when(s_id < n_steps)
    def _():
        far, near = bias_ref[0], bias_ref[1]
        biases = [far] * PAGES_PER_STEP
        last = s_id == n_steps - 1
        biases[-1] = jnp.where(last, near, far)
        attend([r[...] for r in k_refs], [r[...] for r in v_refs], [sel_ref[u] for u in range(PAGES_PER_STEP)], biases)

    @pl.when(s_id == n_steps)
    def _():
        attend([knew_ref[...]], [vnew_ref[...]], [selnew_ref[...]], [bias_ref[2]])
        o_ref[...] = acc_ref[...] / l_ref[...]


def _sample_attn(page_table, q8, sel, k_new, v_new, bias_tiles, cache_k, cache_v):
    n_batch, n_pages = page_table.shape
    n_steps = n_pages // PAGES_PER_STEP
    rows = KV_HEADS * GROUP * Q_PAD
    kern = functools.partial(_sample_attn_kernel, n_steps=n_steps)

    def page_spec(u):
        return pl.BlockSpec((None, KV_WIDTH, PAGE),
                            lambda b, s, pt: (pt[b, jnp.minimum(s, n_steps - 1) * PAGES_PER_STEP + u], 0, 0))

    per_batch = lambda a: pl.BlockSpec((None,) + a.shape[1:], lambda b, s, pt: (b,) + (0,) * (a.ndim - 1))
    return pl.pallas_call(
        kern,
        grid_spec=pltpu.PrefetchScalarGridSpec(
            num_scalar_prefetch=1,
            grid=(n_batch, n_steps + 1),
            in_specs=[
                per_batch(q8),
                pl.BlockSpec((None, PAGES_PER_STEP, Q_PAD, PAGE), lambda b, s, pt: (b, jnp.minimum(s, n_steps - 1), 0, 0)),
                pl.BlockSpec((None, None, Q_PAD, PAGE), lambda b, s, pt: (b, n_pages, 0, 0)),
                per_batch(k_new), per_batch(v_new),
                pl.BlockSpec(bias_tiles.shape, lambda b, s, pt: (0, 0, 0)),
            ] + [page_spec(u) for u in range(PAGES_PER_STEP)] * 2,
            out_specs=pl.BlockSpec((None, rows, HEAD_DIM), lambda b, s, pt: (b, 0, 0)),
            scratch_shapes=[pltpu.VMEM((rows, 1), F32), pltpu.VMEM((rows, 1), F32), pltpu.VMEM((rows, HEAD_DIM), F32)],
        ),
        out_shape=jax.ShapeDtypeStruct((n_batch, rows, HEAD_DIM), F32),
        compiler_params=_params("parallel", "arbitrary"),
    )(page_table, q8, sel, sel, k_new, v_new, bias_tiles, *([cache_k] * PAGES_PER_STEP), *([cache_v] * PAGES_PER_STEP))


def _mix_kernel(ysp_ref, yst_ref, bonusp_ref, bonust_ref, gp_ref, gt_ref, yap_ref, yat_ref,
                xn_ref, gng_ref, gnb_ref, ones_ref, wor_ref, woa_ref, lg_ref, lb_ref,
                x1_ref, x1b_ref, x1t_ref, *, n_prompt_tiles):
    is_prompt = pl.program_id(0) < n_prompt_tiles
    pick = lambda p_ref, t_ref: jnp.where(is_prompt, p_ref[...], t_ref[...])
    ones_bd = ones_ref[...]
    ys = pick(ysp_ref, yst_ref)
    inv = 1.0 / HEAD_DIM
    yc = ys - _segsum(ys, ones_bd) * inv
    var = _segsum(yc * yc, ones_bd) * inv
    yr = (yc * lax.rsqrt(var + GN_EPS) * gng_ref[...] + gnb_ref[...] + pick(bonusp_ref, bonust_ref)) * pick(gp_ref, gt_ref)
    mix = _dot(yr.astype(BF16), wor_ref[...]) + _dot(pick(yap_ref, yat_ref).astype(BF16), woa_ref[...])
    x1 = _ln(DN_ALPHA * xn_ref[...] + mix, lg_ref[...], lb_ref[...])
    x1_ref[...] = x1
    x1b_ref[...] = x1.astype(BF16)
    x1t_ref[...] = x1.T.astype(BF16)


def _mix(ys, bonus, g, ya, xn, gn_g, gn_b, ones_bd, wo_r, wo_a, ln_g, ln_b, tm):
    rows = xn.shape[0]
    n_prompt_tiles = ys[0].shape[0] // tm
    half_p = pl.BlockSpec((tm, RWKV_WIDTH), lambda i: (jnp.minimum(i, n_prompt_tiles - 1), 0))
    half_t = pl.BlockSpec((tm, RWKV_WIDTH), lambda i: (jnp.maximum(i - n_prompt_tiles, 0), 0))
    row = pl.BlockSpec((tm, D_MODEL), lambda i: (i, 0))
    vec = lambda n: pl.BlockSpec((1, n), lambda i: (0, 0))
    full = lambda a: pl.BlockSpec(a.shape, lambda i: (0, 0))
    return pl.pallas_call(
        functools.partial(_mix_kernel, n_prompt_tiles=n_prompt_tiles),
        grid=(rows // tm,),
        in_specs=[half_p, half_t] * 4 + [row, vec(RWKV_WIDTH), vec(RWKV_WIDTH), full(ones_bd), full(wo_r), full(wo_a),
                                         vec(D_MODEL), vec(D_MODEL)],
        out_specs=[row, row, pl.BlockSpec((D_MODEL, tm), lambda i: (0, i))],
        out_shape=[jax.ShapeDtypeStruct((rows, D_MODEL), F32), jax.ShapeDtypeStruct((rows, D_MODEL), BF16),
                   jax.ShapeDtypeStruct((D_MODEL, rows), BF16)],
        compiler_params=_params("parallel"),
    )(*ys, *bonus, *g, *ya, xn, gn_g, gn_b, ones_bd, wo_r, wo_a, ln_g, ln_b)


CAND_PAIRS = [(c, d) for c in range(PEER_TOPK) for d in range(PEER_TOPK) if (c + 1) * (d + 1) <= PEER_TOPK]


def _top_rows(x, n):
    rows = []
    for _ in range(n):
        m = jnp.max(x, axis=0, keepdims=True)
        rows.append(m)
        x = jnp.where(x == m, -jnp.inf, x)
    return rows


def _peer_route_kernel(x_ref, wq_ref, sub_ref, tau_ref, f_ref, s2_ref, e2_ref):
    q = _dot(x_ref[...], wq_ref[...])
    for h in range(PEER_HEADS):
        base = h * 2 * PEER_HALF
        s1 = _dot_nt_hp(sub_ref[h, 0], q[:, base:base + PEER_HALF])
        s2 = _dot_nt_hp(sub_ref[h, 1], q[:, base + PEER_HALF:base + 2 * PEER_HALF])
        top1 = _top_rows(s1, PEER_TOPK)
        top2 = _top_rows(s2, PEER_TOPK)
        cand = jnp.concatenate([top1[c] + top2[d] for c, d in CAND_PAIRS]
                               + [jnp.full_like(top1[0], -jnp.inf)] * (-len(CAND_PAIRS) % 8), axis=0)
        best = _top_rows(cand, PEER_TOPK)
        thr = best[-1]
        m = top1[0] + top2[0]
        z = jnp.sum(jnp.where(cand >= thr, jnp.exp(cand - m), 0.0), axis=0, keepdims=True)
        tau = jnp.full(s1.shape, jnp.inf, F32)
        for d in range(PEER_TOPK):
            tau = jnp.where(s1 + top2[d] >= thr, top2[d], tau)
        tau_ref[h] = tau
        f_ref[h] = jnp.exp(s1 - top1[0]) * (1.0 / z)
        s2_ref[h] = s2
        e2_ref[h] = jnp.exp(s2 - top2[0])


def _peer_route(x1b, wq, subkeys):
    rows = x1b.shape[0]
    tm = LANE
    big = pl.BlockSpec((PEER_HEADS, PEER_NKEYS, tm), lambda i: (0, 0, i))
    big_shape = jax.ShapeDtypeStruct((PEER_HEADS, PEER_NKEYS, rows), F32)
    return pl.pallas_call(
        _peer_route_kernel,
        grid=(rows // tm,),
        in_specs=[pl.BlockSpec((tm, D_MODEL), lambda i: (i, 0)),
                  pl.BlockSpec(wq.shape, lambda i: (0, 0)),
                  pl.BlockSpec(subkeys.shape, lambda i: (0, 0, 0, 0))],
        out_specs=[big] * 4,
        out_shape=[big_shape] * 4,
        compiler_params=_params("parallel"),
    )(x1b, wq, subkeys)


EXPERT_ROWS = 8
MXU_DEPTH = 256


def _peer_dense_kernel(xt_ref, u_ref, vt_ref, tau_ref, f_ref, s2_ref, e2_ref, o_ref, *scratch):
    n_sub = EXPERT_ROWS * PEER_NKEYS // MXU_DEPTH
    ht_refs, g_refs, a_refs = scratch[:n_sub], scratch[n_sub:2 * n_sub], scratch[2 * n_sub:]
    j = pl.program_id(1)
    tm = xt_ref.shape[1]

    @pl.when(j == 0)
    def _():
        o_ref[...] = jnp.zeros(o_ref.shape, F32)

    per = MXU_DEPTH // PEER_NKEYS
    half = PEER_NKEYS // 2

    def gate_rows(p):
        return [([tau_ref[h, e:e + 1, :] for h in range(PEER_HEADS)], [f_ref[h, e:e + 1, :] for h in range(PEER_HEADS)])
                for e in range(p * per, (p + 1) * per)]

    def gate_block(p, rows_p, c, jh):
        cols = slice(c, c + LANE)
        jr = slice(jh * half, (jh + 1) * half)
        acc = [None] * per
        for h in range(PEER_HEADS):
            s2, e2 = s2_ref[h, jr, cols], e2_ref[h, jr, cols]
            for e in range(per):
                tau_i, f_i = rows_p[e]
                g = jnp.where(s2 >= tau_i[h][:, cols], e2 * f_i[h][:, cols], 0.0)
                acc[e] = g if acc[e] is None else acc[e] + g
        for e in range(per):
            g_refs[p][e * PEER_NKEYS + jh * half:e * PEER_NKEYS + (jh + 1) * half, cols] = acc[e]

    def gate_blocks(p):
        rows_p = gate_rows(p)
        return [functools.partial(gate_block, p, rows_p, c, jh) for c in range(0, tm, LANE) for jh in range(2)]

    def up_piece(p, k):
        r = p * MXU_DEPTH + k * PEER_NKEYS
        ht_refs[p][k * PEER_NKEYS:(k + 1) * PEER_NKEYS, :] = _dot(u_ref[r:r + PEER_NKEYS, :], xt_ref[...])

    def act(p):
        he = ht_refs[p][...]
        a_refs[p][...] = (0.5 * he * (1.0 + lax.erf(he * (2.0 ** -0.5))) * g_refs[p][...]).astype(BF16)

    down_rows = 512

    def down_piece(p, m):
        r = p * MXU_DEPTH
        rows = slice(m * down_rows, (m + 1) * down_rows)
        o_ref[rows, :] += _dot(vt_ref[rows, r:r + MXU_DEPTH], a_refs[p][...])

    def run(pieces):
        for piece in pieces:
            piece()

    ups = lambda p: [functools.partial(up_piece, p, k) for k in range(per)]
    downs = lambda p: [functools.partial(down_piece, p, m) for m in range(D_MODEL // down_rows)]
    run(gate_blocks(0) + ups(0))
    for p in range(n_sub):
        if p + 1 < n_sub:
            run(gate_blocks(p + 1) + ups(p + 1))
        act(p)
        run(downs(p))


def _peer_dense(x1t, u, vt, tau, f, s2, e2, tm):
    rows = x1t.shape[1]
    eb = EXPERT_ROWS * PEER_NKEYS
    step_rows = pl.BlockSpec((PEER_HEADS, EXPERT_ROWS, tm), lambda i, j: (0, j, i))
    big = pl.BlockSpec((PEER_HEADS, PEER_NKEYS, tm), lambda i, j: (0, 0, i))
    return pl.pallas_call(
        _peer_dense_kernel,
        grid=(rows // tm, PEER_EXPERTS // eb),
        in_specs=[pl.BlockSpec((D_MODEL, tm), lambda i, j: (0, i)),
                  pl.BlockSpec((eb, D_MODEL), lambda i, j: (j, 0)),
                  pl.BlockSpec((D_MODEL, eb), lambda i, j: (0, j)),
                  step_rows, step_rows, big, big],
        out_specs=pl.BlockSpec((D_MODEL, tm), lambda i, j: (0, i)),
        out_shape=jax.ShapeDtypeStruct((D_MODEL, rows), F32),
        scratch_shapes=([pltpu.VMEM((MXU_DEPTH, tm), F32)] * (2 * eb // MXU_DEPTH)
                        + [pltpu.VMEM((MXU_DEPTH, tm), BF16)] * (eb // MXU_DEPTH)),
        compiler_params=_params("parallel", "arbitrary"),
    )(x1t, u, vt, tau, f, s2, e2)


def _rel_buckets(dist):
    max_exact = REL_BUCKETS // 2
    d = np.maximum(dist, 0)
    ratio = np.log(np.maximum(d, 1).astype(np.float32) / np.float32(max_exact)) / np.float32(math.log(REL_MAX_DIST / max_exact))
    log_b = max_exact + (ratio * np.float32(REL_BUCKETS - max_exact)).astype(np.int32)
    return np.where(d < max_exact, d, np.minimum(log_b, REL_BUCKETS - 1)).astype(np.int32)


def _bias_lookup(rel_bias, dist):
    onehot = np.eye(REL_BUCKETS, dtype=np.float32)[_rel_buckets(dist)]
    return jnp.dot(jnp.asarray(onehot), rel_bias.astype(F32), precision=lax.Precision.HIGHEST)


def _prompt_bias_tiles(rel_bias):
    kk = np.arange(LANE)[:, None]
    qq = np.arange(LANE)[None, :]
    tiles = []
    for delta in range(3):
        b = _bias_lookup(rel_bias, delta * LANE + qq - kk)
        b = b.reshape(LANE, LANE, KV_HEADS, GROUP).transpose(2, 0, 3, 1).reshape(KV_HEADS, LANE, GROUP * LANE)
        tiles.append(b)
    return jnp.stack(tiles, axis=1).astype(F32)


def _sample_bias_tiles(rel_bias, past_len, dec_seq):
    q = np.arange(Q_PAD)[:, None]
    off = np.arange(PAGE)[None, :]
    qpos = past_len + np.minimum(q, dec_seq - 1)
    dists = [qpos - 0 * off - (past_len - 2 * PAGE), qpos - (past_len - PAGE + off), qpos - (past_len + np.minimum(off, dec_seq - 1))]
    tiles = []
    for d in dists:
        b = _bias_lookup(rel_bias, d + 0 * off)
        tiles.append(b.transpose(2, 0, 1).reshape(ATTN_HEADS * Q_PAD, PAGE))
    return jnp.stack(tiles).astype(F32)


def kernel(x_prompt, x_sample, cache_k, cache_v, cache_kidx, state_wkv, state_shift, page_table, meta_tokens, ln_in_g, ln_in_b, rel_bias, w_in, mu_shift, w0, w_up, a0, a_up, g_up, k_k, k_a, r_k, gn_g, gn_b, w_o, ln1_g, ln1_b, peer_wq, peer_subkeys, peer_u, peer_v, ln2_g, ln2_b):
    n_batch, seq, _ = x_prompt.shape
    dec_batch, dec_seq, _ = x_sample.shape
    n_pages = page_table.shape[1]
    past_len = n_pages * PAGE
    t_len = seq + N_META
    n_blocks = -(-t_len // LANE)
    t_pad = n_blocks * LANE
    rows_p = n_batch * t_pad
    rows_s = dec_batch * dec_seq
    assert rows_s == LANE and Q_PAD >= dec_seq and n_pages % PAGES_PER_STEP == 0
    assert past_len >= 2 * PAGE + REL_MAX_DIST
    rows = -(-(rows_p + rows_s) // ROW_ALIGN) * ROW_ALIGN
    layer = 0

    meta = jnp.broadcast_to(meta_tokens[None], (n_batch, N_META, D_MODEL))
    xp = jnp.pad(jnp.concatenate([meta, x_prompt], axis=1), ((0, 0), (0, t_pad - t_len), (0, 0)))
    x_all = jnp.concatenate([xp.reshape(rows_p, D_MODEL), x_sample.reshape(rows_s, D_MODEL),
                             jnp.zeros((rows - rows_p - rows_s, D_MODEL), F32)], axis=0)
    xn, xb = _ln_in(x_all, ln_in_g, ln_in_b, 256)

    w = w_in[layer]
    c0 = RWKV_COLS
    w_rwkv = jnp.pad(w[:, :c0], ((0, 0), (0, RWKV_PAD - RWKV_COLS))).astype(BF16)
    w_qq = jnp.concatenate([w[:, c0:c0 + ATTN_WIDTH], w[:, c0 + ATTN_WIDTH + 2 * KV_WIDTH:c0 + 2 * ATTN_WIDTH + 2 * KV_WIDTH]], axis=1).astype(BF16)
    c_ki = c0 + 2 * ATTN_WIDTH + 2 * KV_WIDTH
    w_kvi = jnp.concatenate([w[:, c0 + ATTN_WIDTH:c0 + ATTN_WIDTH + 2 * KV_WIDTH], w[:, c_ki:c_ki + IDX_DIM],
                             jnp.pad(w[:, c_ki + IDX_DIM:], ((0, 0), (0, LANE - IDX_HEADS)))], axis=1).astype(BF16)
    feat = _matmul(xb, w_rwkv, 640, RWKV_PAD // 3)
    qq = _matmul(xb, w_qq, 640, 1024)
    kvi = _matmul(xb, w_kvi, 640, w_kvi.shape[1])

    def prompt_rows(a):
        return a[:rows_p].reshape(n_batch, t_pad, -1)[:, :t_len]

    def sample_rows(a):
        return a[rows_p:rows_p + rows_s].reshape(dec_batch, dec_seq, -1)

    ones_bd = jnp.asarray(np.kron(np.eye(RWKV_HEADS), np.ones((HEAD_DIM, HEAD_DIM))), BF16)
    pad_cols = lambda a: jnp.pad(a, ((0, 0), (0, RWKV_PAD - RWKV_COLS)))
    init = jnp.zeros((dec_batch, dec_seq, RWKV_PAD), F32).at[:, 0].set(pad_cols(state_shift[layer]))
    init = jnp.concatenate([jnp.zeros((LANE, RWKV_PAD), F32), init.reshape(rows_s, RWKV_PAD)], axis=0)
    wup = jnp.pad(w_up[layer], ((0, ICLR_LORA), (0, 0)))
    aup = jnp.pad(a_up[layer], ((DECAY_LORA, 0), (0, 0)))
    gup = jnp.pad(g_up[layer], ((0, GATE_PAD - GATE_LORA), (0, 0)))
    vec = lambda a: a.reshape(1, -1)
    pre = _rwkv_pre(feat, init, vec(pad_cols(mu_shift[layer][None])), vec(w0[layer]), vec(a0[layer]), vec(k_k[layer]),
                    vec(k_a[layer]), vec(r_k[layer]), wup, aup, gup, ones_bd, rows_p // LANE, n_blocks, dec_seq)
    pre_p, pre_t = pre

    def to_scan(a, nb, steps):
        return a.reshape(nb, steps, RWKV_HEADS, HEAD_DIM).transpose(1, 3, 0, 2).reshape(steps, HEAD_DIM, nb * RWKV_HEADS)

    take_p = lambda a: a.reshape(n_batch, t_pad, -1)
    take_s = lambda a: a[:rows_s].reshape(dec_batch, dec_seq, -1)

    def from_scan(y, nb, steps):
        return y.reshape(steps, HEAD_DIM, nb, RWKV_HEADS).transpose(2, 0, 3, 1).reshape(nb, steps, RWKV_WIDTH)

    def state_in(s):
        nb = s.shape[0]
        return s.transpose(2, 3, 0, 1).reshape(HEAD_DIM, HEAD_DIM, nb * RWKV_HEADS)

    def state_out(s, nb):
        return s.reshape(HEAD_DIM, HEAD_DIM, nb, RWKV_HEADS).transpose(2, 3, 0, 1)

    tc = max(d for d in range(1, 49) if math.gcd(t_len, t_pad) % d == 0)
    y_p, wkv_p = _rwkv_scan(*_to_scan_rows([take_p(a) for a in pre_p[:6]], tc),
                            jnp.zeros((HEAD_DIM, HEAD_DIM, n_batch * RWKV_HEADS), F32), tc, t_len)
    y_p = y_p.reshape(t_pad, HEAD_DIM, RWKV_HEADS, n_batch).transpose(3, 0, 2, 1).reshape(rows_p, RWKV_WIDTH)
    wkv_p = wkv_p.reshape(HEAD_DIM, HEAD_DIM, RWKV_HEADS, n_batch).transpose(3, 2, 0, 1)
    y_s, wkv_s = _rwkv_scan(*[to_scan(take_s(a), dec_batch, dec_seq) for a in pre_t[:6]],
                            state_in(state_wkv[layer]), dec_seq, dec_seq)
    rows_t = rows - rows_p
    pad_tail = lambda a: jnp.pad(a, ((0, rows_t - rows_s), (0, 0)))
    ys = (y_p, pad_tail(from_scan(y_s, dec_batch, dec_seq).reshape(rows_s, RWKV_WIDTH)))

    ya_p = _prompt_attn(qq, kvi, _prompt_bias_tiles(rel_bias), n_batch, n_blocks, min(IDX_TOPK, t_len // 4))

    qq_s, kvi_s = sample_rows(qq), sample_rows(kvi)

    def pad_q(a):
        a = jnp.pad(a.transpose(0, 2, 1, 3), ((0, 0), (0, 0), (0, Q_PAD - dec_seq), (0, 0)))
        return a.reshape(dec_batch, -1, a.shape[-1])

    qi8 = pad_q(qq_s[..., ATTN_WIDTH:].reshape(dec_batch, dec_seq, IDX_HEADS, IDX_DIM))
    wi_s = kvi_s[..., 2 * KV_WIDTH + IDX_DIM:2 * KV_WIDTH + IDX_DIM + IDX_HEADS]
    wb = jnp.broadcast_to(pad_q(wi_s[..., None]), (dec_batch, IDX_HEADS * Q_PAD, IDX_DIM))
    q8 = pad_q(qq_s[..., :ATTN_WIDTH].reshape(dec_batch, dec_seq, ATTN_HEADS, HEAD_DIM))
    pad_keys = lambda a: jnp.pad(a, ((0, 0), (0, PAGE - dec_seq), (0, 0)))
    ki_new = pad_keys(kvi_s[..., 2 * KV_WIDTH:2 * KV_WIDTH + IDX_DIM])
    k_new = pad_keys(kvi_s[..., :KV_WIDTH])
    v_new = pad_keys(kvi_s[..., KV_WIDTH:2 * KV_WIDTH])
    n_pool = cache_k.shape[1]
    pages_t = lambda c: c[layer].transpose(0, 2, 3, 1).reshape(n_pool, KV_WIDTH, PAGE)
    sc = _sample_scores(page_table, qi8, wb, cache_kidx[layer])
    sel = _sample_select(sc, qi8, wb, ki_new, dec_seq, min(IDX_TOPK, (past_len + dec_seq) // 4))
    o_s = _sample_attn(page_table, q8, sel, k_new.transpose(0, 2, 1), v_new.transpose(0, 2, 1),
                       _sample_bias_tiles(rel_bias, past_len, dec_seq), pages_t(cache_k), pages_t(cache_v))
    ya_s = o_s.reshape(dec_batch, ATTN_HEADS, Q_PAD, HEAD_DIM)[:, :, :dec_seq].transpose(0, 2, 1, 3).reshape(rows_s, ATTN_WIDTH)

    wo = w_o[layer].astype(BF16)
    x1, x1b, x1t = _mix(ys, (pre_p[7], pre_t[7]), (pre_p[6], pre_t[6]), (ya_p, pad_tail(ya_s)), xn,
                        vec(gn_g[layer]), vec(gn_b[layer]), ones_bd,
                        wo[:RWKV_WIDTH], wo[RWKV_WIDTH:], vec(ln1_g[layer]), vec(ln1_b[layer]), 256)
    routing = _peer_route(x1b, peer_wq[layer].astype(BF16), peer_subkeys[layer])
    peer_t = _peer_dense(x1t, peer_u[layer].astype(BF16), peer_v[layer].T.astype(BF16), *routing, 512)
    y_p, y_t = _ln_out(x1, peer_t, ln2_g[layer], ln2_b[layer], 256, rows_p)

    last_p = feat[jnp.arange(n_batch) * t_pad + (t_len - 1)]
    last_s = feat[rows_p + jnp.arange(dec_batch) * dec_seq + (dec_seq - 1)]
    kvi_p = prompt_rows(kvi)
    kv4 = lambda a, nb, steps: a.reshape(nb, steps, KV_HEADS, HEAD_DIM)[None]
    return (
        y_p.reshape(n_batch, t_pad, D_MODEL)[:, N_META:t_len], y_t[:rows_s].reshape(dec_batch, dec_seq, D_MODEL),
        kv4(kvi_p[..., :KV_WIDTH], n_batch, t_len), kv4(kvi_p[..., KV_WIDTH:2 * KV_WIDTH], n_batch, t_len),
        kvi_p[..., 2 * KV_WIDTH:2 * KV_WIDTH + IDX_DIM][None],
        wkv_p[None], last_p[:, :RWKV_COLS][None],
        kv4(kvi_s[..., :KV_WIDTH], dec_batch, dec_seq), kv4(kvi_s[..., KV_WIDTH:2 * KV_WIDTH], dec_batch, dec_seq),
        kvi_s[..., 2 * KV_WIDTH:2 * KV_WIDTH + IDX_DIM][None],
        state_out(wkv_s, dec_batch)[None], last_s[:, :RWKV_COLS][None],
    )
```
